```python
import math
import jax, jax.numpy as jnp
from jax import lax
import numpy as np

D_MODEL = 1024
BATCH = 16
SEQ = 2048
DEPTH = 2
DEC_BATCH = 32
DEC_SEQ = 4
PAST_LEN = 16384
PAGE_SIZE = 128

N_MIXERS = 2
N_CONV_LAYERS = (DEPTH + 1) // 2
N_NSA_LAYERS = DEPTH // 2
D_FF = 2816
CONV_W = 3
N_HEADS = 16
HEAD_DIM = D_MODEL // N_HEADS
N_KV_HEADS = 4
GROUP = N_HEADS // N_KV_HEADS
KV_W = N_KV_HEADS * HEAD_DIM
NSA_IN = N_HEADS * HEAD_DIM + 6 * KV_W + 3 * N_HEADS
L_CMP = 32
L_SEL = 64
TOP_N = 16
WINDOW = 512
Q_CHUNK = 16
NORM_EPS = 1e-6
FORCE_BONUS = 1e4
NEG = -1e30

kernel_name = "conv_nsa_macaron_hybrid_step"


def rmsnorm(x, g):
    xf = x.astype(jnp.float32)
    y = xf * lax.rsqrt(jnp.mean(xf * xf, axis=-1, keepdims=True) + NORM_EPS)
    return (y * g.astype(jnp.float32)).astype(x.dtype)


def swiglu(x, w_gu, w_down):
    g, u = jnp.split(x @ w_gu, 2, axis=-1)
    return (jax.nn.silu(g) * u) @ w_down


def short_conv_mixer(h, conv_state, w_in, w_conv, w_out):
    bg, cg, v = jnp.split(h @ w_in, 3, axis=-1)
    u = cg * v
    u_ext = jnp.concatenate([conv_state.astype(u.dtype), u], axis=1)
    T = h.shape[1]
    conv = w_conv[0] * u_ext[:, 0:T]
    for i in range(1, CONV_W):
        conv = conv + w_conv[i] * u_ext[:, i:i + T]
    y = (bg * conv) @ w_out
    return y, u_ext[:, -(CONV_W - 1):]


def nsa_project(h, w_in):
    B, T, _ = h.shape
    splits = np.cumsum([N_HEADS * HEAD_DIM] + [KV_W] * 6).tolist()
    p = jnp.split(h @ w_in, splits, axis=-1)
    q = p[0].reshape(B, T, N_KV_HEADS, GROUP, HEAD_DIM)

    def kv(a, b):
        return jnp.stack([a, b], axis=2).reshape(B, T, 2, N_KV_HEADS, HEAD_DIM)

    gates = jax.nn.sigmoid(p[7]).reshape(B, T, 3, N_KV_HEADS, GROUP)
    return q, kv(p[1], p[2]), kv(p[3], p[4]), kv(p[5], p[6]), gates


def compress(rows, pe, w1k, w2k, w1v, w2v):
    B, T = rows.shape[:2]
    blk = rows.reshape(B, T // L_CMP, L_CMP, 2, N_KV_HEADS, HEAD_DIM) + pe[:, None, None, :]
    k = jax.nn.gelu(jnp.einsum('bnlkd,lde->bnke', blk[:, :, :, 0], w1k)) @ w2k
    v = jax.nn.gelu(jnp.einsum('bnlkd,lde->bnke', blk[:, :, :, 1], w1v)) @ w2v
    return k, v


def nsa_attend(q, t_pos, gates, k_cmp, v_cmp, gather_sel, kv_win, s_win):
    B, Tq = q.shape[:2]
    scale = HEAD_DIM ** -0.5
    nb_c = k_cmp.shape[1]
    blk_end = (jnp.arange(nb_c) + 1) * L_CMP - 1
    m_c = (blk_end[None, :] <= t_pos[:, None])[None, :, None, None, :]
    s_c = jnp.einsum('bqkgd,bnkd->bqkgn', q, k_cmp).astype(jnp.float32) * scale
    p_c = jax.nn.softmax(jnp.where(m_c, s_c, NEG), axis=-1) * m_c
    o_c = jnp.einsum('bqkgn,bnkd->bqkgd', p_c.astype(q.dtype), v_cmp)
    ratio = L_SEL // L_CMP
    nb_s = nb_c // ratio
    imp = p_c.sum(axis=3).reshape(B, Tq, N_KV_HEADS, nb_s, ratio).sum(-1)
    blk = jnp.arange(nb_s)[None, :]
    cur = (t_pos // L_SEL)[:, None]
    forced = ((blk == 0) | (blk == cur) | (blk == cur - 1))[None, :, None, :]
    valid = (blk * L_SEL <= t_pos[:, None])[None, :, None, :]
    score = jnp.where(forced, FORCE_BONUS, jnp.where(valid, imp, -FORCE_BONUS))
    _, idx = lax.top_k(score, min(TOP_N, nb_s))
    k_sel, v_sel = gather_sel(idx)
    key_pos = idx[..., None] * L_SEL + jnp.arange(L_SEL)
    m_s = (key_pos <= t_pos[None, :, None, None, None])[:, :, :, None]
    s_s = jnp.einsum('bqkgd,bqkjld->bqkgjl', q, k_sel).astype(jnp.float32) * scale
    s_s = jnp.where(m_s, s_s, NEG)
    n_sel = s_s.shape[4]
    p_s = jax.nn.softmax(s_s.reshape(B, Tq, N_KV_HEADS, GROUP, n_sel * L_SEL), axis=-1)
    p_s = p_s.reshape(B, Tq, N_KV_HEADS, GROUP, n_sel, L_SEL)
    o_s = jnp.einsum('bqkgjl,bqkjld->bqkgd', p_s.astype(q.dtype), v_sel)
    d = t_pos[:, None] - s_win[None, :]
    m_w = ((d >= 0) & (d < WINDOW) & (s_win[None, :] >= 0))[None, :, None, None, :]
    s_w = jnp.einsum('bqkgd,bskd->bqkgs', q, kv_win[:, :, 0]).astype(jnp.float32) * scale
    p_w = jax.nn.softmax(jnp.where(m_w, s_w, NEG), axis=-1)
    o_w = jnp.einsum('bqkgs,bskd->bqkgd', p_w.astype(q.dtype), kv_win[:, :, 1])
    o = gates[:, :, 0, ..., None] * o_c + gates[:, :, 1, ..., None] * o_s + gates[:, :, 2, ..., None] * o_w
    return o.reshape(B, Tq, N_HEADS * HEAD_DIM)


def nsa_prompt(h, w_in, pe, w1k, w2k, w1v, w2v, w_out):
    B, T = h.shape[:2]
    q, kv_c, kv_s, kv_w, gates = nsa_project(h, w_in)
    k_c, v_c = compress(kv_c, pe, w1k, w2k, w1v, w2v)
    sel_blocks = kv_s.reshape(B, T // L_SEL, L_SEL, 2, N_KV_HEADS, HEAD_DIM)
    win_pad = jnp.pad(kv_w, ((0, 0), (WINDOW, 0), (0, 0), (0, 0), (0, 0)))
    b_idx = jnp.arange(B)[:, None, None, None]
    kv_idx = jnp.arange(N_KV_HEADS)[None, None, :, None]

    def gather_sel(idx):
        blk = sel_blocks[b_idx, idx, :, :, kv_idx, :]
        return blk[..., 0, :], blk[..., 1, :]

    def chunk(c):
        start = c * Q_CHUNK
        qc = lax.dynamic_slice_in_dim(q, start, Q_CHUNK, axis=1)
        gc = lax.dynamic_slice_in_dim(gates, start, Q_CHUNK, axis=1)
        t_pos = start + jnp.arange(Q_CHUNK)
        kvw = lax.dynamic_slice_in_dim(win_pad, start, WINDOW + Q_CHUNK, axis=1)
        s_win = start - WINDOW + jnp.arange(WINDOW + Q_CHUNK)
        return nsa_attend(qc, t_pos, gc, k_c, v_c, gather_sel, kvw, s_win)

    o = lax.map(chunk, jnp.arange(T // Q_CHUNK))
    o = o.transpose(1, 0, 2, 3).reshape(B, T, N_HEADS * HEAD_DIM)
    return o @ w_out, kv_c, kv_s, kv_w[:, -min(WINDOW, T):]


def nsa_sample(h, cache_cmp, cache_sel, cache_win, page_table, w_in, pe, w1k, w2k, w1v, w2v, w_out):
    B, T = h.shape[:2]
    q, kv_c, kv_s, kv_w, gates = nsa_project(h, w_in)
    n_pages = page_table.shape[1]
    past = n_pages * PAGE_SIZE
    t_pos = past + jnp.arange(T)
    n_new_blk = -(-T // L_SEL)
    pad = ((0, 0), (0, n_new_blk * L_SEL - T), (0, 0), (0, 0), (0, 0))
    past_c = cache_cmp[page_table].reshape(B, past, 2, N_KV_HEADS, HEAD_DIM)
    rows_c = jnp.concatenate([past_c, jnp.pad(kv_c, pad).astype(past_c.dtype)], axis=1)
    k_c, v_c = compress(rows_c, pe, w1k, w2k, w1v, w2v)
    spp = PAGE_SIZE // L_SEL
    pool_blocks = cache_sel.reshape(-1, L_SEL, 2, N_KV_HEADS, HEAD_DIM)
    new_blocks = jnp.pad(kv_s, pad).reshape(B, n_new_blk, L_SEL, 2, N_KV_HEADS, HEAD_DIM)
    n_past_blk = past // L_SEL
    b_idx = jnp.arange(B)[:, None, None, None]
    kv_idx = jnp.arange(N_KV_HEADS)[None, None, :, None]

    def gather_sel(idx):
        from_past = (idx < n_past_blk)[..., None, None, None]
        ip = jnp.minimum(idx, n_past_blk - 1)
        phys = page_table[b_idx, ip // spp] * spp + ip % spp
        blk_p = pool_blocks[phys, :, :, kv_idx, :]
        jn = jnp.clip(idx - n_past_blk, 0, n_new_blk - 1)
        blk_n = new_blocks[b_idx, jn, :, :, kv_idx, :]
        blk = jnp.where(from_past, blk_p, blk_n.astype(blk_p.dtype))
        return blk[..., 0, :], blk[..., 1, :]

    wb = cache_win.shape[1]
    kvw = jnp.concatenate([cache_win.astype(kv_w.dtype), kv_w], axis=1)
    s_win = past - wb + jnp.arange(wb + T)
    o = nsa_attend(q, t_pos, gates, k_c, v_c, gather_sel, kvw, s_win)
    return o @ w_out, kv_c, kv_s, kvw[:, -wb:]


def setup_inputs(seed: int = 0) -> dict:
    key = jax.random.key(seed)
    ks = iter(jax.random.split(key, 32))
    n_pages = PAST_LEN // PAGE_SIZE
    n_used = DEC_BATCH * n_pages
    n_phys = (n_used * 5) // 4
    wb = min(WINDOW, PAST_LEN)

    def nrm(shape, scale):
        return jax.random.normal(next(ks), shape, jnp.float32) * scale

    def gain(shape):
        return 1.0 + nrm(shape, 0.02)

    page_table = jax.random.permutation(next(ks), n_phys)[:n_used].astype(jnp.int32).reshape(DEC_BATCH, n_pages)
    return {
        "x_prompt": nrm((BATCH, SEQ, D_MODEL), 1.0),
        "x_sample": nrm((DEC_BATCH, DEC_SEQ, D_MODEL), 1.0),
        "state_conv": nrm((N_CONV_LAYERS, DEC_BATCH, CONV_W - 1, D_MODEL), 1.0),
        "cache_cmp_kv": nrm((N_NSA_LAYERS, n_phys, PAGE_SIZE, 2, N_KV_HEADS, HEAD_DIM), 1.0),
        "cache_sel_kv": nrm((N_NSA_LAYERS, n_phys, PAGE_SIZE, 2, N_KV_HEADS, HEAD_DIM), 1.0),
        "cache_win_kv": nrm((N_NSA_LAYERS, DEC_BATCH, wb, 2, N_KV_HEADS, HEAD_DIM), 1.0),
        "page_table": page_table,
        "norm_ffa": gain((DEPTH, D_MODEL)),
        "w_ffa_gu": nrm((DEPTH, D_MODEL, 2 * D_FF), D_MODEL ** -0.5),
        "w_ffa_down": nrm((DEPTH, D_FF, D_MODEL), D_FF ** -0.5),
        "norm_mix": gain((DEPTH, D_MODEL)),
        "norm_ffb": gain((DEPTH, D_MODEL)),
        "w_ffb_gu": nrm((DEPTH, D_MODEL, 2 * D_FF), D_MODEL ** -0.5),
        "w_ffb_down": nrm((DEPTH, D_FF, D_MODEL), D_FF ** -0.5),
        "w_conv_in": nrm((N_CONV_LAYERS, D_MODEL, 3 * D_MODEL), D_MODEL ** -0.5),
        "w_conv": nrm((N_CONV_LAYERS, CONV_W, D_MODEL), CONV_W ** -0.5),
        "w_conv_out": nrm((N_CONV_LAYERS, D_MODEL, D_MODEL), D_MODEL ** -0.5),
        "w_nsa_in": nrm((N_NSA_LAYERS, D_MODEL, NSA_IN), D_MODEL ** -0.5),
        "pe_cmp": nrm((N_NSA_LAYERS, L_CMP, HEAD_DIM), 0.1),
        "w_cmp_k1": nrm((N_NSA_LAYERS, L_CMP, HEAD_DIM, HEAD_DIM), (L_CMP * HEAD_DIM) ** -0.5),
        "w_cmp_k2": nrm((N_NSA_LAYERS, HEAD_DIM, HEAD_DIM), HEAD_DIM ** -0.5),
        "w_cmp_v1": nrm((N_NSA_LAYERS, L_CMP, HEAD_DIM, HEAD_DIM), (L_CMP * HEAD_DIM) ** -0.5),
        "w_cmp_v2": nrm((N_NSA_LAYERS, HEAD_DIM, HEAD_DIM), HEAD_DIM ** -0.5),
        "w_nsa_out": nrm((N_NSA_LAYERS, N_HEADS * HEAD_DIM, D_MODEL), (N_HEADS * HEAD_DIM) ** -0.5),
        "norm_final": gain((D_MODEL,)),
    }


def reference(x_prompt, x_sample, state_conv, cache_cmp_kv, cache_sel_kv, cache_win_kv, page_table,
              norm_ffa, w_ffa_gu, w_ffa_down, norm_mix, norm_ffb, w_ffb_gu, w_ffb_down,
              w_conv_in, w_conv, w_conv_out, w_nsa_in, pe_cmp, w_cmp_k1, w_cmp_k2, w_cmp_v1, w_cmp_v2,
              w_nsa_out, norm_final):
    xp, xs = x_prompt, x_sample
    conv_p, conv_s, cmp_p, cmp_s, sel_p, sel_s, win_p, win_s = [], [], [], [], [], [], [], []
    for i in range(DEPTH):
        xp = xp + 0.5 * swiglu(rmsnorm(xp, norm_ffa[i]), w_ffa_gu[i], w_ffa_down[i])
        xs = xs + 0.5 * swiglu(rmsnorm(xs, norm_ffa[i]), w_ffa_gu[i], w_ffa_down[i])
        hp = rmsnorm(xp, norm_mix[i])
        hs = rmsnorm(xs, norm_mix[i])
        j = i // N_MIXERS
        if i % N_MIXERS == 0:
            zero_state = jnp.zeros((hp.shape[0], CONV_W - 1, D_MODEL), hp.dtype)
            yp, st_p = short_conv_mixer(hp, zero_state, w_conv_in[j], w_conv[j], w_conv_out[j])
            ys, st_s = short_conv_mixer(hs, state_conv[j], w_conv_in[j], w_conv[j], w_conv_out[j])
            conv_p.append(st_p)
            conv_s.append(st_s)
        else:
            yp, c_p, s_p, w_p = nsa_prompt(hp, w_nsa_in[j], pe_cmp[j], w_cmp_k1[j], w_cmp_k2[j],
                                           w_cmp_v1[j], w_cmp_v2[j], w_nsa_out[j])
            ys, c_s, s_s, w_s = nsa_sample(hs, cache_cmp_kv[j], cache_sel_kv[j], cache_win_kv[j], page_table,
                                           w_nsa_in[j], pe_cmp[j], w_cmp_k1[j], w_cmp_k2[j],
                                           w_cmp_v1[j], w_cmp_v2[j], w_nsa_out[j])
            cmp_p.append(c_p)
            cmp_s.append(c_s)
            sel_p.append(s_p)
            sel_s.append(s_s)
            win_p.append(w_p)
            win_s.append(w_s)
        xp = xp + yp
        xs = xs + ys
        xp = xp + 0.5 * swiglu(rmsnorm(xp, norm_ffb[i]), w_ffb_gu[i], w_ffb_down[i])
        xs = xs + 0.5 * swiglu(rmsnorm(xs, norm_ffb[i]), w_ffb_gu[i], w_ffb_down[i])
    y_prompt = rmsnorm(xp, norm_final)
    y_sample = rmsnorm(xs, norm_final)
    return (y_prompt, y_sample,
            jnp.stack(conv_p), jnp.stack(conv_s),
            jnp.stack(cmp_p), jnp.stack(cmp_s),
            jnp.stack(sel_p), jnp.stack(sel_s),
            jnp.stack(win_p), jnp.stack(win_s))
```

```python
import functools

import jax
import jax.numpy as jnp
from jax import lax
from jax.experimental import pallas as pl
from jax.experimental.pallas import tpu as pltpu

D_MODEL = 1024
D_FF = 2816
CONV_W = 3
N_HEADS = 16
HEAD_DIM = 64
N_KV_HEADS = 4
GROUP = N_HEADS // N_KV_HEADS
KV_W = N_KV_HEADS * HEAD_DIM
L_CMP = 32
L_SEL = 64
TOP_N = 16
WINDOW = 512
PAGE_SIZE = 128
NORM_EPS = 1e-6
FORCE_BONUS = 1e4
NEG = -1e30

LANES = 128
DUP_W = N_KV_HEADS * LANES
VMEM_LIMIT = 56 * 1024 * 1024

F32 = jnp.float32
BF16 = jnp.bfloat16


def _params(*sem):
    return pltpu.CompilerParams(dimension_semantics=sem, vmem_limit_bytes=VMEM_LIMIT)


def _rms(x, g):
    return x * lax.rsqrt(jnp.mean(x * x, axis=-1, keepdims=True) + NORM_EPS) * g


def _dot(a, b):
    return jnp.dot(a, b, preferred_element_type=F32)


def _dot_nt(a, b):
    return lax.dot_general(a, b, (((1,), (1,)), ((), ())), preferred_element_type=F32)


def _dot_tn(a, b):
    return lax.dot_general(a, b, (((0,), (0,)), ((), ())), preferred_element_type=F32)


def _ffn_kernel(x_ref, g_ref, wg_ref, wu_ref, wd_ref, gf_ref, o_ref, xn_ref, acc_ref, *, n_ff, final):
    j = pl.program_id(1)

    @pl.when(j == 0)
    def _():
        xn_ref[...] = _rms(x_ref[...], g_ref[...]).astype(BF16)
        acc_ref[...] = jnp.zeros_like(acc_ref)

    xn = xn_ref[...]
    gate = _dot(xn, wg_ref[...])
    up = _dot(xn, wu_ref[...])
    act = (gate * jax.nn.sigmoid(gate) * up).astype(BF16)
    acc_ref[...] += _dot(act, wd_ref[...])

    @pl.when(j == n_ff - 1)
    def _():
        y = x_ref[...] + 0.5 * acc_ref[...]
        o_ref[...] = _rms(y, gf_ref[...]) if final else y


def _ffn(x, g, w_gu, w_down, g_final, *, tm, tf=1408, final=False):
    n = x.shape[0]
    n_ff = D_FF // tf
    return pl.pallas_call(
        functools.partial(_ffn_kernel, n_ff=n_ff, final=final),
        grid=(n // tm, n_ff),
        in_specs=[
            pl.BlockSpec((tm, D_MODEL), lambda i, j: (i, 0)),
            pl.BlockSpec((1, D_MODEL), lambda i, j: (0, 0)),
            pl.BlockSpec((D_MODEL, tf), lambda i, j: (0, j)),
            pl.BlockSpec((D_MODEL, tf), lambda i, j: (0, j + n_ff)),
            pl.BlockSpec((tf, D_MODEL), lambda i, j: (j, 0)),
            pl.BlockSpec((1, D_MODEL), lambda i, j: (0, 0)),
        ],
        out_specs=pl.BlockSpec((tm, D_MODEL), lambda i, j: (i, 0)),
        out_shape=jax.ShapeDtypeStruct((n, D_MODEL), F32),
        scratch_shapes=[pltpu.VMEM((tm, D_MODEL), BF16), pltpu.VMEM((tm, D_MODEL), F32)],
        compiler_params=_params("parallel", "arbitrary"),
        name="ffn_final" if final else "ffn",
    )(x, g.reshape(1, D_MODEL), w_gu, w_gu, w_down, g_final.reshape(1, D_MODEL))


CARRY = 8


def _conv_prompt_kernel(x_ref, g_ref, win_ref, wc_ref, wout_ref, o_ref, st_ref, uext_ref, *, tm):
    @pl.when(pl.program_id(1) == 0)
    def _():
        uext_ref[0:CARRY, :] = jnp.zeros((CARRY, D_MODEL), F32)

    x = x_ref[0]
    h = _rms(x, g_ref[...]).astype(BF16)
    p = _dot(h, win_ref[...])
    bg = p[:, :D_MODEL]
    u = p[:, D_MODEL:2 * D_MODEL] * p[:, 2 * D_MODEL:]
    uext_ref[CARRY:CARRY + tm, :] = u
    wc = wc_ref[...]
    conv = (wc[0:1] * uext_ref[CARRY - 2:CARRY - 2 + tm, :]
            + wc[1:2] * uext_ref[CARRY - 1:CARRY - 1 + tm, :]
            + wc[2:3] * u)
    y = _dot((bg * conv).astype(BF16), wout_ref[...])
    o_ref[0] = x + y
    st_ref[0] = uext_ref[CARRY + tm - 2:CARRY + tm, :]
    uext_ref[0:CARRY, :] = uext_ref[tm:tm + CARRY, :]


def _conv_prompt(x, g, w_in, w_conv, w_out, *, tm=512):
    b, t, _ = x.shape
    return pl.pallas_call(
        functools.partial(_conv_prompt_kernel, tm=tm),
        grid=(b, t // tm),
        in_specs=[
            pl.BlockSpec((1, tm, D_MODEL), lambda bi, ti: (bi, ti, 0)),
            pl.BlockSpec((1, D_MODEL), lambda bi, ti: (0, 0)),
            pl.BlockSpec((D_MODEL, 3 * D_MODEL), lambda bi, ti: (0, 0)),
            pl.BlockSpec((CONV_W, D_MODEL), lambda bi, ti: (0, 0)),
            pl.BlockSpec((D_MODEL, D_MODEL), lambda bi, ti: (0, 0)),
        ],
        out_specs=[
            pl.BlockSpec((1, tm, D_MODEL), lambda bi, ti: (bi, ti, 0)),
            pl.BlockSpec((1, CONV_W - 1, D_MODEL), lambda bi, ti: (bi, 0, 0)),
        ],
        out_shape=[jax.ShapeDtypeStruct((b, t, D_MODEL), F32),
                   jax.ShapeDtypeStruct((b, CONV_W - 1, D_MODEL), F32)],
        scratch_shapes=[pltpu.VMEM((CARRY + tm, D_MODEL), F32)],
        compiler_params=_params("parallel", "arbitrary"),
        name="conv_prompt",
    )(x, g.reshape(1, D_MODEL), w_in, w_conv, w_out)


def _conv_sample_kernel(x_ref, g_ref, win_ref, wc_ref, wout_ref, s1_ref, s2_ref, o_ref, u_ref, *, t_len):
    x = x_ref[...]
    h = _rms(x, g_ref[...]).astype(BF16)
    p = _dot(h, win_ref[...])
    bg = p[:, :D_MODEL]
    u = p[:, D_MODEL:2 * D_MODEL] * p[:, 2 * D_MODEL:]
    pos = lax.broadcasted_iota(jnp.int32, u.shape, 0) % t_len
    u1 = jnp.where(pos >= 1, pltpu.roll(u, 1, axis=0), s1_ref[...])
    u2 = jnp.where(pos >= 2, pltpu.roll(u, 2, axis=0), s2_ref[...])
    wc = wc_ref[...]
    conv = wc[0:1] * u2 + wc[1:2] * u1 + wc[2:3] * u
    o_ref[...] = x + _dot((bg * conv).astype(BF16), wout_ref[...])
    u_ref[...] = u


def _conv_sample(x, state, g, w_in, w_conv, w_out):
    b, t, _ = x.shape
    n = b * t
    zeros = jnp.zeros((b, t, D_MODEL), F32)
    s1 = zeros.at[:, 0].set(state[:, 1]).reshape(n, D_MODEL)
    s2 = zeros.at[:, 0].set(state[:, 0]).at[:, 1].set(state[:, 1]).reshape(n, D_MODEL)
    y, u = pl.pallas_call(
        functools.partial(_conv_sample_kernel, t_len=t),
        out_shape=[jax.ShapeDtypeStruct((n, D_MODEL), F32), jax.ShapeDtypeStruct((n, D_MODEL), F32)],
        compiler_params=pltpu.CompilerParams(vmem_limit_bytes=VMEM_LIMIT),
        name="conv_sample",
    )(x.reshape(n, D_MODEL), g.reshape(1, D_MODEL), w_in, w_conv, w_out, s1, s2)
    return y.reshape(b, t, D_MODEL), u.reshape(b, t, D_MODEL)[:, t - (CONV_W - 1):]


def _proj_kernel(x_ref, g_ref, wq_ref, wkv_ref, wdup_ref, wgate_ref,
                 q_ref, kvc_ref, kvs_ref, kvw_ref, ksd_ref, vsd_ref, kwd_ref, vwd_ref, gate_ref):
    h = _rms(x_ref[...], g_ref[...]).astype(BF16)
    q_ref[...] = (_dot(h, wq_ref[...]) * (HEAD_DIM ** -0.5)).astype(BF16)
    kv = _dot(h, wkv_ref[...])
    kvc_ref[...] = kv[:, :2 * KV_W]
    kvs_ref[...] = kv[:, 2 * KV_W:4 * KV_W]
    kvw_ref[...] = kv[:, 4 * KV_W:]
    dup = _dot(h, wdup_ref[...]).astype(BF16)
    ksd_ref[...] = dup[:, :DUP_W]
    vsd_ref[...] = dup[:, DUP_W:2 * DUP_W]
    kwd_ref[...] = dup[:, 2 * DUP_W:3 * DUP_W]
    vwd_ref[...] = dup[:, 3 * DUP_W:]
    gate_ref[...] = jax.nn.sigmoid(_dot(h, wgate_ref[...]))


def _dup_heads(w):
    w = w.reshape(w.shape[0], N_KV_HEADS, HEAD_DIM)
    return jnp.concatenate([w, w], axis=-1).reshape(w.shape[0], DUP_W)


def _nsa_weights(w_in):
    n_q = N_HEADS * HEAD_DIM
    wq = w_in[:, :n_q].astype(BF16)
    wkv = w_in[:, n_q:n_q + 6 * KV_W].astype(BF16)
    wdup = jnp.concatenate(
        [_dup_heads(w_in[:, n_q + (2 + i) * KV_W:n_q + (3 + i) * KV_W]) for i in range(4)], axis=1).astype(BF16)
    wgate = jnp.pad(w_in[:, n_q + 6 * KV_W:], ((0, 0), (0, LANES - 3 * N_HEADS))).astype(BF16)
    return wq, wkv, wdup, wgate


def _nsa_project(x, g, weights, *, tm):
    n = x.shape[0]
    wq, wkv, wdup, wgate = weights
    row = lambda w: pl.BlockSpec((tm, w), lambda i: (i, 0))
    full = lambda a: pl.BlockSpec(a.shape, lambda i: (0, 0))
    return pl.pallas_call(
        _proj_kernel,
        grid=(n // tm,),
        in_specs=[row(D_MODEL), pl.BlockSpec((1, D_MODEL), lambda i: (0, 0)), full(wq), full(wkv), full(wdup), full(wgate)],
        out_specs=[row(N_HEADS * HEAD_DIM), row(2 * KV_W), row(2 * KV_W), row(2 * KV_W),
                   row(DUP_W), row(DUP_W), row(DUP_W), row(DUP_W), row(LANES)],
        out_shape=[jax.ShapeDtypeStruct((n, N_HEADS * HEAD_DIM), BF16)]
        + [jax.ShapeDtypeStruct((n, 2 * KV_W), F32)] * 3
        + [jax.ShapeDtypeStruct((n, DUP_W), BF16)] * 4
        + [jax.ShapeDtypeStruct((n, LANES), F32)],
        compiler_params=_params("parallel"),
        name="nsa_project",
    )(x, g.reshape(1, D_MODEL), wq, wkv, wdup, wgate)


def _compress_kernel(x_ref, pe_ref, w1k_ref, w1v_ref, w2k_ref, w2v_ref, kc_ref, vc_ref, acck_ref, accv_ref):
    l = pl.program_id(1)

    @pl.when(l == 0)
    def _():
        acck_ref[...] = jnp.zeros_like(acck_ref)
        accv_ref[...] = jnp.zeros_like(accv_ref)

    xb = x_ref[...] + pe_ref[...]
    acck_ref[...] += _dot(xb[:, :KV_W].astype(BF16), w1k_ref[...])
    accv_ref[...] += _dot(xb[:, KV_W:].astype(BF16), w1v_ref[...])

    @pl.when(l == L_CMP - 1)
    def _():
        kc_ref[...] = _dot(jax.nn.gelu(acck_ref[...]).astype(BF16), w2k_ref[...]).astype(BF16)
        vc_ref[...] = _dot(jax.nn.gelu(accv_ref[...]).astype(BF16), w2v_ref[...]).astype(BF16)


def _compress_weights(pe, w1k, w2k, w1v, w2v):
    eye = jnp.eye(N_KV_HEADS, dtype=F32)
    bd1 = lambda w1: jax.vmap(lambda w: jnp.kron(eye, w))(w1).astype(BF16)
    bd2 = lambda w2: jnp.kron(eye, jnp.concatenate([w2, w2], axis=1)).astype(BF16)
    pe_t = jnp.tile(pe, (1, 2 * N_KV_HEADS)).reshape(L_CMP, 1, 2 * KV_W)
    return pe_t, bd1(w1k), bd1(w1v), bd2(w2k), bd2(w2v)


def _compress(x, weights, *, tm):
    m = x.shape[0]
    tm = min(tm, m)
    pe_t, w1k, w1v, w2k, w2v = weights
    return pl.pallas_call(
        _compress_kernel,
        grid=(m // tm, L_CMP),
        in_specs=[
            pl.BlockSpec((tm, 2 * KV_W), lambda i, l: (i, l)),
            pl.BlockSpec((None, 1, 2 * KV_W), lambda i, l: (l, 0, 0)),
            pl.BlockSpec((None, KV_W, KV_W), lambda i, l: (l, 0, 0)),
            pl.BlockSpec((None, KV_W, KV_W), lambda i, l: (l, 0, 0)),
            pl.BlockSpec((KV_W, DUP_W), lambda i, l: (0, 0)),
            pl.BlockSpec((KV_W, DUP_W), lambda i, l: (0, 0)),
        ],
        out_specs=[pl.BlockSpec((tm, DUP_W), lambda i, l: (i, 0))] * 2,
        out_shape=[jax.ShapeDtypeStruct((m, DUP_W), BF16)] * 2,
        scratch_shapes=[pltpu.VMEM((tm, KV_W), F32)] * 2,
        compiler_params=_params("parallel", "arbitrary"),
        name="compress",
    )(x, pe_t, w1k, w1v, w2k, w2v)


TQ = 128
KCH = 512


def _softmax_cols(s, mask):
    s = jnp.where(mask, s, NEG)
    e = jnp.exp(s - jnp.max(s, axis=0, keepdims=True))
    return jnp.where(mask, e / jnp.sum(e, axis=0, keepdims=True), 0.0)


def _attn_prompt_kernel(x_ref, q_ref, gate_ref, kc_ref, vc_ref, ks_ref, vs_ref, kw_ref, vw_ref, wout_ref,
                        o_ref, mask_ref, m_ref, l_ref, acc_ref, *, t_len):
    i = pl.program_id(1)
    t0 = i * TQ
    n_sel = t_len // L_SEL
    n_cmp = t_len // L_CMP
    rows = GROUP * TQ
    lane = lax.broadcasted_iota(jnp.int32, (TQ, LANES), 1)
    lo_half = lane < HEAD_DIM
    t_row = t0 + lax.broadcasted_iota(jnp.int32, (rows, 1), 0) % TQ
    t_col = t0 + lax.broadcasted_iota(jnp.int32, (1, rows), 1) % TQ
    t_tok = t0 + lax.broadcasted_iota(jnp.int32, (1, TQ), 1)
    r_l = lax.broadcasted_iota(jnp.int32, (1, n_cmp), 1)
    end_l = (2 * (r_l % n_sel) + r_l // n_sel + 1) * L_CMP - 1
    r_s = lax.broadcasted_iota(jnp.int32, (n_cmp, 1), 0)
    end_s = (2 * (r_s % n_sel) + r_s // n_sel + 1) * L_CMP - 1
    blk = lax.broadcasted_iota(jnp.int32, (n_sel, TQ), 0)
    cur = t_tok // L_SEL
    forced = (blk == 0) | (blk == cur) | (blk == cur - 1)
    valid = blk <= cur
    expand = (lax.broadcasted_iota(jnp.int32, (n_sel, t_len), 1) // L_SEL
              == lax.broadcasted_iota(jnp.int32, (n_sel, t_len), 0)).astype(F32)
    n_chunks = t_len // KCH
    last_chunk = (t0 + TQ - 1) // KCH
    gates = gate_ref[0]
    zero = jnp.zeros((TQ, LANES), BF16)
    slabs = []
    for kv in range(N_KV_HEADS):
        qa = q_ref[0, :, kv * GROUP * HEAD_DIM:kv * GROUP * HEAD_DIM + LANES]
        qb = q_ref[0, :, kv * GROUP * HEAD_DIM + LANES:(kv + 1) * GROUP * HEAD_DIM]
        qh = jnp.concatenate([jnp.where(lo_half, qa, zero), jnp.where(lo_half, zero, qa),
                              jnp.where(lo_half, qb, zero), jnp.where(lo_half, zero, qb)], axis=0)
        sl = slice(kv * LANES, (kv + 1) * LANES)
        kc = kc_ref[0, :, sl]
        vc = vc_ref[0, :, sl]

        m_c = end_l <= t_row
        s_c = jnp.where(m_c, _dot_nt(qh, kc), NEG)
        e_c = jnp.exp(s_c - jnp.max(s_c, axis=1, keepdims=True))
        o_c = _dot(jnp.where(m_c, e_c, 0.0).astype(BF16), vc) / jnp.sum(e_c, axis=1, keepdims=True)

        p_t = _softmax_cols(_dot_nt(kc, qh), end_s <= t_col)
        p_g = p_t[:, 0:TQ]
        for g in range(1, GROUP):
            p_g = p_g + p_t[:, g * TQ:(g + 1) * TQ]
        imp = p_g[0:n_sel] + p_g[n_sel:n_cmp]
        score = jnp.where(forced, FORCE_BONUS, jnp.where(valid, imp, -FORCE_BONUS))
        rank = jnp.zeros((n_sel, TQ), jnp.int32)
        for j in range(n_sel):
            s_j = score[j:j + 1, :]
            ahead = (s_j > score) | ((s_j == score) & (blk > j))
            rank = rank + ahead.astype(jnp.int32)
        sel = ((rank < TOP_N) & valid).astype(F32)
        mask_all = _dot_tn(sel, expand)
        for c in range(n_chunks):
            mask_ref[c] = mask_all[:, c * KCH:(c + 1) * KCH]

        m_ref[...] = jnp.full(m_ref.shape, NEG, F32)
        l_ref[...] = jnp.zeros_like(l_ref)
        acc_ref[...] = jnp.zeros_like(acc_ref)
        for c in range(n_chunks):
            @pl.when(c <= last_chunk)
            def _():
                s = _dot_nt(qh, ks_ref[0, c * KCH:(c + 1) * KCH, sl])
                key = c * KCH + lax.broadcasted_iota(jnp.int32, (1, KCH), 1)
                keep = jnp.concatenate([mask_ref[c]] * GROUP, axis=0) > 0.5
                s = jnp.where(keep & (key <= t_row), s, NEG)
                m_old = m_ref[...]
                m_new = jnp.maximum(m_old, jnp.max(s, axis=1, keepdims=True))
                alpha = jnp.exp(m_old - m_new)
                e = jnp.exp(s - m_new)
                l_ref[...] = alpha * l_ref[...] + jnp.sum(e, axis=1, keepdims=True)
                acc_ref[...] = alpha * acc_ref[...] + _dot(e.astype(BF16), vs_ref[0, c * KCH:(c + 1) * KCH, sl])
                m_ref[...] = m_new
        o_s = acc_ref[...] / l_ref[...]

        w0 = pl.multiple_of(jnp.maximum(t0 - WINDOW, 0), TQ)
        s_w = _dot_nt(qh, kw_ref[0, pl.ds(w0, WINDOW + TQ), sl])
        dist = t_row - (w0 + lax.broadcasted_iota(jnp.int32, (1, WINDOW + TQ), 1))
        s_w = jnp.where((dist >= 0) & (dist < WINDOW), s_w, NEG)
        e_w = jnp.exp(s_w - jnp.max(s_w, axis=1, keepdims=True))
        o_w = _dot(e_w.astype(BF16), vw_ref[0, pl.ds(w0, WINDOW + TQ), sl]) / jnp.sum(e_w, axis=1, keepdims=True)

        heads = []
        for g in range(GROUP):
            r = slice(g * TQ, (g + 1) * TQ)
            col = kv * GROUP + g
            heads.append(gates[:, col:col + 1] * o_c[r]
                         + gates[:, N_HEADS + col:N_HEADS + col + 1] * o_s[r]
                         + gates[:, 2 * N_HEADS + col:2 * N_HEADS + col + 1] * o_w[r])
        slabs.append(jnp.where(lo_half, heads[0], heads[1]).astype(BF16))
        slabs.append(jnp.where(lo_half, heads[2], heads[3]).astype(BF16))
    o_ref[0] = x_ref[0] + _dot(jnp.concatenate(slabs, axis=1), wout_ref[...])


def _attn_prompt(x, q, gates, kc, vc, ksd, vsd, kwd, vwd, w_out):
    b, t, _ = x.shape
    n_cmp = t // L_CMP
    tile = lambda w: pl.BlockSpec((1, TQ, w), lambda bi, i: (bi, i, 0))
    seq = lambda n, w: pl.BlockSpec((1, n, w), lambda bi, i: (bi, 0, 0))
    rows = GROUP * TQ
    return pl.pallas_call(
        functools.partial(_attn_prompt_kernel, t_len=t),
        grid=(b, t // TQ),
        in_specs=[tile(D_MODEL), tile(N_HEADS * HEAD_DIM), tile(LANES), seq(n_cmp, DUP_W), seq(n_cmp, DUP_W),
                  seq(t, DUP_W), seq(t, DUP_W), seq(t, DUP_W), seq(t, DUP_W),
                  pl.BlockSpec((N_HEADS * HEAD_DIM, D_MODEL), lambda bi, i: (0, 0))],
        out_specs=tile(D_MODEL),
        out_shape=jax.ShapeDtypeStruct((b, t, D_MODEL), F32),
        scratch_shapes=[pltpu.VMEM((t // KCH, TQ, KCH), F32), pltpu.VMEM((rows, 1), F32),
                        pltpu.VMEM((rows, 1), F32), pltpu.VMEM((rows, LANES), F32)],
        compiler_params=_params("parallel", "arbitrary"),
        name="attn_prompt",
    )(x, q, gates, kc, vc, ksd, vsd, kwd, vwd, w_out)


def _attend_sample(q, t_pos, gates, k_cmp, v_cmp, gather_sel, kv_win, s_win):
    b, tq = q.shape[:2]
    nb_c = k_cmp.shape[1]
    blk_end = (jnp.arange(nb_c) + 1) * L_CMP - 1
    m_c = (blk_end[None, :] <= t_pos[:, None])[None, :, None, None, :]
    s_c = jnp.einsum('bqkgd,bnkd->bqkgn', q, k_cmp).astype(F32)
    p_c = jax.nn.softmax(jnp.where(m_c, s_c, NEG), axis=-1) * m_c
    o_c = jnp.einsum('bqkgn,bnkd->bqkgd', p_c, v_cmp)
    ratio = L_SEL // L_CMP
    nb_s = nb_c // ratio
    imp = p_c.sum(axis=3).reshape(b, tq, N_KV_HEADS, nb_s, ratio).sum(-1)
    blk = jnp.arange(nb_s)[None, :]
    cur = (t_pos // L_SEL)[:, None]
    forced = ((blk == 0) | (blk == cur) | (blk == cur - 1))[None, :, None, :]
    valid = (blk * L_SEL <= t_pos[:, None])[None, :, None, :]
    score = jnp.where(forced, FORCE_BONUS, jnp.where(valid, imp, -FORCE_BONUS))
    _, idx = lax.top_k(score, min(TOP_N, nb_s))
    k_sel, v_sel = gather_sel(idx)
    key_pos = idx[..., None] * L_SEL + jnp.arange(L_SEL)
    m_s = (key_pos <= t_pos[None, :, None, None, None])[:, :, :, None]
    s_s = jnp.einsum('bqkgd,bqkjld->bqkgjl', q, k_sel).astype(F32)
    s_s = jnp.where(m_s, s_s, NEG)
    n_sel = s_s.shape[4]
    p_s = jax.nn.softmax(s_s.reshape(b, tq, N_KV_HEADS, GROUP, n_sel * L_SEL), axis=-1)
    p_s = p_s.reshape(b, tq, N_KV_HEADS, GROUP, n_sel, L_SEL)
    o_s = jnp.einsum('bqkgjl,bqkjld->bqkgd', p_s, v_sel)
    d = t_pos[:, None] - s_win[None, :]
    m_w = ((d >= 0) & (d < WINDOW) & (s_win[None, :] >= 0))[None, :, None, None, :]
    s_w = jnp.einsum('bqkgd,bskd->bqkgs', q, kv_win[:, :, 0]).astype(F32)
    p_w = jax.nn.softmax(jnp.where(m_w, s_w, NEG), axis=-1)
    o_w = jnp.einsum('bqkgs,bskd->bqkgd', p_w, kv_win[:, :, 1])
    o = gates[:, :, 0, ..., None] * o_c + gates[:, :, 1, ..., None] * o_s + gates[:, :, 2, ..., None] * o_w
    return o.reshape(b, tq, N_HEADS * HEAD_DIM)


def _undup(a):
    return a.reshape(a.shape[:-1] + (N_KV_HEADS, 2 * HEAD_DIM))[..., :HEAD_DIM].astype(F32)


def kernel(x_prompt, x_sample, state_conv, cache_cmp_kv, cache_sel_kv, cache_win_kv, page_table, norm_ffa, w_ffa_gu, w_ffa_down, norm_mix, norm_ffb, w_ffb_gu, w_ffb_down, w_conv_in, w_conv, w_conv_out, w_nsa_in, pe_cmp, w_cmp_k1, w_cmp_k2, w_cmp_v1, w_cmp_v2, w_nsa_out, norm_final):
    bp, tp, _ = x_prompt.shape
    bs, ts, _ = x_sample.shape
    n_p, n_s = bp * tp, bs * ts
    xp = x_prompt.reshape(n_p, D_MODEL)
    xs = x_sample.reshape(n_s, D_MODEL)

    def ffn_pair(xp, xs, g, w_gu, w_down, final=False):
        w_gu, w_down = w_gu.astype(BF16), w_down.astype(BF16)
        return (_ffn(xp, g, w_gu, w_down, norm_final, tm=512, final=final),
                _ffn(xs, g, w_gu, w_down, norm_final, tm=n_s, final=final))

    xp, xs = ffn_pair(xp, xs, norm_ffa[0], w_ffa_gu[0], w_ffa_down[0])
    w_in, w_out = w_conv_in[0].astype(BF16), w_conv_out[0].astype(BF16)
    xp3, conv_p = _conv_prompt(xp.reshape(bp, tp, D_MODEL), norm_mix[0], w_in, w_conv[0], w_out)
    xs3, conv_s = _conv_sample(xs.reshape(bs, ts, D_MODEL), state_conv[0], norm_mix[0], w_in, w_conv[0], w_out)
    xp, xs = ffn_pair(xp3.reshape(n_p, D_MODEL), xs3.reshape(n_s, D_MODEL), norm_ffb[0], w_ffb_gu[0], w_ffb_down[0])

    xp, xs = ffn_pair(xp, xs, norm_ffa[1], w_ffa_gu[1], w_ffa_down[1])
    proj_w = _nsa_weights(w_nsa_in[0])
    cmp_w = _compress_weights(pe_cmp[0], w_cmp_k1[0], w_cmp_k2[0], w_cmp_v1[0], w_cmp_v2[0])
    w_out = w_nsa_out[0].astype(BF16)
    kv_shape = (2, N_KV_HEADS, HEAD_DIM)

    q, kvc, kvs, kvw, ksd, vsd, kwd, vwd, gates = _nsa_project(xp, norm_mix[1], proj_w, tm=512)
    n_cmp = tp // L_CMP
    kc, vc = _compress(kvc.reshape(bp * n_cmp, L_CMP * 2 * KV_W), cmp_w, tm=512)
    even_odd = lambda a: a.reshape(bp, n_cmp // 2, 2, DUP_W).transpose(0, 2, 1, 3).reshape(bp, n_cmp, DUP_W)
    seq = lambda a: a.reshape(bp, tp, a.shape[-1])
    xp3 = _attn_prompt(seq(xp), seq(q), seq(gates), even_odd(kc), even_odd(vc),
                       seq(ksd), seq(vsd), seq(kwd), seq(vwd), w_out)
    cmp_p = kvc.reshape((1, bp, tp) + kv_shape)
    sel_p = kvs.reshape((1, bp, tp) + kv_shape)
    win_p = kvw.reshape((bp, tp) + kv_shape)[None, :, tp - min(WINDOW, tp):]

    q, kvc, kvs, kvw, _, _, _, _, gates = _nsa_project(xs, norm_mix[1], proj_w, tm=n_s)
    n_pages = page_table.shape[1]
    past = n_pages * PAGE_SIZE
    n_new = -(-ts // L_SEL) * L_SEL
    row_w = 2 * KV_W
    past_c = cache_cmp_kv[0].reshape(-1, PAGE_SIZE * row_w)[page_table]
    kc_past, vc_past = _compress(past_c.reshape(bs * past // L_CMP, L_CMP * row_w), cmp_w, tm=1024)
    new_c = jnp.pad(kvc.reshape(bs, ts, row_w), ((0, 0), (0, n_new - ts), (0, 0)))
    kc_new, vc_new = _compress(new_c.reshape(bs * n_new // L_CMP, L_CMP * row_w), cmp_w, tm=bs * n_new // L_CMP)
    join = lambda a, c: jnp.concatenate([_undup(a).reshape(bs, past // L_CMP, N_KV_HEADS, HEAD_DIM),
                                         _undup(c).reshape(bs, n_new // L_CMP, N_KV_HEADS, HEAD_DIM)], axis=1)
    k_c, v_c = join(kc_past, kc_new), join(vc_past, vc_new)
    kv_s = kvs.reshape((bs, ts) + kv_shape)
    kv_w = kvw.reshape((bs, ts) + kv_shape)
    spp = PAGE_SIZE // L_SEL
    pool_blocks = cache_sel_kv[0].reshape((-1, L_SEL) + kv_shape)
    new_blocks = jnp.pad(kv_s, ((0, 0), (0, n_new - ts), (0, 0), (0, 0), (0, 0))).reshape((bs, n_new // L_SEL, L_SEL) + kv_shape)
    n_past_blk = past // L_SEL
    b_idx = jnp.arange(bs)[:, None, None, None]
    kv_idx = jnp.arange(N_KV_HEADS)[None, None, :, None]

    def gather_sel(idx):
        from_past = (idx < n_past_blk)[..., None, None, None]
        ip = jnp.minimum(idx, n_past_blk - 1)
        phys = page_table[b_idx, ip // spp] * spp + ip % spp
        blk_p = pool_blocks[phys, :, :, kv_idx, :]
        jn = jnp.clip(idx - n_past_blk, 0, n_new // L_SEL - 1)
        blk_n = new_blocks[b_idx, jn, :, :, kv_idx, :]
        blk = jnp.where(from_past, blk_p, blk_n)
        return blk[..., 0, :], blk[..., 1, :]

    wb = cache_win_kv.shape[2]
    kvw_all = jnp.concatenate([cache_win_kv[0], kv_w], axis=1)
    qs = q.astype(F32).reshape(bs, ts, N_KV_HEADS, GROUP, HEAD_DIM)
    gs = gates[:, :3 * N_HEADS].reshape(bs, ts, 3, N_KV_HEADS, GROUP)
    o = _attend_sample(qs, past + jnp.arange(ts), gs, k_c, v_c, gather_sel, kvw_all, past - wb + jnp.arange(wb + ts))
    xs = xs + (o.reshape(n_s, N_HEADS * HEAD_DIM) @ w_nsa_out[0])
    cmp_s = kvc.reshape((1, bs, ts) + kv_shape)
    sel_s = kv_s[None]
    win_s = kvw_all[None, :, ts:]

    xp, xs = ffn_pair(xp3.reshape(n_p, D_MODEL), xs, norm_ffb[1], w_ffb_gu[1], w_ffb_down[1], final=True)
    return (xp.reshape(bp, tp, D_MODEL), xs.reshape(bs, ts, D_MODEL),
            conv_p[None], conv_s[None], cmp_p, cmp_s, sel_p, sel_s, win_p, win_s)
```

```python
import functools

import jax
import jax.numpy as jnp
from jax import lax
from jax.experimental import pallas as pl
from jax.experimental.pallas import tpu as pltpu

D_MODEL = 1024
D_FF = 2816
CONV_W = 3
N_HEADS = 16
HEAD_DIM = 64
N_KV_HEADS = 4
GROUP = N_HEADS // N_KV_HEADS
KV_W = N_KV_HEADS * HEAD_DIM
L_CMP = 32
L_SEL = 64
TOP_N = 16
WINDOW = 512
PAGE_SIZE = 128
NORM_EPS = 1e-6
FORCE_BONUS = 1e4
NEG = -1e30

LANES = 128
VMEM_LIMIT = 56 * 1024 * 1024

F32 = jnp.float32
BF16 = jnp.bfloat16


def _params(*sem):
    return pltpu.CompilerParams(dimension_semantics=sem, vmem_limit_bytes=VMEM_LIMIT)


def _rms(x, g):
    return x * lax.rsqrt(jnp.mean(x * x, axis=-1, keepdims=True) + NORM_EPS) * g


def _dot(a, b):
    return jnp.dot(a, b, preferred_element_type=F32)


def _dot_nt(a, b):
    return lax.dot_general(a, b, (((1,), (1,)), ((), ())), preferred_element_type=F32)


def _dot_tn(a, b):
    return lax.dot_general(a, b, (((0,), (0,)), ((), ())), preferred_element_type=F32)


def _ffn_kernel(x_ref, g_ref, wg_ref, wu_ref, wd_ref, gf_ref, o_ref, xn_ref, acc_ref, *, n_ff, final):
    j = pl.program_id(1)

    @pl.when(j == 0)
    def _():
        xn_ref[...] = _rms(x_ref[...], g_ref[...]).astype(BF16)
        acc_ref[...] = jnp.zeros_like(acc_ref)

    xn = xn_ref[...]
    gate = _dot(xn, wg_ref[...])
    up = _dot(xn, wu_ref[...])
    act = (gate * jax.nn.sigmoid(gate) * up).astype(BF16)
    acc_ref[...] += _dot(act, wd_ref[...])

    @pl.when(j == n_ff - 1)
    def _():
        y = x_ref[...] + 0.5 * acc_ref[...]
        o_ref[...] = _rms(y, gf_ref[...]) if final else y


def _ffn(x, g, w_gu, w_down, g_final, *, tm, tf=1408, final=False):
    n = x.shape[0]
    n_ff = D_FF // tf
    return pl.pallas_call(
        functools.partial(_ffn_kernel, n_ff=n_ff, final=final),
        grid=(n // tm, n_ff),
        in_specs=[
            pl.BlockSpec((tm, D_MODEL), lambda i, j: (i, 0)),
            pl.BlockSpec((1, D_MODEL), lambda i, j: (0, 0)),
            pl.BlockSpec((D_MODEL, tf), lambda i, j: (0, j)),
            pl.BlockSpec((D_MODEL, tf), lambda i, j: (0, j + n_ff)),
            pl.BlockSpec((tf, D_MODEL), lambda i, j: (j, 0)),
            pl.BlockSpec((1, D_MODEL), lambda i, j: (0, 0)),
        ],
        out_specs=pl.BlockSpec((tm, D_MODEL), lambda i, j: (i, 0)),
        out_shape=jax.ShapeDtypeStruct((n, D_MODEL), F32),
        scratch_shapes=[pltpu.VMEM((tm, D_MODEL), BF16), pltpu.VMEM((tm, D_MODEL), F32)],
        compiler_params=_params("parallel", "arbitrary"),
        name="ffn_final" if final else "ffn",
    )(x, g.reshape(1, D_MODEL), w_gu, w_gu, w_down, g_final.reshape(1, D_MODEL))


CARRY = 8


def _conv_prompt_kernel(x_ref, g_ref, win_ref, wc_ref, wout_ref, o_ref, st_ref, uext_ref, *, tm):
    @pl.when(pl.program_id(1) == 0)
    def _():
        uext_ref[0:CARRY, :] = jnp.zeros((CARRY, D_MODEL), F32)

    x = x_ref[0]
    h = _rms(x, g_ref[...]).astype(BF16)
    p = _dot(h, win_ref[...])
    bg = p[:, :D_MODEL]
    u = p[:, D_MODEL:2 * D_MODEL] * p[:, 2 * D_MODEL:]
    uext_ref[CARRY:CARRY + tm, :] = u
    wc = wc_ref[...]
    conv = (wc[0:1] * uext_ref[CARRY - 2:CARRY - 2 + tm, :]
            + wc[1:2] * uext_ref[CARRY - 1:CARRY - 1 + tm, :]
            + wc[2:3] * u)
    y = _dot((bg * conv).astype(BF16), wout_ref[...])
    o_ref[0] = x + y
    st_ref[0] = uext_ref[CARRY + tm - 2:CARRY + tm, :]
    uext_ref[0:CARRY, :] = uext_ref[tm:tm + CARRY, :]


def _conv_prompt(x, g, w_in, w_conv, w_out, *, tm=512):
    b, t, _ = x.shape
    return pl.pallas_call(
        functools.partial(_conv_prompt_kernel, tm=tm),
        grid=(b, t // tm),
        in_specs=[
            pl.BlockSpec((1, tm, D_MODEL), lambda bi, ti: (bi, ti, 0)),
            pl.BlockSpec((1, D_MODEL), lambda bi, ti: (0, 0)),
            pl.BlockSpec((D_MODEL, 3 * D_MODEL), lambda bi, ti: (0, 0)),
            pl.BlockSpec((CONV_W, D_MODEL), lambda bi, ti: (0, 0)),
            pl.BlockSpec((D_MODEL, D_MODEL), lambda bi, ti: (0, 0)),
        ],
        out_specs=[
            pl.BlockSpec((1, tm, D_MODEL), lambda bi, ti: (bi, ti, 0)),
            pl.BlockSpec((1, CONV_W - 1, D_MODEL), lambda bi, ti: (bi, 0, 0)),
        ],
        out_shape=[jax.ShapeDtypeStruct((b, t, D_MODEL), F32),
                   jax.ShapeDtypeStruct((b, CONV_W - 1, D_MODEL), F32)],
        scratch_shapes=[pltpu.VMEM((CARRY + tm, D_MODEL), F32)],
        compiler_params=_params("parallel", "arbitrary"),
        name="conv_prompt",
    )(x, g.reshape(1, D_MODEL), w_in, w_conv, w_out)


def _conv_sample_kernel(x_ref, g_ref, win_ref, wc_ref, wout_ref, s1_ref, s2_ref, o_ref, u_ref, *, t_len):
    x = x_ref[...]
    h = _rms(x, g_ref[...]).astype(BF16)
    p = _dot(h, win_ref[...])
    bg = p[:, :D_MODEL]
    u = p[:, D_MODEL:2 * D_MODEL] * p[:, 2 * D_MODEL:]
    pos = lax.broadcasted_iota(jnp.int32, u.shape, 0) % t_len
    u1 = jnp.where(pos >= 1, pltpu.roll(u, 1, axis=0), s1_ref[...])
    u2 = jnp.where(pos >= 2, pltpu.roll(u, 2, axis=0), s2_ref[...])
    wc = wc_ref[...]
    conv = wc[0:1] * u2 + wc[1:2] * u1 + wc[2:3] * u
    o_ref[...] = x + _dot((bg * conv).astype(BF16), wout_ref[...])
    u_ref[...] = u


def _conv_sample(x, state, g, w_in, w_conv, w_out):
    b, t, _ = x.shape
    n = b * t
    zeros = jnp.zeros((b, t, D_MODEL), F32)
    s1 = zeros.at[:, 0].set(state[:, 1]).reshape(n, D_MODEL)
    s2 = zeros.at[:, 0].set(state[:, 0]).at[:, 1].set(state[:, 1]).reshape(n, D_MODEL)
    y, u = pl.pallas_call(
        functools.partial(_conv_sample_kernel, t_len=t),
        out_shape=[jax.ShapeDtypeStruct((n, D_MODEL), F32), jax.ShapeDtypeStruct((n, D_MODEL), F32)],
        compiler_params=pltpu.CompilerParams(vmem_limit_bytes=VMEM_LIMIT),
        name="conv_sample",
    )(x.reshape(n, D_MODEL), g.reshape(1, D_MODEL), w_in, w_conv, w_out, s1, s2)
    return y.reshape(b, t, D_MODEL), u.reshape(b, t, D_MODEL)[:, t - (CONV_W - 1):]


N_Q = N_HEADS * HEAD_DIM


def _proj_kernel(x_ref, g_ref, wqt_ref, wkv_ref, wvt_ref, wgt_ref,
                 qt_ref, kvc_ref, kvs_ref, kvw_ref, ks_ref, kw_ref, vst_ref, vwt_ref, gt_ref):
    h = _rms(x_ref[...], g_ref[...]).astype(BF16)
    qt_ref[...] = (_dot_nt(wqt_ref[...], h) * (HEAD_DIM ** -0.5)).astype(BF16)
    kv = _dot(h, wkv_ref[...])
    kvc_ref[...] = kv[:, :2 * KV_W]
    kvs_ref[...] = kv[:, 2 * KV_W:4 * KV_W]
    kvw_ref[...] = kv[:, 4 * KV_W:]
    ks_ref[...] = kv[:, 2 * KV_W:3 * KV_W].astype(BF16)
    kw_ref[...] = kv[:, 4 * KV_W:5 * KV_W].astype(BF16)
    vt = _dot_nt(wvt_ref[...], h).astype(BF16)
    vst_ref[...] = vt[:KV_W]
    vwt_ref[...] = vt[KV_W:]
    gt_ref[...] = jax.nn.sigmoid(_dot_nt(wgt_ref[...], h))


def _nsa_weights(w_in):
    wqt = w_in[:, :N_Q].T.astype(BF16)
    wkv = w_in[:, N_Q:N_Q + 6 * KV_W].astype(BF16)
    wvt = jnp.concatenate([w_in[:, N_Q + 3 * KV_W:N_Q + 4 * KV_W], w_in[:, N_Q + 5 * KV_W:N_Q + 6 * KV_W]], axis=1).T.astype(BF16)
    wgt = jnp.pad(w_in[:, N_Q + 6 * KV_W:], ((0, 0), (0, LANES - 3 * N_HEADS))).T.astype(BF16)
    return wqt, wkv, wvt, wgt


def _nsa_project(x, g, weights, *, tm):
    n = x.shape[0]
    row = lambda w: pl.BlockSpec((tm, w), lambda i: (i, 0))
    col = lambda h: pl.BlockSpec((h, tm), lambda i: (0, i))
    full = lambda a: pl.BlockSpec(a.shape, lambda i: (0, 0))
    return pl.pallas_call(
        _proj_kernel,
        grid=(n // tm,),
        in_specs=[row(D_MODEL), pl.BlockSpec((1, D_MODEL), lambda i: (0, 0))] + [full(w) for w in weights],
        out_specs=[col(N_Q), row(2 * KV_W), row(2 * KV_W), row(2 * KV_W), row(KV_W), row(KV_W),
                   col(KV_W), col(KV_W), col(LANES)],
        out_shape=[jax.ShapeDtypeStruct((N_Q, n), BF16)]
        + [jax.ShapeDtypeStruct((n, 2 * KV_W), F32)] * 3
        + [jax.ShapeDtypeStruct((n, KV_W), BF16)] * 2
        + [jax.ShapeDtypeStruct((KV_W, n), BF16)] * 2
        + [jax.ShapeDtypeStruct((LANES, n), F32)],
        compiler_params=_params("parallel"),
        name="nsa_project",
    )(x, g.reshape(1, D_MODEL), *weights)


def _compress_out(acck, accv, w2k_ref, w2vt_ref, kc_ref, vct_ref):
    kc_ref[...] = _dot(jax.nn.gelu(acck).astype(BF16), w2k_ref[...]).astype(BF16)
    vct_ref[...] = _dot_nt(w2vt_ref[...], jax.nn.gelu(accv).astype(BF16)).astype(BF16)


def _compress_kernel(x_ref, pe_ref, w1k_ref, w1v_ref, w2k_ref, w2vt_ref, kc_ref, vct_ref, acck_ref, accv_ref):
    l = pl.program_id(1)

    @pl.when(l == 0)
    def _():
        acck_ref[...] = jnp.zeros_like(acck_ref)
        accv_ref[...] = jnp.zeros_like(accv_ref)

    xb = x_ref[...] + pe_ref[...]
    acck_ref[...] += _dot(xb[:, :KV_W].astype(BF16), w1k_ref[...])
    accv_ref[...] += _dot(xb[:, KV_W:].astype(BF16), w1v_ref[...])

    @pl.when(l == L_CMP - 1)
    def _():
        _compress_out(acck_ref[...], accv_ref[...], w2k_ref, w2vt_ref, kc_ref, vct_ref)


PAGES_PER_STEP = 32
BLOCKS_PER_PAGE = PAGE_SIZE // L_CMP


def _compress_weights(pe, w1k, w2k, w1v, w2v):
    eye = jnp.eye(N_KV_HEADS, dtype=F32)
    bd1 = lambda w1: jax.vmap(lambda w: jnp.kron(eye, w))(w1).astype(BF16)
    pe_row = jnp.tile(pe, (1, 2 * N_KV_HEADS))
    r = jnp.arange(2 * PAGE_SIZE)
    src = (r % 8 // BLOCKS_PER_PAGE) * PAGE_SIZE + (r % BLOCKS_PER_PAGE) * L_CMP + r // 8
    perm = (src[:, None] == r[None, :]).astype(BF16)
    return dict(pe_row=pe_row.reshape(L_CMP, 1, 2 * KV_W), pe_page=jnp.tile(pe_row, (BLOCKS_PER_PAGE, 1)), perm=perm,
                w1k=bd1(w1k), w1v=bd1(w1v), w2k=jnp.kron(eye, w2k).astype(BF16), w2vt=jnp.kron(eye, w2v).T.astype(BF16))


def _compress(x, w, *, tm):
    m = x.shape[0]
    tm = min(tm, m)
    return pl.pallas_call(
        _compress_kernel,
        grid=(m // tm, L_CMP),
        in_specs=[
            pl.BlockSpec((tm, 2 * KV_W), lambda i, l: (i, l)),
            pl.BlockSpec((None, 1, 2 * KV_W), lambda i, l: (l, 0, 0)),
            pl.BlockSpec((None, KV_W, KV_W), lambda i, l: (l, 0, 0)),
            pl.BlockSpec((None, KV_W, KV_W), lambda i, l: (l, 0, 0)),
            pl.BlockSpec((KV_W, KV_W), lambda i, l: (0, 0)),
            pl.BlockSpec((KV_W, KV_W), lambda i, l: (0, 0)),
        ],
        out_specs=[pl.BlockSpec((tm, KV_W), lambda i, l: (i, 0)), pl.BlockSpec((KV_W, tm), lambda i, l: (0, i))],
        out_shape=[jax.ShapeDtypeStruct((m, KV_W), BF16), jax.ShapeDtypeStruct((KV_W, m), BF16)],
        scratch_shapes=[pltpu.VMEM((tm, KV_W), F32)] * 2,
        compiler_params=_params("parallel", "arbitrary"),
        name="compress",
    )(x, w["pe_row"], w["w1k"], w["w1v"], w["w2k"], w["w2vt"])


def _compress_paged_kernel(pt_ref, *refs):
    del pt_ref
    pages = refs[:PAGES_PER_STEP]
    pe_ref, perm_ref, w1k_ref, w1v_ref, w2k_ref, w2vt_ref, kc_ref, vct_ref = refs[PAGES_PER_STEP:]
    pe = pe_ref[...]
    perm = perm_ref[...]
    by_l = []
    for p in range(0, PAGES_PER_STEP, 2):
        pair = jnp.concatenate([(pages[p][...] + pe).astype(BF16), (pages[p + 1][...] + pe).astype(BF16)], axis=0)
        by_l.append(_dot(perm, pair))
    acck = jnp.zeros((PAGES_PER_STEP * BLOCKS_PER_PAGE, KV_W), F32)
    accv = jnp.zeros((PAGES_PER_STEP * BLOCKS_PER_PAGE, KV_W), F32)
    for l in range(L_CMP):
        xl = jnp.concatenate([y[l * 8:(l + 1) * 8] for y in by_l], axis=0).astype(BF16)
        acck = acck + _dot(xl[:, :KV_W], w1k_ref[l])
        accv = accv + _dot(xl[:, KV_W:], w1v_ref[l])
    _compress_out(acck, accv, w2k_ref, w2vt_ref, kc_ref, vct_ref)


def _compress_paged(cache, page_table, w):
    b, n_pages = page_table.shape
    steps = n_pages // PAGES_PER_STEP
    m_step = PAGES_PER_STEP * BLOCKS_PER_PAGE
    m = b * n_pages * BLOCKS_PER_PAGE
    page = lambda j: pl.BlockSpec((None, PAGE_SIZE, 2 * KV_W), lambda bi, s, pt: (pt[bi, s * PAGES_PER_STEP + j], 0, 0))
    full = lambda a: pl.BlockSpec(a.shape, lambda bi, s, pt: (0,) * a.ndim)
    consts = [w["pe_page"], w["perm"], w["w1k"], w["w1v"], w["w2k"], w["w2vt"]]
    return pl.pallas_call(
        _compress_paged_kernel,
        grid_spec=pltpu.PrefetchScalarGridSpec(
            num_scalar_prefetch=1,
            grid=(b, steps),
            in_specs=[page(j) for j in range(PAGES_PER_STEP)] + [full(a) for a in consts],
            out_specs=[pl.BlockSpec((m_step, KV_W), lambda bi, s, pt: (bi * steps + s, 0)),
                       pl.BlockSpec((KV_W, m_step), lambda bi, s, pt: (0, bi * steps + s))],
        ),
        out_shape=[jax.ShapeDtypeStruct((m, KV_W), BF16), jax.ShapeDtypeStruct((KV_W, m), BF16)],
        compiler_params=_params("parallel", "arbitrary"),
        name="compress_paged",
    )(page_table, *([cache] * PAGES_PER_STEP), *consts)


TQ = 128
COLS = GROUP * TQ
KCH = 512
WIN_BLOCKS = WINDOW // TQ + 1


def _head_queries(q_t, kv):
    zero = jnp.zeros((HEAD_DIM, TQ), BF16)
    cols = []
    for g in range(GROUP):
        h = kv * GROUP + g
        qg = q_t[h * HEAD_DIM:(h + 1) * HEAD_DIM, :]
        cols.append(jnp.concatenate([qg, zero] if kv % 2 == 0 else [zero, qg], axis=0))
    return jnp.concatenate(cols, axis=1)


def _slab(kv):
    return slice((kv // 2) * LANES, (kv // 2 + 1) * LANES)


def _head(kv):
    return slice(kv * HEAD_DIM, (kv + 1) * HEAD_DIM)


def _cmp_branch(kc, vc_t, x, keep):
    s = jnp.where(keep, _dot(kc, x), NEG)
    e = jnp.exp(s - jnp.max(s, axis=0, keepdims=True))
    l = jnp.sum(e, axis=0, keepdims=True)
    e = jnp.where(keep, e, 0.0)
    return _dot(vc_t, e.astype(BF16)) / l, e / l


def _group_sum(p):
    out = p[:, 0:TQ]
    for g in range(1, GROUP):
        out = out + p[:, g * TQ:(g + 1) * TQ]
    return out


def _topn_rows(score, n):
    nb = score.shape[0]
    row = lax.broadcasted_iota(jnp.int32, score.shape, 0).astype(F32)
    picked = jnp.zeros(score.shape, F32)
    ids = []
    for _ in range(n):
        top = jnp.max(score, axis=0, keepdims=True)
        first = jnp.min(jnp.where(score == top, row, float(nb)), axis=0, keepdims=True)
        hit = row == first
        picked = jnp.where(hit, 1.0, picked)
        score = jnp.where(hit, -jnp.inf, score)
        ids.append(first)
    return picked, ids


def _masked_softmax_pv(s, keep, v_t):
    s = jnp.concatenate([jnp.where(keep, s[:, g * TQ:(g + 1) * TQ], NEG) for g in range(GROUP)], axis=1)
    e = jnp.exp(s - jnp.max(s, axis=0, keepdims=True))
    return _dot(v_t, e.astype(BF16)) / jnp.sum(e, axis=0, keepdims=True)


def _gate_mix(g_t, kv, branches):
    out = []
    for g in range(GROUP):
        h = kv * GROUP + g
        c = slice(g * TQ, (g + 1) * TQ)
        acc = None
        for br, o in branches:
            term = g_t[br * N_HEADS + h:br * N_HEADS + h + 1, :] * o[:, c]
            acc = term if acc is None else acc + term
        out.append(acc)
    return out


def _attn_prompt_kernel(x_ref, qt_ref, gt_ref, kc_ref, vct_ref, ks_ref, vst_ref, kw_ref, *rest, t_len):
    vwt_refs = rest[:WIN_BLOCKS]
    wout_ref, o_ref, mask_ref, m_ref, l_ref, acc_ref = rest[WIN_BLOCKS:]
    t0 = pl.program_id(1) * TQ
    n_sel = t_len // L_SEL
    n_cmp = t_len // L_CMP
    t_tok = t0 + lax.broadcasted_iota(jnp.int32, (1, TQ), 1)
    t_col = t0 + lax.broadcasted_iota(jnp.int32, (1, COLS), 1) % TQ
    r_c = lax.broadcasted_iota(jnp.int32, (n_cmp, 1), 0)
    keep_c = (2 * (r_c % n_sel) + r_c // n_sel + 1) * L_CMP - 1 <= t_col
    blk = lax.broadcasted_iota(jnp.int32, (n_sel, TQ), 0)
    cur = t_tok // L_SEL
    forced = (blk == 0) | (blk == cur) | (blk == cur - 1)
    valid = blk <= cur
    key_in_blk = lax.broadcasted_iota(jnp.int32, (L_SEL, 1), 0)
    win_pos = t0 - WINDOW + lax.broadcasted_iota(jnp.int32, (WIN_BLOCKS * TQ, 1), 0)
    dist = t_tok - win_pos
    keep_w = (dist >= 0) & (dist < WINDOW) & (win_pos >= 0)
    n_chunks = t_len // KCH
    last_chunk = (t0 + TQ - 1) // KCH
    q_t = qt_ref[...]
    g_t = gt_ref[...]

    o_cmp = []
    for kv in range(N_KV_HEADS):
        o_c, p = _cmp_branch(kc_ref[0, :, _slab(kv)], vct_ref[0, _head(kv), :], _head_queries(q_t, kv), keep_c)
        o_cmp.append(o_c)
        p_g = _group_sum(p)
        imp = p_g[0:n_sel] + p_g[n_sel:n_cmp]
        score = jnp.where(forced, FORCE_BONUS, jnp.where(valid, imp, -FORCE_BONUS))
        picked, _ = _topn_rows(score, min(TOP_N, n_sel))
        sel = jnp.where(valid, picked, 0.0)
        for j in range(n_sel):
            causal = j * L_SEL + key_in_blk <= t_tok
            mask_ref[kv, j * L_SEL:(j + 1) * L_SEL, :] = jnp.where(
                causal, jnp.broadcast_to(sel[j:j + 1, :], (L_SEL, TQ)), 0.0)

    m_ref[...] = jnp.full(m_ref.shape, NEG, F32)
    l_ref[...] = jnp.zeros_like(l_ref)
    acc_ref[...] = jnp.zeros_like(acc_ref)
    for c in range(n_chunks):
        @pl.when(c <= last_chunk)
        def _():
            keys = slice(c * KCH, (c + 1) * KCH)
            for kv in range(N_KV_HEADS):
                s = _dot(ks_ref[0, keys, _slab(kv)], _head_queries(q_t, kv))
                keep = mask_ref[kv, keys, :] > 0.5
                s = jnp.concatenate([jnp.where(keep, s[:, g * TQ:(g + 1) * TQ], NEG) for g in range(GROUP)], axis=1)
                m_old = m_ref[kv]
                m_new = jnp.maximum(m_old, jnp.max(s, axis=0, keepdims=True))
                alpha = jnp.exp(m_old - m_new)
                e = jnp.exp(s - m_new)
                l_ref[kv] = alpha * l_ref[kv] + jnp.sum(e, axis=0, keepdims=True)
                acc_ref[kv] = alpha * acc_ref[kv] + _dot(vst_ref[_head(kv), keys], e.astype(BF16))
                m_ref[kv] = m_new

    heads = []
    for kv in range(N_KV_HEADS):
        x = _head_queries(q_t, kv)
        s_w = []
        for j in range(WIN_BLOCKS):
            start = pl.multiple_of(jnp.maximum(t0 - WINDOW + j * TQ, 0), TQ)
            s_w.append(_dot(kw_ref[0, pl.ds(start, TQ), _slab(kv)], x))
        v_w = jnp.concatenate([r[_head(kv), :] for r in vwt_refs], axis=1)
        o_w = _masked_softmax_pv(jnp.concatenate(s_w, axis=0), keep_w, v_w)
        o_s = acc_ref[kv] / l_ref[kv]
        heads += _gate_mix(g_t, kv, ((0, o_cmp[kv]), (1, o_s), (2, o_w)))
    o = jnp.concatenate(heads, axis=0).T.astype(BF16)
    o_ref[0] = x_ref[0] + _dot(o, wout_ref[...])


def _attn_prompt(x, q_t, g_t, kc, vc_t, ks, vs_t, kw, vw_t, w_out):
    b, t, _ = x.shape
    nt = t // TQ
    n_cmp = t // L_CMP
    tile = lambda w: pl.BlockSpec((1, TQ, w), lambda bi, i: (bi, i, 0))
    tile_t = lambda h: pl.BlockSpec((h, TQ), lambda bi, i: (0, bi * nt + i))
    win_t = lambda j: pl.BlockSpec((KV_W, TQ), lambda bi, i: (0, bi * nt + jnp.maximum(i - (WIN_BLOCKS - 1) + j, 0)))
    return pl.pallas_call(
        functools.partial(_attn_prompt_kernel, t_len=t),
        grid=(b, nt),
        in_specs=[tile(D_MODEL), tile_t(N_Q), tile_t(LANES),
                  pl.BlockSpec((1, n_cmp, KV_W), lambda bi, i: (bi, 0, 0)),
                  pl.BlockSpec((1, KV_W, n_cmp), lambda bi, i: (bi, 0, 0)),
                  pl.BlockSpec((1, t, KV_W), lambda bi, i: (bi, 0, 0)),
                  pl.BlockSpec((KV_W, t), lambda bi, i: (0, bi)),
                  pl.BlockSpec((1, t, KV_W), lambda bi, i: (bi, 0, 0))]
        + [win_t(j) for j in range(WIN_BLOCKS)]
        + [pl.BlockSpec((N_Q, D_MODEL), lambda bi, i: (0, 0))],
        out_specs=tile(D_MODEL),
        out_shape=jax.ShapeDtypeStruct((b, t, D_MODEL), F32),
        scratch_shapes=[pltpu.VMEM((N_KV_HEADS, t, TQ), F32), pltpu.VMEM((N_KV_HEADS, 1, COLS), F32),
                        pltpu.VMEM((N_KV_HEADS, 1, COLS), F32), pltpu.VMEM((N_KV_HEADS, HEAD_DIM, COLS), F32)],
        compiler_params=_params("parallel", "arbitrary"),
        name="attn_prompt",
    )(x, q_t, g_t, kc, vc_t, ks, vs_t, kw, *([vw_t] * WIN_BLOCKS), w_out)


NEW_ROWS = 16


def _attn_sample_kernel(qt_ref, gt_ref, kc_ref, vct_ref, win_ref, new_ref, part_ref, idx_ref, *, past, n_blk, half):
    t_tok = past + lax.broadcasted_iota(jnp.int32, (1, TQ), 1)
    t_col = past + lax.broadcasted_iota(jnp.int32, (1, COLS), 1) % TQ
    r_c = lax.broadcasted_iota(jnp.int32, (2 * half, 1), 0)
    keep_c = (r_c % half < n_blk) & ((2 * (r_c % half) + r_c // half + 1) * L_CMP - 1 <= t_col)
    blk = lax.broadcasted_iota(jnp.int32, (half, TQ), 0)
    cur = t_tok // L_SEL
    forced = (blk == 0) | (blk == cur) | (blk == cur - 1)
    valid = blk <= cur
    wb = win_ref.shape[1]
    win_pos = past - wb + lax.broadcasted_iota(jnp.int32, (wb + NEW_ROWS, 1), 0)
    dist = t_tok - win_pos
    keep_w = (dist >= 0) & (dist < WINDOW) & (win_pos >= 0)
    q_t = qt_ref[0]
    g_t = gt_ref[0]
    heads = []
    for kv in range(N_KV_HEADS):
        x = _head_queries(q_t, kv)
        o_c, p = _cmp_branch(kc_ref[0, :, _slab(kv)], vct_ref[0, _head(kv), :], x, keep_c)
        p_g = _group_sum(p)
        imp = p_g[0:half] + p_g[half:2 * half]
        score = jnp.where(forced, FORCE_BONUS, jnp.where(valid, imp, -FORCE_BONUS))
        score = jnp.where(blk < n_blk, score, -jnp.inf)
        _, ids = _topn_rows(score, min(TOP_N, n_blk))
        for j, first in enumerate(ids):
            idx_ref[0, kv, j:j + 1, :] = first.astype(jnp.int32)

        k_w = jnp.concatenate([win_ref[0, :, _slab(kv)], new_ref[0, :, _slab(kv)]], axis=0).astype(BF16)
        v_slab = slice(KV_W + (kv // 2) * LANES, KV_W + (kv // 2 + 1) * LANES)
        v_w = jnp.concatenate([win_ref[0, :, v_slab], new_ref[0, :, v_slab]], axis=0)
        s = _dot(k_w, x)
        s = jnp.concatenate([jnp.where(keep_w, s[:, g * TQ:(g + 1) * TQ], NEG) for g in range(GROUP)], axis=1)
        e = jnp.exp(s - jnp.max(s, axis=0, keepdims=True))
        p_w = e / jnp.sum(e, axis=0, keepdims=True)
        o_w = _dot_tn(v_w, p_w)[(kv % 2) * HEAD_DIM:(kv % 2 + 1) * HEAD_DIM]
        heads += _gate_mix(g_t, kv, ((0, o_c), (2, o_w)))
    part_ref[0] = jnp.concatenate(heads, axis=0)


def _attn_sample(q_t, g_t, kc, vc_t, win, new, *, past, n_blk):
    b = q_t.shape[0]
    half = kc.shape[1] // 2
    n_top = min(TOP_N, n_blk)
    blk3 = lambda a: pl.BlockSpec((1,) + a.shape[1:], lambda bi: (bi, 0, 0))
    return pl.pallas_call(
        functools.partial(_attn_sample_kernel, past=past, n_blk=n_blk, half=half),
        grid=(b,),
        in_specs=[blk3(q_t), blk3(g_t), blk3(kc), blk3(vc_t), blk3(win), blk3(new)],
        out_specs=[pl.BlockSpec((1, N_Q, TQ), lambda bi: (bi, 0, 0)),
                   pl.BlockSpec((1, N_KV_HEADS, n_top, TQ), lambda bi: (bi, 0, 0, 0))],
        out_shape=[jax.ShapeDtypeStruct((b, N_Q, TQ), F32), jax.ShapeDtypeStruct((b, N_KV_HEADS, n_top, TQ), jnp.int32)],
        compiler_params=_params("parallel"),
        name="attn_sample",
    )(q_t, g_t, kc, vc_t, win, new)


def _sel_sample_kernel(half_ref, bid_ref, x_ref, *refs, past, n_past_blk, n_top):
    del half_ref
    blk_refs = refs[:N_KV_HEADS * n_top]
    new_ref, o_ref = refs[N_KV_HEADS * n_top:]
    bi, ti = pl.program_id(0), pl.program_id(1)
    base = (bi * pl.num_programs(1) + ti) * N_KV_HEADS * n_top
    t_pos = past + ti
    row = lax.broadcasted_iota(jnp.int32, (L_SEL, 1), 0)
    for kv in range(N_KV_HEADS):
        v_slab = slice(KV_W + (kv // 2) * LANES, KV_W + (kv // 2 + 1) * LANES)
        ks, vs, keep = [], [], []
        for j in range(n_top):
            bid = bid_ref[base + kv * n_top + j]
            from_past = bid < n_past_blk
            ref = blk_refs[kv * n_top + j]
            ks.append(jnp.where(from_past, ref[:, _slab(kv)], new_ref[:, _slab(kv)]).astype(BF16))
            vs.append(jnp.where(from_past, ref[:, v_slab], new_ref[:, v_slab]))
            keep.append(bid * L_SEL + row <= t_pos)
        s = jnp.where(jnp.concatenate(keep, axis=0), _dot(jnp.concatenate(ks, axis=0), x_ref[kv]), NEG)
        e = jnp.exp(s - jnp.max(s, axis=0, keepdims=True))
        p = e / jnp.sum(e, axis=0, keepdims=True)
        o_ref[kv] = _dot_tn(p, jnp.concatenate(vs, axis=0))[0:8]


def _sel_sample(half_ids, blk_ids, x_q, cache_halves, new_rows, *, past, n_past_blk, n_top):
    b, t, n_kv = x_q.shape[:3]
    row_w = cache_halves.shape[-1]
    blk = lambda j: pl.BlockSpec((None, L_SEL, row_w), lambda bi, ti, hid, bid: (hid[(bi * t + ti) * n_kv * n_top + j], 0, 0))
    return pl.pallas_call(
        functools.partial(_sel_sample_kernel, past=past, n_past_blk=n_past_blk, n_top=n_top),
        grid_spec=pltpu.PrefetchScalarGridSpec(
            num_scalar_prefetch=2,
            grid=(b, t),
            in_specs=[pl.BlockSpec((None, None, n_kv, LANES, LANES), lambda bi, ti, hid, bid: (bi, ti, 0, 0, 0))]
            + [blk(j) for j in range(n_kv * n_top)]
            + [pl.BlockSpec((None, L_SEL, row_w), lambda bi, ti, hid, bid: (bi, 0, 0))],
            out_specs=pl.BlockSpec((None, None, n_kv, 8, LANES), lambda bi, ti, hid, bid: (bi, ti, 0, 0, 0)),
        ),
        out_shape=jax.ShapeDtypeStruct((b, t, n_kv, 8, LANES), F32),
        compiler_params=_params("parallel", "arbitrary"),
        name="sel_sample",
    )(half_ids, blk_ids, x_q, *([cache_halves] * (n_kv * n_top)), new_rows)


def _out_sample_kernel(x_ref, part_ref, os_ref, gs_ref, wout_ref, o_ref):
    o = (part_ref[...] + gs_ref[...] * os_ref[...]).astype(BF16)
    o_ref[...] = x_ref[...] + _dot(o, wout_ref[...])


def _out_sample(x, part, o_s, g_s, w_out):
    return pl.pallas_call(
        _out_sample_kernel,
        out_shape=jax.ShapeDtypeStruct(x.shape, F32),
        compiler_params=pltpu.CompilerParams(vmem_limit_bytes=VMEM_LIMIT),
        name="out_sample",
    )(x, part, o_s, g_s, w_out)


def _even_odd(a, axis, pad_to=None):
    parts = []
    for start in (0, 1):
        part = lax.slice_in_dim(a, start, a.shape[axis], stride=2, axis=axis)
        if pad_to is not None:
            widths = [(0, 0)] * a.ndim
            widths[axis] = (0, pad_to - part.shape[axis])
            part = jnp.pad(part, widths)
        parts.append(part)
    return jnp.concatenate(parts, axis=axis)


def kernel(x_prompt, x_sample, state_conv, cache_cmp_kv, cache_sel_kv, cache_win_kv, page_table, norm_ffa, w_ffa_gu, w_ffa_down, norm_mix, norm_ffb, w_ffb_gu, w_ffb_down, w_conv_in, w_conv, w_conv_out, w_nsa_in, pe_cmp, w_cmp_k1, w_cmp_k2, w_cmp_v1, w_cmp_v2, w_nsa_out, norm_final):
    bp, tp, _ = x_prompt.shape
    bs, ts, _ = x_sample.shape
    n_p, n_s = bp * tp, bs * ts
    xp = x_prompt.reshape(n_p, D_MODEL)
    xs = x_sample.reshape(n_s, D_MODEL)

    def ffn_pair(xp, xs, g, w_gu, w_down, final=False):
        w_gu, w_down = w_gu.astype(BF16), w_down.astype(BF16)
        return (_ffn(xp, g, w_gu, w_down, norm_final, tm=512, final=final),
                _ffn(xs, g, w_gu, w_down, norm_final, tm=n_s, final=final))

    xp, xs = ffn_pair(xp, xs, norm_ffa[0], w_ffa_gu[0], w_ffa_down[0])
    w_in, w_out = w_conv_in[0].astype(BF16), w_conv_out[0].astype(BF16)
    xp3, conv_p = _conv_prompt(xp.reshape(bp, tp, D_MODEL), norm_mix[0], w_in, w_conv[0], w_out)
    xs3, conv_s = _conv_sample(xs.reshape(bs, ts, D_MODEL), state_conv[0], norm_mix[0], w_in, w_conv[0], w_out)
    xp, xs = ffn_pair(xp3.reshape(n_p, D_MODEL), xs3.reshape(n_s, D_MODEL), norm_ffb[0], w_ffb_gu[0], w_ffb_down[0])

    xp, xs = ffn_pair(xp, xs, norm_ffa[1], w_ffa_gu[1], w_ffa_down[1])
    proj_w = _nsa_weights(w_nsa_in[0])
    cmp_w = _compress_weights(pe_cmp[0], w_cmp_k1[0], w_cmp_k2[0], w_cmp_v1[0], w_cmp_v2[0])
    w_out = w_nsa_out[0].astype(BF16)
    kv_shape = (2, N_KV_HEADS, HEAD_DIM)
    row_w = 2 * KV_W

    q_t, kvc, kvs, kvw, ks, kw, vs_t, vw_t, g_t = _nsa_project(xp, norm_mix[1], proj_w, tm=512)
    n_cmp = tp // L_CMP
    kc, vc_t = _compress(kvc.reshape(bp * n_cmp, L_CMP * row_w), cmp_w, tm=512)
    kc = _even_odd(kc.reshape(bp, n_cmp, KV_W), 1)
    vc_t = _even_odd(vc_t.reshape(KV_W, bp, n_cmp), 2).transpose(1, 0, 2)
    seq = lambda a: a.reshape(bp, tp, a.shape[-1])
    xp3 = _attn_prompt(seq(xp), q_t, g_t, kc, vc_t, seq(ks), vs_t, seq(kw), vw_t, w_out)
    cmp_p = kvc.reshape((1, bp, tp) + kv_shape)
    sel_p = kvs.reshape((1, bp, tp) + kv_shape)
    win_p = kvw.reshape((bp, tp) + kv_shape)[None, :, tp - min(WINDOW, tp):]

    q_t, kvc, kvs, kvw, _, _, _, _, g_t = _nsa_project(xs, norm_mix[1], proj_w, tm=n_s)
    n_pages = page_table.shape[1]
    past = n_pages * PAGE_SIZE
    n_new = -(-ts // L_SEL) * L_SEL
    n_past_blk = past // L_SEL
    n_blk = n_past_blk + n_new // L_SEL
    n_top = min(TOP_N, n_blk)
    lanes = lambda a: jnp.pad(a.reshape(a.shape[0], bs, ts).transpose(1, 0, 2), ((0, 0), (0, 0), (0, TQ - ts)))
    kc_past, vct_past = _compress_paged(cache_cmp_kv[0].reshape(-1, PAGE_SIZE, row_w), page_table, cmp_w)
    new_c = jnp.pad(kvc.reshape(bs, ts, row_w), ((0, 0), (0, n_new - ts), (0, 0)))
    kc_new, vct_new = _compress(new_c.reshape(bs * n_new // L_CMP, L_CMP * row_w), cmp_w, tm=bs * n_new // L_CMP)
    half = -(-n_blk // LANES) * LANES
    kc_all = _even_odd(jnp.concatenate([kc_past.reshape(bs, -1, KV_W), kc_new.reshape(bs, -1, KV_W)], axis=1), 1, half)
    vct_all = _even_odd(jnp.concatenate([vct_past.reshape(KV_W, bs, -1), vct_new.reshape(KV_W, bs, -1)], axis=2), 2, half)
    wb = cache_win_kv.shape[2]
    new_w = jnp.pad(kvw.reshape(bs, ts, row_w), ((0, 0), (0, NEW_ROWS - ts), (0, 0)))
    part, idx = _attn_sample(lanes(q_t), lanes(g_t), kc_all, vct_all.transpose(1, 0, 2),
                             cache_win_kv[0].reshape(bs, wb, row_w), new_w, past=past, n_blk=n_blk)
    bid = idx[..., :ts].transpose(0, 3, 1, 2)
    spp = PAGE_SIZE // L_SEL
    ip = jnp.minimum(bid, n_past_blk - 1)
    hid = page_table[jnp.arange(bs)[:, None, None, None], ip // spp] * spp + ip % spp
    q5 = q_t.reshape(N_KV_HEADS, GROUP, HEAD_DIM, bs, ts).transpose(3, 4, 0, 2, 1)
    q5 = jnp.pad(q5, ((0, 0),) * 4 + ((0, LANES - GROUP),))
    zq = jnp.zeros_like(q5)
    even = (jnp.arange(N_KV_HEADS) % 2 == 0)[None, None, :, None, None]
    x_q = jnp.where(even, jnp.concatenate([q5, zq], axis=3), jnp.concatenate([zq, q5], axis=3))
    new_s = jnp.pad(kvs.reshape(bs, ts, row_w), ((0, 0), (0, L_SEL - ts), (0, 0)))
    o_s = _sel_sample(hid.reshape(-1).astype(jnp.int32), bid.reshape(-1), x_q,
                      cache_sel_kv[0].reshape(-1, L_SEL, row_w), new_s, past=past, n_past_blk=n_past_blk, n_top=n_top)
    o_s = o_s[:, :, :, :GROUP].reshape(bs, ts, N_KV_HEADS // 2, 2, GROUP, 2, HEAD_DIM)
    o_s = jnp.stack([o_s[:, :, :, 0, :, 0], o_s[:, :, :, 1, :, 1]], axis=3).reshape(n_s, N_Q)
    g_s = jnp.repeat(g_t[N_HEADS:2 * N_HEADS].T, HEAD_DIM, axis=1)
    part = part[:, :, :ts].transpose(0, 2, 1).reshape(n_s, N_Q)
    xs = _out_sample(xs, part, o_s, g_s, w_out)
    cmp_s = kvc.reshape((1, bs, ts) + kv_shape)
    sel_s = kvs.reshape((1, bs, ts) + kv_shape)
    win_s = jnp.concatenate([cache_win_kv[0], kvw.reshape((bs, ts) + kv_shape)], axis=1)[None, :, ts:]

    xp, xs = ffn_pair(xp3.reshape(n_p, D_MODEL), xs, norm_ffb[1], w_ffb_gu[1], w_ffb_down[1], final=True)
    return (xp.reshape(bp, tp, D_MODEL), xs.reshape(bs, ts, D_MODEL),
            conv_p[None], conv_s[None], cmp_p, cmp_s, sel_p, sel_s, win_p, win_s)
```

```python
import functools

import jax
import jax.numpy as jnp
from jax import lax
from jax.experimental import pallas as pl
from jax.experimental.pallas import tpu as pltpu

D_MODEL = 1024
D_FF = 2816
CONV_W = 3
N_HEADS = 16
HEAD_DIM = 64
N_KV_HEADS = 4
GROUP = N_HEADS // N_KV_HEADS
KV_W = N_KV_HEADS * HEAD_DIM
L_CMP = 32
L_SEL = 64
TOP_N = 16
WINDOW = 512
PAGE_SIZE = 128
NORM_EPS = 1e-6
FORCE_BONUS = 1e4
NEG = -1e30

LANES = 128
VMEM_LIMIT = 56 * 1024 * 1024

F32 = jnp.float32
BF16 = jnp.bfloat16


def _params(*sem):
    return pltpu.CompilerParams(dimension_semantics=sem, vmem_limit_bytes=VMEM_LIMIT)


def _rms(x, g):
    return x * lax.rsqrt(jnp.mean(x * x, axis=-1, keepdims=True) + NORM_EPS) * g


def _dot(a, b):
    return jnp.dot(a, b, preferred_element_type=F32)


def _dot_nt(a, b):
    return lax.dot_general(a, b, (((1,), (1,)), ((), ())), preferred_element_type=F32)


def _dot_tn(a, b):
    return lax.dot_general(a, b, (((0,), (0,)), ((), ())), preferred_element_type=F32)


def _ffn_kernel(x_ref, g_ref, wg_ref, wu_ref, wd_ref, gf_ref, o_ref, *, final):
    x = x_ref[...]
    xn = _rms(x, g_ref[...]).astype(BF16)
    gate = _dot(xn, wg_ref[...])
    up = _dot(xn, wu_ref[...])
    act = (gate * jax.nn.sigmoid(gate) * up).astype(BF16)
    y = x + 0.5 * _dot(act, wd_ref[...])
    o_ref[...] = _rms(y, gf_ref[...]) if final else y


def _resident(shape, index):
    return pl.BlockSpec(shape, index, pipeline_mode=pl.Buffered(1))


def _ffn(x, g, w_gu, w_down, g_final, *, tm, final=False):
    n = x.shape[0]
    return pl.pallas_call(
        functools.partial(_ffn_kernel, final=final),
        grid=(n // tm,),
        in_specs=[
            pl.BlockSpec((tm, D_MODEL), lambda i: (i, 0)),
            pl.BlockSpec((1, D_MODEL), lambda i: (0, 0)),
            _resident((D_MODEL, D_FF), lambda i: (0, 0)),
            _resident((D_MODEL, D_FF), lambda i: (0, 1)),
            _resident((D_FF, D_MODEL), lambda i: (0, 0)),
            pl.BlockSpec((1, D_MODEL), lambda i: (0, 0)),
        ],
        out_specs=pl.BlockSpec((tm, D_MODEL), lambda i: (i, 0)),
        out_shape=jax.ShapeDtypeStruct((n, D_MODEL), F32),
        compiler_params=_params("parallel"),
        name="ffn_final" if final else "ffn",
    )(x, g.reshape(1, D_MODEL), w_gu, w_gu, w_down, g_final.reshape(1, D_MODEL))


CARRY = 8


def _conv_prompt_kernel(x_ref, g_ref, win_ref, wc_ref, wout_ref, o_ref, st_ref, uext_ref, *, tm):
    @pl.when(pl.program_id(1) == 0)
    def _():
        uext_ref[0:CARRY, :] = jnp.zeros((CARRY, D_MODEL), F32)

    x = x_ref[0]
    h = _rms(x, g_ref[...]).astype(BF16)
    p = _dot(h, win_ref[...])
    bg = p[:, :D_MODEL]
    u = p[:, D_MODEL:2 * D_MODEL] * p[:, 2 * D_MODEL:]
    uext_ref[CARRY:CARRY + tm, :] = u
    wc = wc_ref[...]
    conv = (wc[0:1] * uext_ref[CARRY - 2:CARRY - 2 + tm, :]
            + wc[1:2] * uext_ref[CARRY - 1:CARRY - 1 + tm, :]
            + wc[2:3] * u)
    y = _dot((bg * conv).astype(BF16), wout_ref[...])
    o_ref[0] = x + y
    st_ref[0] = uext_ref[CARRY + tm - 2:CARRY + tm, :]
    uext_ref[0:CARRY, :] = uext_ref[tm:tm + CARRY, :]


def _conv_prompt(x, g, w_in, w_conv, w_out, *, tm=512):
    b, t, _ = x.shape
    return pl.pallas_call(
        functools.partial(_conv_prompt_kernel, tm=tm),
        grid=(b, t // tm),
        in_specs=[
            pl.BlockSpec((1, tm, D_MODEL), lambda bi, ti: (bi, ti, 0)),
            pl.BlockSpec((1, D_MODEL), lambda bi, ti: (0, 0)),
            pl.BlockSpec((D_MODEL, 3 * D_MODEL), lambda bi, ti: (0, 0)),
            pl.BlockSpec((CONV_W, D_MODEL), lambda bi, ti: (0, 0)),
            pl.BlockSpec((D_MODEL, D_MODEL), lambda bi, ti: (0, 0)),
        ],
        out_specs=[
            pl.BlockSpec((1, tm, D_MODEL), lambda bi, ti: (bi, ti, 0)),
            pl.BlockSpec((1, CONV_W - 1, D_MODEL), lambda bi, ti: (bi, 0, 0)),
        ],
        out_shape=[jax.ShapeDtypeStruct((b, t, D_MODEL), F32),
                   jax.ShapeDtypeStruct((b, CONV_W - 1, D_MODEL), F32)],
        scratch_shapes=[pltpu.VMEM((CARRY + tm, D_MODEL), F32)],
        compiler_params=_params("parallel", "arbitrary"),
        name="conv_prompt",
    )(x, g.reshape(1, D_MODEL), w_in, w_conv, w_out)


def _conv_sample_kernel(x_ref, g_ref, win_ref, wc_ref, wout_ref, s1_ref, s2_ref, o_ref, u_ref, *, t_len):
    x = x_ref[...]
    h = _rms(x, g_ref[...]).astype(BF16)
    p = _dot(h, win_ref[...])
    bg = p[:, :D_MODEL]
    u = p[:, D_MODEL:2 * D_MODEL] * p[:, 2 * D_MODEL:]
    pos = lax.broadcasted_iota(jnp.int32, u.shape, 0) % t_len
    u1 = jnp.where(pos >= 1, pltpu.roll(u, 1, axis=0), s1_ref[...])
    u2 = jnp.where(pos >= 2, pltpu.roll(u, 2, axis=0), s2_ref[...])
    wc = wc_ref[...]
    conv = wc[0:1] * u2 + wc[1:2] * u1 + wc[2:3] * u
    o_ref[...] = x + _dot((bg * conv).astype(BF16), wout_ref[...])
    u_ref[...] = u


def _conv_sample(x, state, g, w_in, w_conv, w_out):
    b, t, _ = x.shape
    n = b * t
    zeros = jnp.zeros((b, t, D_MODEL), F32)
    s1 = zeros.at[:, 0].set(state[:, 1]).reshape(n, D_MODEL)
    s2 = zeros.at[:, 0].set(state[:, 0]).at[:, 1].set(state[:, 1]).reshape(n, D_MODEL)
    y, u = pl.pallas_call(
        functools.partial(_conv_sample_kernel, t_len=t),
        out_shape=[jax.ShapeDtypeStruct((n, D_MODEL), F32), jax.ShapeDtypeStruct((n, D_MODEL), F32)],
        compiler_params=pltpu.CompilerParams(vmem_limit_bytes=VMEM_LIMIT),
        name="conv_sample",
    )(x.reshape(n, D_MODEL), g.reshape(1, D_MODEL), w_in, w_conv, w_out, s1, s2)
    return y.reshape(b, t, D_MODEL), u.reshape(b, t, D_MODEL)[:, t - (CONV_W - 1):]


N_Q = N_HEADS * HEAD_DIM


def _proj_common(x_ref, g_ref, wqt_ref, wgt_ref, qt_ref, gt_ref):
    h = _rms(x_ref[...], g_ref[...]).astype(BF16)
    qt_ref[...] = (_dot_nt(wqt_ref[...], h) * (HEAD_DIM ** -0.5)).astype(BF16)
    gt_ref[...] = jax.nn.sigmoid(_dot_nt(wgt_ref[...], h))
    return h


def _proj_prompt_kernel(x_ref, g_ref, wqt_ref, wgt_ref, wkvt_ref, wk_ref,
                        qt_ref, gt_ref, kvct_ref, kvst_ref, kvwt_ref, ks_ref, kw_ref, vst_ref, vwt_ref):
    h = _proj_common(x_ref, g_ref, wqt_ref, wgt_ref, qt_ref, gt_ref)
    kv_t = _dot_nt(wkvt_ref[...], h)
    kvct_ref[0] = kv_t[:2 * KV_W]
    kvst_ref[0] = kv_t[2 * KV_W:4 * KV_W]
    kvwt_ref[0] = kv_t[4 * KV_W:]
    vst_ref[...] = kv_t[3 * KV_W:4 * KV_W].astype(BF16)
    vwt_ref[...] = kv_t[5 * KV_W:].astype(BF16)
    k = _dot(h, wk_ref[...]).astype(BF16)
    ks_ref[...] = k[:, :KV_W]
    kw_ref[...] = k[:, KV_W:]


def _proj_sample_kernel(x_ref, g_ref, wqt_ref, wgt_ref, wkv_ref, qt_ref, gt_ref, kvc_ref, kvs_ref, kvw_ref):
    h = _proj_common(x_ref, g_ref, wqt_ref, wgt_ref, qt_ref, gt_ref)
    kv = _dot(h, wkv_ref[...])
    kvc_ref[...] = kv[:, :2 * KV_W]
    kvs_ref[...] = kv[:, 2 * KV_W:4 * KV_W]
    kvw_ref[...] = kv[:, 4 * KV_W:]


def _nsa_weights(w_in):
    wkv = w_in[:, N_Q:N_Q + 6 * KV_W]
    wk = jnp.concatenate([wkv[:, 2 * KV_W:3 * KV_W], wkv[:, 4 * KV_W:5 * KV_W]], axis=1)
    wgt = jnp.pad(w_in[:, N_Q + 6 * KV_W:], ((0, 0), (0, LANES - 3 * N_HEADS))).T
    return dict(wqt=w_in[:, :N_Q].T.astype(BF16), wgt=wgt.astype(BF16), wkv=wkv.astype(BF16),
                wkvt=wkv.T.astype(BF16), wk=wk.astype(BF16))


def _nsa_project_prompt(x, g, w, *, seq_len, tm):
    n = x.shape[0]
    per_seq = seq_len // tm
    row = lambda w_: pl.BlockSpec((tm, w_), lambda i: (i, 0))
    col = lambda h: pl.BlockSpec((h, tm), lambda i: (0, i))
    seq_t = pl.BlockSpec((1, 2 * KV_W, tm), lambda i: (i // per_seq, 0, i % per_seq))
    full = lambda a: pl.BlockSpec(a.shape, lambda i: (0, 0))
    weights = [w["wqt"], w["wgt"], w["wkvt"], w["wk"]]
    return pl.pallas_call(
        _proj_prompt_kernel,
        grid=(n // tm,),
        in_specs=[row(D_MODEL), pl.BlockSpec((1, D_MODEL), lambda i: (0, 0))] + [full(a) for a in weights],
        out_specs=[col(N_Q), col(LANES), seq_t, seq_t, seq_t, row(KV_W), row(KV_W), col(KV_W), col(KV_W)],
        out_shape=[jax.ShapeDtypeStruct((N_Q, n), BF16), jax.ShapeDtypeStruct((LANES, n), F32)]
        + [jax.ShapeDtypeStruct((n // seq_len, 2 * KV_W, seq_len), F32)] * 3
        + [jax.ShapeDtypeStruct((n, KV_W), BF16)] * 2
        + [jax.ShapeDtypeStruct((KV_W, n), BF16)] * 2,
        compiler_params=_params("parallel"),
        name="nsa_project_prompt",
    )(x, g.reshape(1, D_MODEL), *weights)


def _nsa_project_sample(x, g, w):
    n = x.shape[0]
    return pl.pallas_call(
        _proj_sample_kernel,
        out_shape=[jax.ShapeDtypeStruct((N_Q, n), BF16), jax.ShapeDtypeStruct((LANES, n), F32)]
        + [jax.ShapeDtypeStruct((n, 2 * KV_W), F32)] * 3,
        compiler_params=pltpu.CompilerParams(vmem_limit_bytes=VMEM_LIMIT),
        name="nsa_project_sample",
    )(x, g.reshape(1, D_MODEL), w["wqt"], w["wgt"], w["wkv"])


def _compress_out(acck, accv, w2k_ref, w2vt_ref, kc_ref, vct_ref):
    kc_ref[...] = _dot(jax.nn.gelu(acck).astype(BF16), w2k_ref[...]).astype(BF16)
    vct_ref[...] = _dot_nt(w2vt_ref[...], jax.nn.gelu(accv).astype(BF16)).astype(BF16)


def _compress_kernel(x_ref, pe_ref, w1k_ref, w1v_ref, w2k_ref, w2vt_ref, kc_ref, vct_ref, acck_ref, accv_ref):
    l = pl.program_id(1)

    @pl.when(l == 0)
    def _():
        acck_ref[...] = jnp.zeros_like(acck_ref)
        accv_ref[...] = jnp.zeros_like(accv_ref)

    xb = x_ref[...] + pe_ref[...]
    acck_ref[...] += _dot(xb[:, :KV_W].astype(BF16), w1k_ref[...])
    accv_ref[...] += _dot(xb[:, KV_W:].astype(BF16), w1v_ref[...])

    @pl.when(l == L_CMP - 1)
    def _():
        _compress_out(acck_ref[...], accv_ref[...], w2k_ref, w2vt_ref, kc_ref, vct_ref)


PAGES_PER_STEP = 32
BLOCKS_PER_PAGE = PAGE_SIZE // L_CMP


def _compress_weights(pe, w1k, w2k, w1v, w2v):
    eye = jnp.eye(N_KV_HEADS, dtype=F32)
    bd1 = lambda w1: jax.vmap(lambda w: jnp.kron(eye, w))(w1).astype(BF16)
    pe_row = jnp.tile(pe, (1, 2 * N_KV_HEADS))
    r = jnp.arange(2 * PAGE_SIZE)
    src = (r % 8 // BLOCKS_PER_PAGE) * PAGE_SIZE + (r % BLOCKS_PER_PAGE) * L_CMP + r // 8
    perm = (src[:, None] == r[None, :]).astype(BF16)
    return dict(pe_row=pe_row.reshape(L_CMP, 1, 2 * KV_W), pe_page_t=jnp.tile(pe_row, (BLOCKS_PER_PAGE, 1)).T, perm=perm,
                w1k=bd1(w1k), w1v=bd1(w1v), w2k=jnp.kron(eye, w2k).astype(BF16), w2vt=jnp.kron(eye, w2v).T.astype(BF16))


def _compress(x, w, *, tm):
    m = x.shape[0]
    tm = min(tm, m)
    return pl.pallas_call(
        _compress_kernel,
        grid=(m // tm, L_CMP),
        in_specs=[
            pl.BlockSpec((tm, 2 * KV_W), lambda i, l: (i, l)),
            pl.BlockSpec((None, 1, 2 * KV_W), lambda i, l: (l, 0, 0)),
            pl.BlockSpec((None, KV_W, KV_W), lambda i, l: (l, 0, 0)),
            pl.BlockSpec((None, KV_W, KV_W), lambda i, l: (l, 0, 0)),
            pl.BlockSpec((KV_W, KV_W), lambda i, l: (0, 0)),
            pl.BlockSpec((KV_W, KV_W), lambda i, l: (0, 0)),
        ],
        out_specs=[pl.BlockSpec((tm, KV_W), lambda i, l: (i, 0)), pl.BlockSpec((KV_W, tm), lambda i, l: (0, i))],
        out_shape=[jax.ShapeDtypeStruct((m, KV_W), BF16), jax.ShapeDtypeStruct((KV_W, m), BF16)],
        scratch_shapes=[pltpu.VMEM((tm, KV_W), F32)] * 2,
        compiler_params=_params("parallel", "arbitrary"),
        name="compress",
    )(x, w["pe_row"], w["w1k"], w["w1v"], w["w2k"], w["w2vt"])


def _compress_pages_body(refs, n_pages):
    pages = refs[:n_pages]
    pe_ref, perm_ref, w1k_ref, w1v_ref, w2k_ref, w2vt_ref, kc_ref, vct_ref = refs[n_pages:]
    pe = pe_ref[...]
    perm = perm_ref[...]
    by_l = []
    for p in range(0, n_pages, 2):
        pair_t = jnp.concatenate([(pages[p][...] + pe).astype(BF16), (pages[p + 1][...] + pe).astype(BF16)], axis=1)
        by_l.append(_dot_nt(perm, pair_t))
    acck = jnp.zeros((n_pages * BLOCKS_PER_PAGE, KV_W), F32)
    accv = jnp.zeros((n_pages * BLOCKS_PER_PAGE, KV_W), F32)
    for l in range(L_CMP):
        xl = jnp.concatenate([y[l * 8:(l + 1) * 8] for y in by_l], axis=0).astype(BF16)
        acck = acck + _dot(xl[:, :KV_W], w1k_ref[l])
        accv = accv + _dot(xl[:, KV_W:], w1v_ref[l])
    _compress_out(acck, accv, w2k_ref, w2vt_ref, kc_ref, vct_ref)


def _compress_paged_kernel(pt_ref, *refs, n_pages):
    del pt_ref
    _compress_pages_body(refs, n_pages)


def _compress_seq_kernel(*refs, n_pages):
    _compress_pages_body(refs, n_pages)


def _compress_consts(w):
    return [w["pe_page_t"], w["perm"], w["w1k"], w["w1v"], w["w2k"], w["w2vt"]]


def _compress_paged(cache, page_table, w):
    b, n_pages = page_table.shape
    pps = min(PAGES_PER_STEP, n_pages)
    steps = n_pages // pps
    m_step = pps * BLOCKS_PER_PAGE
    m = b * n_pages * BLOCKS_PER_PAGE
    page = lambda j: pl.BlockSpec((None, 2 * KV_W, PAGE_SIZE), lambda bi, s, pt: (pt[bi, s * pps + j], 0, 0))
    full = lambda a: pl.BlockSpec(a.shape, lambda bi, s, pt: (0,) * a.ndim)
    consts = _compress_consts(w)
    return pl.pallas_call(
        functools.partial(_compress_paged_kernel, n_pages=pps),
        grid_spec=pltpu.PrefetchScalarGridSpec(
            num_scalar_prefetch=1,
            grid=(b, steps),
            in_specs=[page(j) for j in range(pps)] + [full(a) for a in consts],
            out_specs=[pl.BlockSpec((m_step, KV_W), lambda bi, s, pt: (bi * steps + s, 0)),
                       pl.BlockSpec((KV_W, m_step), lambda bi, s, pt: (0, bi * steps + s))],
        ),
        out_shape=[jax.ShapeDtypeStruct((m, KV_W), BF16), jax.ShapeDtypeStruct((KV_W, m), BF16)],
        compiler_params=_params("parallel", "arbitrary"),
        name="compress_paged",
    )(page_table, *([cache] * pps), *consts)


def _compress_seq(rows_t, w):
    b, _, t = rows_t.shape
    per_seq = t // PAGE_SIZE
    n_pages = b * per_seq
    pps = min(PAGES_PER_STEP, n_pages)
    m_step = pps * BLOCKS_PER_PAGE
    m = n_pages * BLOCKS_PER_PAGE
    page = lambda j: pl.BlockSpec((None, 2 * KV_W, PAGE_SIZE),
                                  lambda s: ((s * pps + j) // per_seq, 0, (s * pps + j) % per_seq))
    full = lambda a: pl.BlockSpec(a.shape, lambda s: (0,) * a.ndim)
    consts = _compress_consts(w)
    return pl.pallas_call(
        functools.partial(_compress_seq_kernel, n_pages=pps),
        grid=(n_pages // pps,),
        in_specs=[page(j) for j in range(pps)] + [full(a) for a in consts],
        out_specs=[pl.BlockSpec((m_step, KV_W), lambda s: (s, 0)), pl.BlockSpec((KV_W, m_step), lambda s: (0, s))],
        out_shape=[jax.ShapeDtypeStruct((m, KV_W), BF16), jax.ShapeDtypeStruct((KV_W, m), BF16)],
        compiler_params=_params("parallel"),
        name="compress_seq",
    )(*([rows_t] * pps), *consts)


TQ = 128
COLS = GROUP * TQ
KCH = 512
WIN_BLOCKS = WINDOW // TQ + 1


def _head_queries(q_t, kv):
    zero = jnp.zeros((HEAD_DIM, TQ), BF16)
    cols = []
    for g in range(GROUP):
        h = kv * GROUP + g
        qg = q_t[h * HEAD_DIM:(h + 1) * HEAD_DIM, :]
        cols.append(jnp.concatenate([qg, zero] if kv % 2 == 0 else [zero, qg], axis=0))
    return jnp.concatenate(cols, axis=1)


def _slab(kv):
    return slice((kv // 2) * LANES, (kv // 2 + 1) * LANES)


def _head(kv):
    return slice(kv * HEAD_DIM, (kv + 1) * HEAD_DIM)


def _cmp_branch(kc, vc_t, x, keep):
    s = jnp.where(keep, _dot(kc, x), NEG)
    e = jnp.exp(s - jnp.max(s, axis=0, keepdims=True))
    l = jnp.sum(e, axis=0, keepdims=True)
    e = jnp.where(keep, e, 0.0)
    return _dot(vc_t, e.astype(BF16)) / l, e / l


def _group_sum(p):
    out = p[:, 0:TQ]
    for g in range(1, GROUP):
        out = out + p[:, g * TQ:(g + 1) * TQ]
    return out


def _topn_rows(score, n):
    nb = score.shape[0]
    row = lax.broadcasted_iota(jnp.int32, score.shape, 0).astype(F32)
    picked = jnp.zeros(score.shape, F32)
    ids = []
    for _ in range(n):
        top = jnp.max(score, axis=0, keepdims=True)
        first = jnp.min(jnp.where(score == top, row, float(nb)), axis=0, keepdims=True)
        hit = row == first
        picked = jnp.where(hit, 1.0, picked)
        score = jnp.where(hit, -jnp.inf, score)
        ids.append(first)
    return picked, ids


def _masked_softmax_pv(s, keep, v_t):
    s = jnp.concatenate([jnp.where(keep, s[:, g * TQ:(g + 1) * TQ], NEG) for g in range(GROUP)], axis=1)
    e = jnp.exp(s - jnp.max(s, axis=0, keepdims=True))
    return _dot(v_t, e.astype(BF16)) / jnp.sum(e, axis=0, keepdims=True)


def _gate_mix(g_t, kv, branches):
    out = []
    for g in range(GROUP):
        h = kv * GROUP + g
        c = slice(g * TQ, (g + 1) * TQ)
        acc = None
        for br, o in branches:
            term = g_t[br * N_HEADS + h:br * N_HEADS + h + 1, :] * o[:, c]
            acc = term if acc is None else acc + term
        out.append(acc)
    return out


def _attn_prompt_kernel(x_ref, qt_ref, gt_ref, kc_ref, vct_ref, ks_ref, vst_ref, kw_ref, *rest, t_len):
    vwt_refs = rest[:WIN_BLOCKS]
    wout_ref, o_ref, mask_ref, m_ref, l_ref, acc_ref = rest[WIN_BLOCKS:]
    t0 = pl.program_id(1) * TQ
    n_sel = t_len // L_SEL
    n_cmp = t_len // L_CMP
    t_tok = t0 + lax.broadcasted_iota(jnp.int32, (1, TQ), 1)
    t_col = t0 + lax.broadcasted_iota(jnp.int32, (1, COLS), 1) % TQ
    r_c = lax.broadcasted_iota(jnp.int32, (n_cmp, 1), 0)
    keep_c = (2 * (r_c % n_sel) + r_c // n_sel + 1) * L_CMP - 1 <= t_col
    blk = lax.broadcasted_iota(jnp.int32, (n_sel, TQ), 0)
    cur = t_tok // L_SEL
    forced = (blk == 0) | (blk == cur) | (blk == cur - 1)
    valid = blk <= cur
    key_in_blk = lax.broadcasted_iota(jnp.int32, (L_SEL, 1), 0)
    win_pos = t0 - WINDOW + lax.broadcasted_iota(jnp.int32, (WIN_BLOCKS * TQ, 1), 0)
    dist = t_tok - win_pos
    keep_w = (dist >= 0) & (dist < WINDOW) & (win_pos >= 0)
    n_chunks = t_len // KCH
    last_chunk = (t0 + TQ - 1) // KCH
    q_t = qt_ref[...]
    g_t = gt_ref[...]

    o_cmp = []
    for kv in range(N_KV_HEADS):
        o_c, p = _cmp_branch(kc_ref[0, :, _slab(kv)], vct_ref[0, _head(kv), :], _head_queries(q_t, kv), keep_c)
        o_cmp.append(o_c)
        p_g = _group_sum(p)
        imp = p_g[0:n_sel] + p_g[n_sel:n_cmp]
        score = jnp.where(forced, FORCE_BONUS, jnp.where(valid, imp, -FORCE_BONUS))
        picked, _ = _topn_rows(score, min(TOP_N, n_sel))
        sel = jnp.where(valid, picked, 0.0)
        for j in range(n_sel):
            causal = j * L_SEL + key_in_blk <= t_tok
            mask_ref[kv, j * L_SEL:(j + 1) * L_SEL, :] = jnp.where(
                causal, jnp.broadcast_to(sel[j:j + 1, :], (L_SEL, TQ)), 0.0)

    m_ref[...] = jnp.full(m_ref.shape, NEG, F32)
    l_ref[...] = jnp.zeros_like(l_ref)
    acc_ref[...] = jnp.zeros_like(acc_ref)
    for c in range(n_chunks):
        @pl.when(c <= last_chunk)
        def _():
            keys = slice(c * KCH, (c + 1) * KCH)
            for kv in range(N_KV_HEADS):
                s = _dot(ks_ref[0, keys, _slab(kv)], _head_queries(q_t, kv))
                keep = mask_ref[kv, keys, :] > 0.5
                s = jnp.concatenate([jnp.where(keep, s[:, g * TQ:(g + 1) * TQ], NEG) for g in range(GROUP)], axis=1)
                m_old = m_ref[kv]
                m_new = jnp.maximum(m_old, jnp.max(s, axis=0, keepdims=True))
                alpha = jnp.exp(m_old - m_new)
                e = jnp.exp(s - m_new)
                l_ref[kv] = alpha * l_ref[kv] + jnp.sum(e, axis=0, keepdims=True)
                acc_ref[kv] = alpha * acc_ref[kv] + _dot(vst_ref[_head(kv), keys], e.astype(BF16))
                m_ref[kv] = m_new

    heads = []
    for kv in range(N_KV_HEADS):
        x = _head_queries(q_t, kv)
        s_w = []
        for j in range(WIN_BLOCKS):
            start = pl.multiple_of(jnp.maximum(t0 - WINDOW + j * TQ, 0), TQ)
            s_w.append(_dot(kw_ref[0, pl.ds(start, TQ), _slab(kv)], x))
        v_w = jnp.concatenate([r[_head(kv), :] for r in vwt_refs], axis=1)
        o_w = _masked_softmax_pv(jnp.concatenate(s_w, axis=0), keep_w, v_w)
        o_s = acc_ref[kv] / l_ref[kv]
        heads += _gate_mix(g_t, kv, ((0, o_cmp[kv]), (1, o_s), (2, o_w)))
    o = jnp.concatenate(heads, axis=0).T.astype(BF16)
    o_ref[0] = x_ref[0] + _dot(o, wout_ref[...])


def _attn_prompt(x, q_t, g_t, kc, vc_t, ks, vs_t, kw, vw_t, w_out):
    b, t, _ = x.shape
    nt = t // TQ
    n_cmp = t // L_CMP
    tile = lambda w: pl.BlockSpec((1, TQ, w), lambda bi, i: (bi, i, 0))
    tile_t = lambda h: pl.BlockSpec((h, TQ), lambda bi, i: (0, bi * nt + i))
    win_t = lambda j: pl.BlockSpec((KV_W, TQ), lambda bi, i: (0, bi * nt + jnp.maximum(i - (WIN_BLOCKS - 1) + j, 0)))
    return pl.pallas_call(
        functools.partial(_attn_prompt_kernel, t_len=t),
        grid=(b, nt),
        in_specs=[tile(D_MODEL), tile_t(N_Q), tile_t(LANES),
                  pl.BlockSpec((1, n_cmp, KV_W), lambda bi, i: (bi, 0, 0)),
                  pl.BlockSpec((1, KV_W, n_cmp), lambda bi, i: (bi, 0, 0)),
                  pl.BlockSpec((1, t, KV_W), lambda bi, i: (bi, 0, 0)),
                  pl.BlockSpec((KV_W, t), lambda bi, i: (0, bi)),
                  pl.BlockSpec((1, t, KV_W), lambda bi, i: (bi, 0, 0))]
        + [win_t(j) for j in range(WIN_BLOCKS)]
        + [pl.BlockSpec((N_Q, D_MODEL), lambda bi, i: (0, 0))],
        out_specs=tile(D_MODEL),
        out_shape=jax.ShapeDtypeStruct((b, t, D_MODEL), F32),
        scratch_shapes=[pltpu.VMEM((N_KV_HEADS, t, TQ), F32), pltpu.VMEM((N_KV_HEADS, 1, COLS), F32),
                        pltpu.VMEM((N_KV_HEADS, 1, COLS), F32), pltpu.VMEM((N_KV_HEADS, HEAD_DIM, COLS), F32)],
        compiler_params=_params("parallel", "arbitrary"),
        name="attn_prompt",
    )(x, q_t, g_t, kc, vc_t, ks, vs_t, kw, *([vw_t] * WIN_BLOCKS), w_out)


NEW_ROWS = 16


def _attn_sample_kernel(qt_ref, gt_ref, kc_ref, vct_ref, win_ref, new_ref, part_ref, idx_ref, *, past, n_blk, half):
    t_tok = past + lax.broadcasted_iota(jnp.int32, (1, TQ), 1)
    t_col = past + lax.broadcasted_iota(jnp.int32, (1, COLS), 1) % TQ
    r_c = lax.broadcasted_iota(jnp.int32, (2 * half, 1), 0)
    keep_c = (r_c % half < n_blk) & ((2 * (r_c % half) + r_c // half + 1) * L_CMP - 1 <= t_col)
    blk = lax.broadcasted_iota(jnp.int32, (half, TQ), 0)
    cur = t_tok // L_SEL
    forced = (blk == 0) | (blk == cur) | (blk == cur - 1)
    valid = blk <= cur
    wb = win_ref.shape[2]
    win_pos = past - wb + lax.broadcasted_iota(jnp.int32, (wb + NEW_ROWS, 1), 0)
    dist = t_tok - win_pos
    keep_w = (dist >= 0) & (dist < WINDOW) & (win_pos >= 0)
    q_t = qt_ref[0]
    g_t = gt_ref[0]
    heads = []
    for kv in range(N_KV_HEADS):
        x = _head_queries(q_t, kv)
        o_c, p = _cmp_branch(kc_ref[0, :, _slab(kv)], vct_ref[0, _head(kv), :], x, keep_c)
        p_g = _group_sum(p)
        imp = p_g[0:half] + p_g[half:2 * half]
        score = jnp.where(forced, FORCE_BONUS, jnp.where(valid, imp, -FORCE_BONUS))
        score = jnp.where(blk < n_blk, score, -jnp.inf)
        _, ids = _topn_rows(score, min(TOP_N, n_blk))
        for j, first in enumerate(ids):
            idx_ref[0, kv, j:j + 1, :] = first.astype(jnp.int32)

        mine = slice((kv % 2) * HEAD_DIM, (kv % 2 + 1) * HEAD_DIM)
        v_slab = slice(KV_W + (kv // 2) * LANES, KV_W + (kv // 2 + 1) * LANES)
        s = jnp.concatenate([_dot_tn(win_ref[0, _head(kv), :], x[mine].astype(F32)),
                             _dot(new_ref[0, :, _slab(kv)].astype(BF16), x)], axis=0)
        s = jnp.concatenate([jnp.where(keep_w, s[:, g * TQ:(g + 1) * TQ], NEG) for g in range(GROUP)], axis=1)
        e = jnp.exp(s - jnp.max(s, axis=0, keepdims=True))
        p_w = e / jnp.sum(e, axis=0, keepdims=True)
        v_t = win_ref[0, KV_W + kv * HEAD_DIM:KV_W + (kv + 1) * HEAD_DIM, :]
        o_w = _dot(v_t.astype(BF16), p_w[:wb].astype(BF16)) + _dot_tn(new_ref[0, :, v_slab], p_w[wb:])[mine]
        heads += _gate_mix(g_t, kv, ((0, o_c), (2, o_w)))
    part_ref[0] = jnp.concatenate(heads, axis=0)


def _attn_sample(q_t, g_t, kc, vc_t, win, new, *, past, n_blk):
    b = q_t.shape[0]
    half = kc.shape[1] // 2
    n_top = min(TOP_N, n_blk)
    blk3 = lambda a: pl.BlockSpec((1,) + a.shape[1:], lambda bi: (bi, 0, 0))
    return pl.pallas_call(
        functools.partial(_attn_sample_kernel, past=past, n_blk=n_blk, half=half),
        grid=(b,),
        in_specs=[blk3(q_t), blk3(g_t), blk3(kc), blk3(vc_t), blk3(win), blk3(new)],
        out_specs=[pl.BlockSpec((1, N_Q, TQ), lambda bi: (bi, 0, 0)),
                   pl.BlockSpec((1, N_KV_HEADS, n_top, TQ), lambda bi: (bi, 0, 0, 0))],
        out_shape=[jax.ShapeDtypeStruct((b, N_Q, TQ), F32), jax.ShapeDtypeStruct((b, N_KV_HEADS, n_top, TQ), jnp.int32)],
        compiler_params=_params("parallel"),
        name="attn_sample",
    )(q_t, g_t, kc, vc_t, win, new)


Q_ROWS = 16


def _sel_sample_kernel(page_ref, bid_ref, q_ref, *refs, past, n_past_blk, n_top):
    del page_ref
    n_slab = 2 * N_KV_HEADS * n_top
    slab_refs, (new_ref, o_ref) = refs[:n_slab], refs[n_slab:]
    bi, ti = pl.program_id(0), pl.program_id(1)
    base = (bi * pl.num_programs(1) + ti) * N_KV_HEADS * n_top
    t_pos = past + ti
    lane = lax.broadcasted_iota(jnp.int32, (1, PAGE_SIZE), 1)
    for kv in range(N_KV_HEADS):
        k_t, v_t, keep = [], [], []
        for j in range(n_top):
            bid = bid_ref[base + kv * n_top + j]
            from_past = bid < n_past_blk
            first_lane = jnp.where(from_past, (bid % (PAGE_SIZE // L_SEL)) * L_SEL, 0)
            k_t.append(jnp.where(from_past, slab_refs[2 * (kv * n_top + j)][...], new_ref[kv]).astype(BF16))
            v_t.append(jnp.where(from_past, slab_refs[2 * (kv * n_top + j) + 1][...], new_ref[N_KV_HEADS + kv]).astype(BF16))
            in_blk = lane - first_lane
            keep.append((in_blk >= 0) & (in_blk < L_SEL) & (bid * L_SEL + in_blk <= t_pos))
        s = jnp.where(jnp.concatenate(keep, axis=1), _dot(q_ref[kv], jnp.concatenate(k_t, axis=1)), NEG)
        e = jnp.exp(s - jnp.max(s, axis=1, keepdims=True))
        p = e / jnp.sum(e, axis=1, keepdims=True)
        o_ref[kv] = _dot_nt(p.astype(BF16), jnp.concatenate(v_t, axis=1))


def _sel_sample(page_ids, blk_ids, q, cache_slabs, new_slabs, *, past, n_past_blk, n_top):
    b, t, n_kv = q.shape[:3]
    per_page = 2 * n_kv

    def slab(kv, j, c):
        return pl.BlockSpec((None, HEAD_DIM, PAGE_SIZE), lambda bi, ti, pid, bid: (
            pid[((bi * t + ti) * n_kv + kv) * n_top + j] * per_page + c * n_kv + kv, 0, 0))

    return pl.pallas_call(
        functools.partial(_sel_sample_kernel, past=past, n_past_blk=n_past_blk, n_top=n_top),
        grid_spec=pltpu.PrefetchScalarGridSpec(
            num_scalar_prefetch=2,
            grid=(b, t),
            in_specs=[pl.BlockSpec((None, None, n_kv, Q_ROWS, HEAD_DIM), lambda bi, ti, pid, bid: (bi, ti, 0, 0, 0))]
            + [slab(kv, j, c) for kv in range(n_kv) for j in range(n_top) for c in range(2)]
            + [pl.BlockSpec((None, per_page, HEAD_DIM, PAGE_SIZE), lambda bi, ti, pid, bid: (bi, 0, 0, 0))],
            out_specs=pl.BlockSpec((None, None, n_kv, Q_ROWS, HEAD_DIM), lambda bi, ti, pid, bid: (bi, ti, 0, 0, 0)),
        ),
        out_shape=jax.ShapeDtypeStruct((b, t, n_kv, Q_ROWS, HEAD_DIM), F32),
        compiler_params=_params("parallel", "arbitrary"),
        name="sel_sample",
    )(page_ids, blk_ids, q, *([cache_slabs] * (2 * n_kv * n_top)), new_slabs)


def _out_sample_kernel(x_ref, part_ref, os_ref, gs_ref, wout_ref, o_ref):
    o = (part_ref[...] + gs_ref[...] * os_ref[...]).astype(BF16)
    o_ref[...] = x_ref[...] + _dot(o, wout_ref[...])


def _out_sample(x, part, o_s, g_s, w_out):
    return pl.pallas_call(
        _out_sample_kernel,
        out_shape=jax.ShapeDtypeStruct(x.shape, F32),
        compiler_params=pltpu.CompilerParams(vmem_limit_bytes=VMEM_LIMIT),
        name="out_sample",
    )(x, part, o_s, g_s, w_out)


def _even_odd(a, axis, pad_to=None):
    parts = []
    for start in (0, 1):
        part = lax.slice_in_dim(a, start, a.shape[axis], stride=2, axis=axis)
        if pad_to is not None:
            widths = [(0, 0)] * a.ndim
            widths[axis] = (0, pad_to - part.shape[axis])
            part = jnp.pad(part, widths)
        parts.append(part)
    return jnp.concatenate(parts, axis=axis)


def kernel(x_prompt, x_sample, state_conv, cache_cmp_kv, cache_sel_kv, cache_win_kv, page_table, norm_ffa, w_ffa_gu, w_ffa_down, norm_mix, norm_ffb, w_ffb_gu, w_ffb_down, w_conv_in, w_conv, w_conv_out, w_nsa_in, pe_cmp, w_cmp_k1, w_cmp_k2, w_cmp_v1, w_cmp_v2, w_nsa_out, norm_final):
    bp, tp, _ = x_prompt.shape
    bs, ts, _ = x_sample.shape
    n_p, n_s = bp * tp, bs * ts
    xp = x_prompt.reshape(n_p, D_MODEL)
    xs = x_sample.reshape(n_s, D_MODEL)

    def ffn_pair(xp, xs, g, w_gu, w_down, final=False):
        w_gu, w_down = w_gu.astype(BF16), w_down.astype(BF16)
        return (_ffn(xp, g, w_gu, w_down, norm_final, tm=512, final=final),
                _ffn(xs, g, w_gu, w_down, norm_final, tm=n_s, final=final))

    xp, xs = ffn_pair(xp, xs, norm_ffa[0], w_ffa_gu[0], w_ffa_down[0])
    w_in, w_out = w_conv_in[0].astype(BF16), w_conv_out[0].astype(BF16)
    xp3, conv_p = _conv_prompt(xp.reshape(bp, tp, D_MODEL), norm_mix[0], w_in, w_conv[0], w_out)
    xs3, conv_s = _conv_sample(xs.reshape(bs, ts, D_MODEL), state_conv[0], norm_mix[0], w_in, w_conv[0], w_out)
    xp, xs = ffn_pair(xp3.reshape(n_p, D_MODEL), xs3.reshape(n_s, D_MODEL), norm_ffb[0], w_ffb_gu[0], w_ffb_down[0])

    xp, xs = ffn_pair(xp, xs, norm_ffa[1], w_ffa_gu[1], w_ffa_down[1])
    proj_w = _nsa_weights(w_nsa_in[0])
    cmp_w = _compress_weights(pe_cmp[0], w_cmp_k1[0], w_cmp_k2[0], w_cmp_v1[0], w_cmp_v2[0])
    w_out = w_nsa_out[0].astype(BF16)
    kv_shape = (2, N_KV_HEADS, HEAD_DIM)
    row_w = 2 * KV_W

    q_t, g_t, kvc_t, kvs_t, kvw_t, ks, kw, vs_t, vw_t = _nsa_project_prompt(xp, norm_mix[1], proj_w, seq_len=tp, tm=512)
    n_cmp = tp // L_CMP
    kc, vc_t = _compress_seq(kvc_t, cmp_w)
    kc = _even_odd(kc.reshape(bp, n_cmp, KV_W), 1)
    vc_t = _even_odd(vc_t.reshape(KV_W, bp, n_cmp), 2).transpose(1, 0, 2)
    seq = lambda a: a.reshape(bp, tp, a.shape[-1])
    xp3 = _attn_prompt(seq(xp), q_t, g_t, kc, vc_t, seq(ks), vs_t, seq(kw), vw_t, w_out)
    by_row = lambda a: a.reshape((bp,) + kv_shape + (a.shape[-1],)).transpose(0, 4, 1, 2, 3)[None]
    cmp_p = by_row(kvc_t)
    sel_p = by_row(kvs_t)
    win_p = by_row(kvw_t[:, :, tp - min(WINDOW, tp):])

    q_t, g_t, kvc, kvs, kvw = _nsa_project_sample(xs, norm_mix[1], proj_w)
    n_pages = page_table.shape[1]
    past = n_pages * PAGE_SIZE
    n_new = -(-ts // L_SEL) * L_SEL
    n_past_blk = past // L_SEL
    n_blk = n_past_blk + n_new // L_SEL
    n_top = min(TOP_N, n_blk)
    lanes = lambda a: jnp.pad(a.reshape(a.shape[0], bs, ts).transpose(1, 0, 2), ((0, 0), (0, 0), (0, TQ - ts)))
    by_channel = lambda c: c.transpose(0, 2, 3, 4, 1)
    kc_past, vct_past = _compress_paged(by_channel(cache_cmp_kv[0]).reshape(-1, row_w, PAGE_SIZE), page_table, cmp_w)
    new_c = jnp.pad(kvc.reshape(bs, ts, row_w), ((0, 0), (0, n_new - ts), (0, 0)))
    kc_new, vct_new = _compress(new_c.reshape(bs * n_new // L_CMP, L_CMP * row_w), cmp_w, tm=bs * n_new // L_CMP)
    half = -(-n_blk // LANES) * LANES
    kc_all = _even_odd(jnp.concatenate([kc_past.reshape(bs, -1, KV_W), kc_new.reshape(bs, -1, KV_W)], axis=1), 1, half)
    vct_all = _even_odd(jnp.concatenate([vct_past.reshape(KV_W, bs, -1), vct_new.reshape(KV_W, bs, -1)], axis=2), 2, half)
    wb = cache_win_kv.shape[2]
    new_w = jnp.pad(kvw.reshape(bs, ts, row_w), ((0, 0), (0, NEW_ROWS - ts), (0, 0)))
    part, idx = _attn_sample(lanes(q_t), lanes(g_t), kc_all, vct_all.transpose(1, 0, 2),
                             by_channel(cache_win_kv[0]).reshape(bs, row_w, wb), new_w, past=past, n_blk=n_blk)
    bid = idx[..., :ts].transpose(0, 3, 1, 2)
    ip = jnp.minimum(bid, n_past_blk - 1)
    pid = page_table[jnp.arange(bs)[:, None, None, None], ip // (PAGE_SIZE // L_SEL)]
    q5 = q_t.reshape(N_KV_HEADS, GROUP, HEAD_DIM, bs, ts).transpose(3, 4, 0, 1, 2)
    q5 = jnp.pad(q5, ((0, 0),) * 3 + ((0, Q_ROWS - GROUP), (0, 0)))
    new_s = jnp.pad(kvs.reshape(bs, ts, 2 * N_KV_HEADS, HEAD_DIM).transpose(0, 2, 3, 1),
                    ((0, 0),) * 3 + ((0, PAGE_SIZE - ts),))
    o_s = _sel_sample(pid.reshape(-1).astype(jnp.int32), bid.reshape(-1), q5,
                      by_channel(cache_sel_kv[0]).reshape(-1, HEAD_DIM, PAGE_SIZE), new_s,
                      past=past, n_past_blk=n_past_blk, n_top=n_top)
    o_s = o_s[:, :, :, :GROUP].reshape(n_s, N_Q)
    g_s = jnp.repeat(g_t[N_HEADS:2 * N_HEADS].T, HEAD_DIM, axis=1)
    part = part[:, :, :ts].transpose(0, 2, 1).reshape(n_s, N_Q)
    xs = _out_sample(xs, part, o_s, g_s, w_out)
    cmp_s = kvc.reshape((1, bs, ts) + kv_shape)
    sel_s = kvs.reshape((1, bs, ts) + kv_shape)
    win_s = jnp.concatenate([cache_win_kv[0], kvw.reshape((bs, ts) + kv_shape)], axis=1)[None, :, ts:]

    xp, xs = ffn_pair(xp3.reshape(n_p, D_MODEL), xs, norm_ffb[1], w_ffb_gu[1], w_ffb_down[1], final=True)
    return (xp.reshape(bp, tp, D_MODEL), xs.reshape(bs, ts, D_MODEL),
            conv_p[None], conv_s[None], cmp_p, cmp_s, sel_p, sel_s, win_p, win_s)
```

```python
import functools

import jax
import jax.numpy as jnp
from jax import lax
from jax.experimental import pallas as pl
from jax.experimental.pallas import tpu as pltpu

D_MODEL = 1024
D_FF = 2816
CONV_W = 3
N_HEADS = 16
HEAD_DIM = 64
N_KV_HEADS = 4
GROUP = N_HEADS // N_KV_HEADS
KV_W = N_KV_HEADS * HEAD_DIM
L_CMP = 32
L_SEL = 64
TOP_N = 16
WINDOW = 512
PAGE_SIZE = 128
NORM_EPS = 1e-6
FORCE_BONUS = 1e4
NEG = -1e30

LANES = 128
VMEM_LIMIT = 56 * 1024 * 1024

F32 = jnp.float32
BF16 = jnp.bfloat16


def _params(*sem):
    return pltpu.CompilerParams(dimension_semantics=sem, vmem_limit_bytes=VMEM_LIMIT)


def _rms(x, g):
    return x * lax.rsqrt(jnp.mean(x * x, axis=-1, keepdims=True) + NORM_EPS) * g


def _dot(a, b):
    return jnp.dot(a, b, preferred_element_type=F32)


def _dot_nt(a, b):
    return lax.dot_general(a, b, (((1,), (1,)), ((), ())), preferred_element_type=F32)


def _dot_tn(a, b):
    return lax.dot_general(a, b, (((0,), (0,)), ((), ())), preferred_element_type=F32)


def _cast_kernel(w_ref, o_ref):
    o_ref[...] = w_ref[...].astype(BF16)


def _to_bf16(w, layer, *, steps=4):
    _, r, c = w.shape
    tr = r // steps
    return pl.pallas_call(
        _cast_kernel,
        grid=(steps,),
        in_specs=[pl.BlockSpec((None, tr, c), lambda i: (layer, i, 0))],
        out_specs=pl.BlockSpec((tr, c), lambda i: (i, 0)),
        out_shape=jax.ShapeDtypeStruct((r, c), BF16),
        compiler_params=_params("parallel"),
        name="to_bf16",
    )(w)


def _ffn_kernel(x_ref, g_ref, wg_ref, wu_ref, wd_ref, gf_ref, o_ref, *, final):
    x = x_ref[...]
    xn = _rms(x, g_ref[...]).astype(BF16)
    gate = _dot(xn, wg_ref[...])
    up = _dot(xn, wu_ref[...])
    act = (gate * jax.nn.sigmoid(gate) * up).astype(BF16)
    y = x + 0.5 * _dot(act, wd_ref[...])
    o_ref[...] = _rms(y, gf_ref[...]) if final else y


def _resident(shape, index):
    return pl.BlockSpec(shape, index, pipeline_mode=pl.Buffered(1))


def _ffn(x, g, w_gu, w_down, g_final, *, tm, final=False):
    n = x.shape[0]
    return pl.pallas_call(
        functools.partial(_ffn_kernel, final=final),
        grid=(n // tm,),
        in_specs=[
            pl.BlockSpec((tm, D_MODEL), lambda i: (i, 0)),
            pl.BlockSpec((1, D_MODEL), lambda i: (0, 0)),
            _resident((D_MODEL, D_FF), lambda i: (0, 0)),
            _resident((D_MODEL, D_FF), lambda i: (0, 1)),
            _resident((D_FF, D_MODEL), lambda i: (0, 0)),
            pl.BlockSpec((1, D_MODEL), lambda i: (0, 0)),
        ],
        out_specs=pl.BlockSpec((tm, D_MODEL), lambda i: (i, 0)),
        out_shape=jax.ShapeDtypeStruct((n, D_MODEL), F32),
        compiler_params=_params("parallel"),
        name="ffn_final" if final else "ffn",
    )(x, g.reshape(1, D_MODEL), w_gu, w_gu, w_down, g_final.reshape(1, D_MODEL))


CARRY = 8


def _conv_prompt_kernel(x_ref, g_ref, win_ref, wc_ref, wout_ref, o_ref, st_ref, uext_ref, *, tm):
    @pl.when(pl.program_id(1) == 0)
    def _():
        uext_ref[0:CARRY, :] = jnp.zeros((CARRY, D_MODEL), F32)

    x = x_ref[0]
    h = _rms(x, g_ref[...]).astype(BF16)
    p = _dot(h, win_ref[...])
    bg = p[:, :D_MODEL]
    u = p[:, D_MODEL:2 * D_MODEL] * p[:, 2 * D_MODEL:]
    uext_ref[CARRY:CARRY + tm, :] = u
    wc = wc_ref[...]
    conv = (wc[0:1] * uext_ref[CARRY - 2:CARRY - 2 + tm, :]
            + wc[1:2] * uext_ref[CARRY - 1:CARRY - 1 + tm, :]
            + wc[2:3] * u)
    y = _dot((bg * conv).astype(BF16), wout_ref[...])
    o_ref[0] = x + y
    st_ref[0] = uext_ref[CARRY + tm - 2:CARRY + tm, :]
    uext_ref[0:CARRY, :] = uext_ref[tm:tm + CARRY, :]


def _conv_prompt(x, g, w_in, w_conv, w_out, *, tm=512):
    b, t, _ = x.shape
    return pl.pallas_call(
        functools.partial(_conv_prompt_kernel, tm=tm),
        grid=(b, t // tm),
        in_specs=[
            pl.BlockSpec((1, tm, D_MODEL), lambda bi, ti: (bi, ti, 0)),
            pl.BlockSpec((1, D_MODEL), lambda bi, ti: (0, 0)),
            pl.BlockSpec((D_MODEL, 3 * D_MODEL), lambda bi, ti: (0, 0)),
            pl.BlockSpec((CONV_W, D_MODEL), lambda bi, ti: (0, 0)),
            pl.BlockSpec((D_MODEL, D_MODEL), lambda bi, ti: (0, 0)),
        ],
        out_specs=[
            pl.BlockSpec((1, tm, D_MODEL), lambda bi, ti: (bi, ti, 0)),
            pl.BlockSpec((1, CONV_W - 1, D_MODEL), lambda bi, ti: (bi, 0, 0)),
        ],
        out_shape=[jax.ShapeDtypeStruct((b, t, D_MODEL), F32),
                   jax.ShapeDtypeStruct((b, CONV_W - 1, D_MODEL), F32)],
        scratch_shapes=[pltpu.VMEM((CARRY + tm, D_MODEL), F32)],
        compiler_params=_params("parallel", "arbitrary"),
        name="conv_prompt",
    )(x, g.reshape(1, D_MODEL), w_in, w_conv, w_out)


def _conv_sample_kernel(x_ref, g_ref, win_ref, wc_ref, wout_ref, s1_ref, s2_ref, o_ref, u_ref, *, t_len):
    x = x_ref[...]
    h = _rms(x, g_ref[...]).astype(BF16)
    p = _dot(h, win_ref[...])
    bg = p[:, :D_MODEL]
    u = p[:, D_MODEL:2 * D_MODEL] * p[:, 2 * D_MODEL:]
    pos = lax.broadcasted_iota(jnp.int32, u.shape, 0) % t_len
    u1 = jnp.where(pos >= 1, pltpu.roll(u, 1, axis=0), s1_ref[...])
    u2 = jnp.where(pos >= 2, pltpu.roll(u, 2, axis=0), s2_ref[...])
    wc = wc_ref[...]
    conv = wc[0:1] * u2 + wc[1:2] * u1 + wc[2:3] * u
    o_ref[...] = x + _dot((bg * conv).astype(BF16), wout_ref[...])
    u_ref[...] = u


def _conv_sample(x, state, g, w_in, w_conv, w_out):
    b, t, _ = x.shape
    n = b * t
    zeros = jnp.zeros((b, t, D_MODEL), F32)
    s1 = zeros.at[:, 0].set(state[:, 1]).reshape(n, D_MODEL)
    s2 = zeros.at[:, 0].set(state[:, 0]).at[:, 1].set(state[:, 1]).reshape(n, D_MODEL)
    y, u = pl.pallas_call(
        functools.partial(_conv_sample_kernel, t_len=t),
        out_shape=[jax.ShapeDtypeStruct((n, D_MODEL), F32), jax.ShapeDtypeStruct((n, D_MODEL), F32)],
        compiler_params=pltpu.CompilerParams(vmem_limit_bytes=VMEM_LIMIT),
        name="conv_sample",
    )(x.reshape(n, D_MODEL), g.reshape(1, D_MODEL), w_in, w_conv, w_out, s1, s2)
    return y.reshape(b, t, D_MODEL), u.reshape(b, t, D_MODEL)[:, t - (CONV_W - 1):]


N_Q = N_HEADS * HEAD_DIM


def _proj_common(x_ref, g_ref, wqt_ref, wgt_ref, qt_ref, gt_ref):
    h = _rms(x_ref[...], g_ref[...]).astype(BF16)
    qt_ref[...] = (_dot_nt(wqt_ref[...], h) * (HEAD_DIM ** -0.5)).astype(BF16)
    gt_ref[...] = jax.nn.sigmoid(_dot_nt(wgt_ref[...], h))
    return h


def _proj_prompt_kernel(x_ref, g_ref, wqt_ref, wgt_ref, wkvt_ref, wk_ref,
                        qt_ref, gt_ref, kvct_ref, kvst_ref, kvwt_ref, ks_ref, kw_ref, vst_ref, vwt_ref):
    h = _proj_common(x_ref, g_ref, wqt_ref, wgt_ref, qt_ref, gt_ref)
    kv_t = _dot_nt(wkvt_ref[...], h)
    kvct_ref[0] = kv_t[:2 * KV_W]
    kvst_ref[0] = kv_t[2 * KV_W:4 * KV_W]
    kvwt_ref[0] = kv_t[4 * KV_W:]
    vst_ref[...] = kv_t[3 * KV_W:4 * KV_W].astype(BF16)
    vwt_ref[...] = kv_t[5 * KV_W:].astype(BF16)
    k = _dot(h, wk_ref[...]).astype(BF16)
    ks_ref[...] = k[:, :KV_W]
    kw_ref[...] = k[:, KV_W:]


def _proj_sample_kernel(x_ref, g_ref, wqt_ref, wgt_ref, wkv_ref, qt_ref, gt_ref, kvc_ref, kvs_ref, kvw_ref):
    h = _proj_common(x_ref, g_ref, wqt_ref, wgt_ref, qt_ref, gt_ref)
    kv = _dot(h, wkv_ref[...])
    kvc_ref[...] = kv[:, :2 * KV_W]
    kvs_ref[...] = kv[:, 2 * KV_W:4 * KV_W]
    kvw_ref[...] = kv[:, 4 * KV_W:]


def _nsa_weights(w_in):
    wkv = w_in[:, N_Q:N_Q + 6 * KV_W]
    wk = jnp.concatenate([wkv[:, 2 * KV_W:3 * KV_W], wkv[:, 4 * KV_W:5 * KV_W]], axis=1)
    wgt = jnp.pad(w_in[:, N_Q + 6 * KV_W:], ((0, 0), (0, LANES - 3 * N_HEADS))).T
    return dict(wqt=w_in[:, :N_Q].T.astype(BF16), wgt=wgt.astype(BF16), wkv=wkv.astype(BF16),
                wkvt=wkv.T.astype(BF16), wk=wk.astype(BF16))


def _nsa_project_prompt(x, g, w, *, seq_len, tm):
    n = x.shape[0]
    per_seq = seq_len // tm
    row = lambda w_: pl.BlockSpec((tm, w_), lambda i: (i, 0))
    col = lambda h: pl.BlockSpec((h, tm), lambda i: (0, i))
    seq_t = pl.BlockSpec((1, 2 * KV_W, tm), lambda i: (i // per_seq, 0, i % per_seq))
    full = lambda a: pl.BlockSpec(a.shape, lambda i: (0, 0))
    weights = [w["wqt"], w["wgt"], w["wkvt"], w["wk"]]
    return pl.pallas_call(
        _proj_prompt_kernel,
        grid=(n // tm,),
        in_specs=[row(D_MODEL), pl.BlockSpec((1, D_MODEL), lambda i: (0, 0))] + [full(a) for a in weights],
        out_specs=[col(N_Q), col(LANES), seq_t, seq_t, seq_t, row(KV_W), row(KV_W), col(KV_W), col(KV_W)],
        out_shape=[jax.ShapeDtypeStruct((N_Q, n), BF16), jax.ShapeDtypeStruct((LANES, n), F32)]
        + [jax.ShapeDtypeStruct((n // seq_len, 2 * KV_W, seq_len), F32)] * 3
        + [jax.ShapeDtypeStruct((n, KV_W), BF16)] * 2
        + [jax.ShapeDtypeStruct((KV_W, n), BF16)] * 2,
        compiler_params=_params("parallel"),
        name="nsa_project_prompt",
    )(x, g.reshape(1, D_MODEL), *weights)


def _nsa_project_sample(x, g, w):
    n = x.shape[0]
    return pl.pallas_call(
        _proj_sample_kernel,
        out_shape=[jax.ShapeDtypeStruct((N_Q, n), BF16), jax.ShapeDtypeStruct((LANES, n), F32)]
        + [jax.ShapeDtypeStruct((n, 2 * KV_W), F32)] * 3,
        compiler_params=pltpu.CompilerParams(vmem_limit_bytes=VMEM_LIMIT),
        name="nsa_project_sample",
    )(x, g.reshape(1, D_MODEL), w["wqt"], w["wgt"], w["wkv"])


def _compress_out(acck, accv, w2k_ref, w2vt_ref, kc_ref, vct_ref):
    kc_ref[...] = _dot(jax.nn.gelu(acck).astype(BF16), w2k_ref[...]).astype(BF16)
    vct_ref[...] = _dot_nt(w2vt_ref[...], jax.nn.gelu(accv).astype(BF16)).astype(BF16)


def _compress_kernel(x_ref, pe_ref, w1k_ref, w1v_ref, w2k_ref, w2vt_ref, kc_ref, vct_ref, acck_ref, accv_ref):
    l = pl.program_id(1)

    @pl.when(l == 0)
    def _():
        acck_ref[...] = jnp.zeros_like(acck_ref)
        accv_ref[...] = jnp.zeros_like(accv_ref)

    xb = x_ref[...] + pe_ref[...]
    acck_ref[...] += _dot(xb[:, :KV_W].astype(BF16), w1k_ref[...])
    accv_ref[...] += _dot(xb[:, KV_W:].astype(BF16), w1v_ref[...])

    @pl.when(l == L_CMP - 1)
    def _():
        _compress_out(acck_ref[...], accv_ref[...], w2k_ref, w2vt_ref, kc_ref, vct_ref)


PAGES_PER_STEP = 32
BLOCKS_PER_PAGE = PAGE_SIZE // L_CMP


def _compress_weights(pe, w1k, w2k, w1v, w2v):
    eye = jnp.eye(N_KV_HEADS, dtype=F32)
    bd1 = lambda w1: jax.vmap(lambda w: jnp.kron(eye, w))(w1).astype(BF16)
    pe_row = jnp.tile(pe, (1, 2 * N_KV_HEADS))
    r = jnp.arange(2 * PAGE_SIZE)
    src = (r % 8 // BLOCKS_PER_PAGE) * PAGE_SIZE + (r % BLOCKS_PER_PAGE) * L_CMP + r // 8
    perm = (src[:, None] == r[None, :]).astype(BF16)
    return dict(pe_row=pe_row.reshape(L_CMP, 1, 2 * KV_W), pe_page_t=jnp.tile(pe_row, (BLOCKS_PER_PAGE, 1)).T, perm=perm,
                w1k=bd1(w1k), w1v=bd1(w1v), w2k=jnp.kron(eye, w2k).astype(BF16), w2vt=jnp.kron(eye, w2v).T.astype(BF16))


def _compress(x, w, *, tm):
    m = x.shape[0]
    tm = min(tm, m)
    return pl.pallas_call(
        _compress_kernel,
        grid=(m // tm, L_CMP),
        in_specs=[
            pl.BlockSpec((tm, 2 * KV_W), lambda i, l: (i, l)),
            pl.BlockSpec((None, 1, 2 * KV_W), lambda i, l: (l, 0, 0)),
            pl.BlockSpec((None, KV_W, KV_W), lambda i, l: (l, 0, 0)),
            pl.BlockSpec((None, KV_W, KV_W), lambda i, l: (l, 0, 0)),
            pl.BlockSpec((KV_W, KV_W), lambda i, l: (0, 0)),
            pl.BlockSpec((KV_W, KV_W), lambda i, l: (0, 0)),
        ],
        out_specs=[pl.BlockSpec((tm, KV_W), lambda i, l: (i, 0)), pl.BlockSpec((KV_W, tm), lambda i, l: (0, i))],
        out_shape=[jax.ShapeDtypeStruct((m, KV_W), BF16), jax.ShapeDtypeStruct((KV_W, m), BF16)],
        scratch_shapes=[pltpu.VMEM((tm, KV_W), F32)] * 2,
        compiler_params=_params("parallel", "arbitrary"),
        name="compress",
    )(x, w["pe_row"], w["w1k"], w["w1v"], w["w2k"], w["w2vt"])


def _compress_pages_body(refs, n_pages):
    pages = refs[:n_pages]
    pe_ref, perm_ref, w1k_ref, w1v_ref, w2k_ref, w2vt_ref, kc_ref, vct_ref = refs[n_pages:]
    pe = pe_ref[...]
    perm = perm_ref[...]
    by_l = []
    for p in range(0, n_pages, 2):
        pair_t = jnp.concatenate([(pages[p][...] + pe).astype(BF16), (pages[p + 1][...] + pe).astype(BF16)], axis=1)
        by_l.append(_dot_nt(perm, pair_t))
    acck = jnp.zeros((n_pages * BLOCKS_PER_PAGE, KV_W), F32)
    accv = jnp.zeros((n_pages * BLOCKS_PER_PAGE, KV_W), F32)
    for l in range(L_CMP):
        xl = jnp.concatenate([y[l * 8:(l + 1) * 8] for y in by_l], axis=0).astype(BF16)
        acck = acck + _dot(xl[:, :KV_W], w1k_ref[l])
        accv = accv + _dot(xl[:, KV_W:], w1v_ref[l])
    _compress_out(acck, accv, w2k_ref, w2vt_ref, kc_ref, vct_ref)


def _compress_paged_kernel(pt_ref, *refs, n_pages):
    del pt_ref
    _compress_pages_body(refs, n_pages)


def _compress_seq_kernel(*refs, n_pages):
    _compress_pages_body(refs, n_pages)


def _compress_consts(w):
    return [w["pe_page_t"], w["perm"], w["w1k"], w["w1v"], w["w2k"], w["w2vt"]]


def _compress_paged(cache, page_table, w):
    b, n_pages = page_table.shape
    pps = min(PAGES_PER_STEP, n_pages)
    steps = n_pages // pps
    m_step = pps * BLOCKS_PER_PAGE
    m = b * n_pages * BLOCKS_PER_PAGE
    page = lambda j: pl.BlockSpec((None, 2 * KV_W, PAGE_SIZE), lambda bi, s, pt: (pt[bi, s * pps + j], 0, 0))
    full = lambda a: pl.BlockSpec(a.shape, lambda bi, s, pt: (0,) * a.ndim)
    consts = _compress_consts(w)
    return pl.pallas_call(
        functools.partial(_compress_paged_kernel, n_pages=pps),
        grid_spec=pltpu.PrefetchScalarGridSpec(
            num_scalar_prefetch=1,
            grid=(b, steps),
            in_specs=[page(j) for j in range(pps)] + [full(a) for a in consts],
            out_specs=[pl.BlockSpec((m_step, KV_W), lambda bi, s, pt: (bi * steps + s, 0)),
                       pl.BlockSpec((KV_W, m_step), lambda bi, s, pt: (0, bi * steps + s))],
        ),
        out_shape=[jax.ShapeDtypeStruct((m, KV_W), BF16), jax.ShapeDtypeStruct((KV_W, m), BF16)],
        compiler_params=_params("parallel", "arbitrary"),
        name="compress_paged",
    )(page_table, *([cache] * pps), *consts)


def _compress_seq(rows_t, w):
    b, _, t = rows_t.shape
    per_seq = t // PAGE_SIZE
    n_pages = b * per_seq
    pps = min(PAGES_PER_STEP, n_pages)
    m_step = pps * BLOCKS_PER_PAGE
    m = n_pages * BLOCKS_PER_PAGE
    page = lambda j: pl.BlockSpec((None, 2 * KV_W, PAGE_SIZE),
                                  lambda s: ((s * pps + j) // per_seq, 0, (s * pps + j) % per_seq))
    full = lambda a: pl.BlockSpec(a.shape, lambda s: (0,) * a.ndim)
    consts = _compress_consts(w)
    return pl.pallas_call(
        functools.partial(_compress_seq_kernel, n_pages=pps),
        grid=(n_pages // pps,),
        in_specs=[page(j) for j in range(pps)] + [full(a) for a in consts],
        out_specs=[pl.BlockSpec((m_step, KV_W), lambda s: (s, 0)), pl.BlockSpec((KV_W, m_step), lambda s: (0, s))],
        out_shape=[jax.ShapeDtypeStruct((m, KV_W), BF16), jax.ShapeDtypeStruct((KV_W, m), BF16)],
        compiler_params=_params("parallel"),
        name="compress_seq",
    )(*([rows_t] * pps), *consts)


TQ_PROMPT = 256
TQ_SAMPLE = 128
KCH = 512
ONES_ROWS = 16


def _head_queries(q_t, kv):
    zero = jnp.zeros((HEAD_DIM, q_t.shape[1]), BF16)
    cols = []
    for g in range(GROUP):
        h = kv * GROUP + g
        qg = q_t[h * HEAD_DIM:(h + 1) * HEAD_DIM, :]
        cols.append(jnp.concatenate([qg, zero] if kv % 2 == 0 else [zero, qg], axis=0))
    return jnp.concatenate(cols, axis=1)


def _slab(kv):
    return slice((kv // 2) * LANES, (kv // 2 + 1) * LANES)


def _head(kv):
    return slice(kv * HEAD_DIM, (kv + 1) * HEAD_DIM)


def _cmp_branch(kc, vc_t, x, keep):
    s = jnp.where(keep, _dot(kc, x), NEG)
    e = jnp.exp(s - jnp.max(s, axis=0, keepdims=True))
    l = jnp.sum(e, axis=0, keepdims=True)
    e = jnp.where(keep, e, 0.0)
    return _dot(vc_t, e.astype(BF16)) / l, e / l


def _group_sum(p):
    tq = p.shape[1] // GROUP
    out = p[:, 0:tq]
    for g in range(1, GROUP):
        out = out + p[:, g * tq:(g + 1) * tq]
    return out


def _with_ones(v_t):
    return jnp.concatenate([v_t, jnp.ones((ONES_ROWS, v_t.shape[1]), BF16)], axis=0)


def _topn_rows(score, n):
    nb = score.shape[0]
    row = lax.broadcasted_iota(jnp.int32, score.shape, 0).astype(F32)
    picked = jnp.zeros(score.shape, F32)
    ids = []
    for _ in range(n):
        top = jnp.max(score, axis=0, keepdims=True)
        first = jnp.min(jnp.where(score == top, row, float(nb)), axis=0, keepdims=True)
        hit = row == first
        picked = jnp.where(hit, 1.0, picked)
        score = jnp.where(hit, -jnp.inf, score)
        ids.append(first)
    return picked, ids


def _masked_softmax_pv(s, keep, v_t):
    tq = keep.shape[1]
    s = jnp.concatenate([jnp.where(keep, s[:, g * tq:(g + 1) * tq], NEG) for g in range(GROUP)], axis=1)
    e = jnp.exp(s - jnp.max(s, axis=0, keepdims=True))
    o = _dot(_with_ones(v_t), e.astype(BF16))
    return o[:HEAD_DIM] / o[HEAD_DIM:HEAD_DIM + 1]


def _gate_mix(g_t, kv, branches):
    tq = g_t.shape[1]
    out = []
    for g in range(GROUP):
        h = kv * GROUP + g
        c = slice(g * tq, (g + 1) * tq)
        acc = None
        for br, o in branches:
            term = g_t[br * N_HEADS + h:br * N_HEADS + h + 1, :] * o[:, c]
            acc = term if acc is None else acc + term
        out.append(acc)
    return out


def _attn_prompt_kernel(x_ref, qt_ref, gt_ref, kc_ref, vct_ref, ks_ref, vst_ref, kw_ref, *rest, t_len, tq):
    win_blocks = WINDOW // tq + 1
    cols = GROUP * tq
    vwt_refs = rest[:win_blocks]
    wout_ref, o_ref, mask_ref, m_ref, acc_ref = rest[win_blocks:]
    t0 = pl.program_id(1) * tq
    n_sel = t_len // L_SEL
    n_cmp = t_len // L_CMP
    t_tok = t0 + lax.broadcasted_iota(jnp.int32, (1, tq), 1)
    t_col = t0 + lax.broadcasted_iota(jnp.int32, (1, cols), 1) % tq
    t_heads = t0 + lax.broadcasted_iota(jnp.int32, (1, N_KV_HEADS * tq), 1) % tq
    r_c = lax.broadcasted_iota(jnp.int32, (n_cmp, 1), 0)
    keep_c = (2 * (r_c % n_sel) + r_c // n_sel + 1) * L_CMP - 1 <= t_col
    blk = lax.broadcasted_iota(jnp.int32, (n_sel, N_KV_HEADS * tq), 0)
    cur = t_heads // L_SEL
    forced = (blk == 0) | (blk == cur) | (blk == cur - 1)
    valid = blk <= cur
    key_in_blk = lax.broadcasted_iota(jnp.int32, (L_SEL, 1), 0)
    win_pos = t0 - WINDOW + lax.broadcasted_iota(jnp.int32, (win_blocks * tq, 1), 0)
    dist = t_tok - win_pos
    keep_w = (dist >= 0) & (dist < WINDOW) & (win_pos >= 0)
    n_chunks = t_len // KCH
    last_chunk = (t0 + tq - 1) // KCH
    q_t = qt_ref[...]
    g_t = gt_ref[...]

    o_cmp, imp = [], []
    for kv in range(N_KV_HEADS):
        o_c, p = _cmp_branch(kc_ref[0, :, _slab(kv)], vct_ref[0, _head(kv), :], _head_queries(q_t, kv), keep_c)
        o_cmp.append(o_c)
        p_g = _group_sum(p)
        imp.append(p_g[0:n_sel] + p_g[n_sel:n_cmp])
    score = jnp.where(forced, FORCE_BONUS, jnp.where(valid, jnp.concatenate(imp, axis=1), -FORCE_BONUS))
    picked, _ = _topn_rows(score, min(TOP_N, n_sel))
    sel = jnp.where(valid, picked, 0.0)
    for kv in range(N_KV_HEADS):
        for j in range(n_sel):
            causal = j * L_SEL + key_in_blk <= t_tok
            mask_ref[kv, j * L_SEL:(j + 1) * L_SEL, :] = jnp.where(
                causal, jnp.broadcast_to(sel[j:j + 1, kv * tq:(kv + 1) * tq], (L_SEL, tq)), 0.0)

    m_ref[...] = jnp.full(m_ref.shape, NEG, F32)
    acc_ref[...] = jnp.zeros_like(acc_ref)
    for c in range(n_chunks):
        @pl.when(c <= last_chunk)
        def _():
            keys = slice(c * KCH, (c + 1) * KCH)
            for kv in range(N_KV_HEADS):
                s = _dot(ks_ref[0, keys, _slab(kv)], _head_queries(q_t, kv))
                keep = mask_ref[kv, keys, :] > 0.5
                s = jnp.concatenate([jnp.where(keep, s[:, g * tq:(g + 1) * tq], NEG) for g in range(GROUP)], axis=1)
                m_old = m_ref[kv]
                m_new = jnp.maximum(m_old, jnp.max(s, axis=0, keepdims=True))
                e = jnp.exp(s - m_new).astype(BF16)
                acc_ref[kv] = jnp.exp(m_old - m_new) * acc_ref[kv] + _dot(_with_ones(vst_ref[_head(kv), keys]), e)
                m_ref[kv] = m_new

    heads = []
    for kv in range(N_KV_HEADS):
        x = _head_queries(q_t, kv)
        s_w = []
        for j in range(win_blocks):
            start = pl.multiple_of(jnp.maximum(t0 - WINDOW + j * tq, 0), tq)
            s_w.append(_dot(kw_ref[0, pl.ds(start, tq), _slab(kv)], x))
        v_w = jnp.concatenate([r[_head(kv), :] for r in vwt_refs], axis=1)
        o_w = _masked_softmax_pv(jnp.concatenate(s_w, axis=0), keep_w, v_w)
        acc = acc_ref[kv]
        o_s = acc[:HEAD_DIM] / acc[HEAD_DIM:HEAD_DIM + 1]
        heads += _gate_mix(g_t, kv, ((0, o_cmp[kv]), (1, o_s), (2, o_w)))
    o = jnp.concatenate(heads, axis=0).T.astype(BF16)
    o_ref[0] = x_ref[0] + _dot(o, wout_ref[...])


def _attn_prompt(x, q_t, g_t, kc, vc_t, ks, vs_t, kw, vw_t, w_out, *, tq=TQ_PROMPT):
    b, t, _ = x.shape
    nt = t // tq
    n_cmp = t // L_CMP
    win_blocks = WINDOW // tq + 1
    cols = GROUP * tq
    tile = lambda w: pl.BlockSpec((1, tq, w), lambda bi, i: (bi, i, 0))
    tile_t = lambda h: pl.BlockSpec((h, tq), lambda bi, i: (0, bi * nt + i))
    win_t = lambda j: pl.BlockSpec((KV_W, tq), lambda bi, i: (0, bi * nt + jnp.maximum(i - (win_blocks - 1) + j, 0)))
    return pl.pallas_call(
        functools.partial(_attn_prompt_kernel, t_len=t, tq=tq),
        grid=(b, nt),
        in_specs=[tile(D_MODEL), tile_t(N_Q), tile_t(LANES),
                  pl.BlockSpec((1, n_cmp, KV_W), lambda bi, i: (bi, 0, 0)),
                  pl.BlockSpec((1, KV_W, n_cmp), lambda bi, i: (bi, 0, 0)),
                  pl.BlockSpec((1, t, KV_W), lambda bi, i: (bi, 0, 0)),
                  pl.BlockSpec((KV_W, t), lambda bi, i: (0, bi)),
                  pl.BlockSpec((1, t, KV_W), lambda bi, i: (bi, 0, 0))]
        + [win_t(j) for j in range(win_blocks)]
        + [pl.BlockSpec((N_Q, D_MODEL), lambda bi, i: (0, 0))],
        out_specs=tile(D_MODEL),
        out_shape=jax.ShapeDtypeStruct((b, t, D_MODEL), F32),
        scratch_shapes=[pltpu.VMEM((N_KV_HEADS, t, tq), F32), pltpu.VMEM((N_KV_HEADS, 1, cols), F32),
                        pltpu.VMEM((N_KV_HEADS, HEAD_DIM + ONES_ROWS, cols), F32)],
        compiler_params=_params("parallel", "arbitrary"),
        name="attn_prompt",
    )(x, q_t, g_t, kc, vc_t, ks, vs_t, kw, *([vw_t] * win_blocks), w_out)


NEW_ROWS = 16


def _attn_sample_kernel(qt_ref, gt_ref, kc_ref, vct_ref, win_ref, new_ref, part_ref, idx_ref, *, past, n_blk, half):
    tq = TQ_SAMPLE
    t_tok = past + lax.broadcasted_iota(jnp.int32, (1, tq), 1)
    t_col = past + lax.broadcasted_iota(jnp.int32, (1, GROUP * tq), 1) % tq
    t_heads = past + lax.broadcasted_iota(jnp.int32, (1, N_KV_HEADS * tq), 1) % tq
    r_c = lax.broadcasted_iota(jnp.int32, (2 * half, 1), 0)
    keep_c = (r_c % half < n_blk) & ((2 * (r_c % half) + r_c // half + 1) * L_CMP - 1 <= t_col)
    blk = lax.broadcasted_iota(jnp.int32, (half, N_KV_HEADS * tq), 0)
    cur = t_heads // L_SEL
    forced = (blk == 0) | (blk == cur) | (blk == cur - 1)
    valid = blk <= cur
    wb = win_ref.shape[2]
    win_pos = past - wb + lax.broadcasted_iota(jnp.int32, (wb + NEW_ROWS, 1), 0)
    dist = t_tok - win_pos
    keep_w = (dist >= 0) & (dist < WINDOW) & (win_pos >= 0)
    q_t = qt_ref[0]
    g_t = gt_ref[0]
    o_cmp, imp = [], []
    for kv in range(N_KV_HEADS):
        o_c, p = _cmp_branch(kc_ref[0, :, _slab(kv)], vct_ref[0, _head(kv), :], _head_queries(q_t, kv), keep_c)
        o_cmp.append(o_c)
        p_g = _group_sum(p)
        imp.append(p_g[0:half] + p_g[half:2 * half])
    score = jnp.where(forced, FORCE_BONUS, jnp.where(valid, jnp.concatenate(imp, axis=1), -FORCE_BONUS))
    score = jnp.where(blk < n_blk, score, -jnp.inf)
    _, ids = _topn_rows(score, min(TOP_N, n_blk))
    for j, first in enumerate(ids):
        for kv in range(N_KV_HEADS):
            idx_ref[0, kv, j:j + 1, :] = first[:, kv * tq:(kv + 1) * tq].astype(jnp.int32)

    heads = []
    for kv in range(N_KV_HEADS):
        x = _head_queries(q_t, kv)
        mine = slice((kv % 2) * HEAD_DIM, (kv % 2 + 1) * HEAD_DIM)
        v_slab = slice(KV_W + (kv // 2) * LANES, KV_W + (kv // 2 + 1) * LANES)
        s = jnp.concatenate([_dot_tn(win_ref[0, _head(kv), :], x[mine].astype(F32)),
                             _dot(new_ref[0, :, _slab(kv)].astype(BF16), x)], axis=0)
        s = jnp.concatenate([jnp.where(keep_w, s[:, g * tq:(g + 1) * tq], NEG) for g in range(GROUP)], axis=1)
        e = jnp.exp(s - jnp.max(s, axis=0, keepdims=True))
        p_w = e / jnp.sum(e, axis=0, keepdims=True)
        v_t = win_ref[0, KV_W + kv * HEAD_DIM:KV_W + (kv + 1) * HEAD_DIM, :]
        o_w = _dot(v_t.astype(BF16), p_w[:wb].astype(BF16)) + _dot_tn(new_ref[0, :, v_slab], p_w[wb:])[mine]
        heads += _gate_mix(g_t, kv, ((0, o_cmp[kv]), (2, o_w)))
    part_ref[0] = jnp.concatenate(heads, axis=0)


def _attn_sample(q_t, g_t, kc, vc_t, win, new, *, past, n_blk):
    b = q_t.shape[0]
    half = kc.shape[1] // 2
    n_top = min(TOP_N, n_blk)
    blk3 = lambda a: pl.BlockSpec((1,) + a.shape[1:], lambda bi: (bi, 0, 0))
    return pl.pallas_call(
        functools.partial(_attn_sample_kernel, past=past, n_blk=n_blk, half=half),
        grid=(b,),
        in_specs=[blk3(q_t), blk3(g_t), blk3(kc), blk3(vc_t), blk3(win), blk3(new)],
        out_specs=[pl.BlockSpec((1, N_Q, TQ_SAMPLE), lambda bi: (bi, 0, 0)),
                   pl.BlockSpec((1, N_KV_HEADS, n_top, TQ_SAMPLE), lambda bi: (bi, 0, 0, 0))],
        out_shape=[jax.ShapeDtypeStruct((b, N_Q, TQ_SAMPLE), F32),
                   jax.ShapeDtypeStruct((b, N_KV_HEADS, n_top, TQ_SAMPLE), jnp.int32)],
        compiler_params=_params("parallel"),
        name="attn_sample",
    )(q_t, g_t, kc, vc_t, win, new)


Q_ROWS = 16


def _sel_sample_kernel(pt_ref, bid_ref, q_ref, *refs, past, n_past_blk, n_top):
    del pt_ref
    n_slab = N_KV_HEADS * n_top
    slab_refs, (new_ref, o_ref) = refs[:n_slab], refs[n_slab:]
    bi, ti = pl.program_id(0), pl.program_id(1)
    base = (bi * pl.num_programs(1) + ti) * N_KV_HEADS * n_top
    t_pos = past + ti
    lane = lax.broadcasted_iota(jnp.int32, (1, PAGE_SIZE), 1)
    for kv in range(N_KV_HEADS):
        k_t, v_t, keep = [], [], []
        for j in range(n_top):
            bid = bid_ref[base + kv * n_top + j]
            from_past = bid < n_past_blk
            first_lane = jnp.where(from_past, (bid % (PAGE_SIZE // L_SEL)) * L_SEL, 0)
            slab = slab_refs[kv * n_top + j]
            k_t.append(jnp.where(from_past, slab[0], new_ref[0, kv]).astype(BF16))
            v_t.append(jnp.where(from_past, slab[1], new_ref[1, kv]).astype(BF16))
            in_blk = lane - first_lane
            keep.append((in_blk >= 0) & (in_blk < L_SEL) & (bid * L_SEL + in_blk <= t_pos))
        s = jnp.where(jnp.concatenate(keep, axis=1), _dot(q_ref[kv], jnp.concatenate(k_t, axis=1)), NEG)
        e = jnp.exp(s - jnp.max(s, axis=1, keepdims=True))
        p = e / jnp.sum(e, axis=1, keepdims=True)
        o_ref[kv] = _dot_nt(p.astype(BF16), jnp.concatenate(v_t, axis=1))


def _sel_sample(page_table, blk_ids, q, cache, new_slabs, *, past, n_past_blk, n_top):
    b, t, n_kv = q.shape[:3]
    blocks_per_page = PAGE_SIZE // L_SEL

    def slab(kv, j):
        def index(bi, ti, pt, bid):
            past_blk = jnp.minimum(bid[((bi * t + ti) * n_kv + kv) * n_top + j], n_past_blk - 1)
            return pt[bi, past_blk // blocks_per_page], 0, kv, 0, 0
        return pl.BlockSpec((None, 2, None, HEAD_DIM, PAGE_SIZE), index)

    return pl.pallas_call(
        functools.partial(_sel_sample_kernel, past=past, n_past_blk=n_past_blk, n_top=n_top),
        grid_spec=pltpu.PrefetchScalarGridSpec(
            num_scalar_prefetch=2,
            grid=(b, t),
            in_specs=[pl.BlockSpec((None, None, n_kv, Q_ROWS, HEAD_DIM), lambda bi, ti, pt, bid: (bi, ti, 0, 0, 0))]
            + [slab(kv, j) for kv in range(n_kv) for j in range(n_top)]
            + [pl.BlockSpec((None, 2, n_kv, HEAD_DIM, PAGE_SIZE), lambda bi, ti, pt, bid: (bi, 0, 0, 0, 0))],
            out_specs=pl.BlockSpec((None, None, n_kv, Q_ROWS, HEAD_DIM), lambda bi, ti, pt, bid: (bi, ti, 0, 0, 0)),
        ),
        out_shape=jax.ShapeDtypeStruct((b, t, n_kv, Q_ROWS, HEAD_DIM), F32),
        compiler_params=_params("parallel", "arbitrary"),
        name="sel_sample",
    )(page_table, blk_ids, q, *([cache] * (n_kv * n_top)), new_slabs)


def _out_sample_kernel(x_ref, part_ref, os_ref, gs_ref, wout_ref, o_ref):
    o = (part_ref[...] + gs_ref[...] * os_ref[...]).astype(BF16)
    o_ref[...] = x_ref[...] + _dot(o, wout_ref[...])


def _out_sample(x, part, o_s, g_s, w_out):
    return pl.pallas_call(
        _out_sample_kernel,
        out_shape=jax.ShapeDtypeStruct(x.shape, F32),
        compiler_params=pltpu.CompilerParams(vmem_limit_bytes=VMEM_LIMIT),
        name="out_sample",
    )(x, part, o_s, g_s, w_out)


def _even_odd(a, axis, pad_to=None):
    parts = []
    for start in (0, 1):
        part = lax.slice_in_dim(a, start, a.shape[axis], stride=2, axis=axis)
        if pad_to is not None:
            widths = [(0, 0)] * a.ndim
            widths[axis] = (0, pad_to - part.shape[axis])
            part = jnp.pad(part, widths)
        parts.append(part)
    return jnp.concatenate(parts, axis=axis)


def kernel(x_prompt, x_sample, state_conv, cache_cmp_kv, cache_sel_kv, cache_win_kv, page_table, norm_ffa, w_ffa_gu, w_ffa_down, norm_mix, norm_ffb, w_ffb_gu, w_ffb_down, w_conv_in, w_conv, w_conv_out, w_nsa_in, pe_cmp, w_cmp_k1, w_cmp_k2, w_cmp_v1, w_cmp_v2, w_nsa_out, norm_final):
    bp, tp, _ = x_prompt.shape
    bs, ts, _ = x_sample.shape
    n_p, n_s = bp * tp, bs * ts
    xp = x_prompt.reshape(n_p, D_MODEL)
    xs = x_sample.reshape(n_s, D_MODEL)

    def ffn_pair(xp, xs, g, w_gu, w_down, layer, final=False):
        w_gu, w_down = _to_bf16(w_gu, layer), _to_bf16(w_down, layer)
        return (_ffn(xp, g, w_gu, w_down, norm_final, tm=512, final=final),
                _ffn(xs, g, w_gu, w_down, norm_final, tm=n_s, final=final))

    xp, xs = ffn_pair(xp, xs, norm_ffa[0], w_ffa_gu, w_ffa_down, 0)
    w_in, w_out = _to_bf16(w_conv_in, 0), _to_bf16(w_conv_out, 0)
    xp3, conv_p = _conv_prompt(xp.reshape(bp, tp, D_MODEL), norm_mix[0], w_in, w_conv[0], w_out)
    xs3, conv_s = _conv_sample(xs.reshape(bs, ts, D_MODEL), state_conv[0], norm_mix[0], w_in, w_conv[0], w_out)
    xp, xs = ffn_pair(xp3.reshape(n_p, D_MODEL), xs3.reshape(n_s, D_MODEL), norm_ffb[0], w_ffb_gu, w_ffb_down, 0)

    xp, xs = ffn_pair(xp, xs, norm_ffa[1], w_ffa_gu, w_ffa_down, 1)
    proj_w = _nsa_weights(w_nsa_in[0])
    cmp_w = _compress_weights(pe_cmp[0], w_cmp_k1[0], w_cmp_k2[0], w_cmp_v1[0], w_cmp_v2[0])
    w_out = _to_bf16(w_nsa_out, 0)
    kv_shape = (2, N_KV_HEADS, HEAD_DIM)
    row_w = 2 * KV_W

    q_t, g_t, kvc_t, kvs_t, kvw_t, ks, kw, vs_t, vw_t = _nsa_project_prompt(xp, norm_mix[1], proj_w, seq_len=tp, tm=512)
    n_cmp = tp // L_CMP
    kc, vc_t = _compress_seq(kvc_t, cmp_w)
    kc = _even_odd(kc.reshape(bp, n_cmp, KV_W), 1)
    vc_t = _even_odd(vc_t.reshape(KV_W, bp, n_cmp), 2).transpose(1, 0, 2)
    seq = lambda a: a.reshape(bp, tp, a.shape[-1])
    xp3 = _attn_prompt(seq(xp), q_t, g_t, kc, vc_t, seq(ks), vs_t, seq(kw), vw_t, w_out)
    by_row = lambda a: a.reshape((bp,) + kv_shape + (a.shape[-1],)).transpose(0, 4, 1, 2, 3)[None]
    cmp_p = by_row(kvc_t)
    sel_p = by_row(kvs_t)
    win_p = by_row(kvw_t[:, :, tp - min(WINDOW, tp):])

    q_t, g_t, kvc, kvs, kvw = _nsa_project_sample(xs, norm_mix[1], proj_w)
    n_pages = page_table.shape[1]
    past = n_pages * PAGE_SIZE
    n_new = -(-ts // L_SEL) * L_SEL
    n_past_blk = past // L_SEL
    n_blk = n_past_blk + n_new // L_SEL
    n_top = min(TOP_N, n_blk)
    lanes = lambda a: jnp.pad(a.reshape(a.shape[0], bs, ts).transpose(1, 0, 2), ((0, 0), (0, 0), (0, TQ_SAMPLE - ts)))
    by_channel = lambda c: c.transpose(0, 2, 3, 4, 1)
    kc_past, vct_past = _compress_paged(by_channel(cache_cmp_kv[0]).reshape(-1, row_w, PAGE_SIZE), page_table, cmp_w)
    new_c = jnp.pad(kvc.reshape(bs, ts, row_w), ((0, 0), (0, n_new - ts), (0, 0)))
    kc_new, vct_new = _compress(new_c.reshape(bs * n_new // L_CMP, L_CMP * row_w), cmp_w, tm=bs * n_new // L_CMP)
    half = -(-n_blk // LANES) * LANES
    kc_all = _even_odd(jnp.concatenate([kc_past.reshape(bs, -1, KV_W), kc_new.reshape(bs, -1, KV_W)], axis=1), 1, half)
    vct_all = _even_odd(jnp.concatenate([vct_past.reshape(KV_W, bs, -1), vct_new.reshape(KV_W, bs, -1)], axis=2), 2, half)
    wb = cache_win_kv.shape[2]
    new_w = jnp.pad(kvw.reshape(bs, ts, row_w), ((0, 0), (0, NEW_ROWS - ts), (0, 0)))
    part, idx = _attn_sample(lanes(q_t), lanes(g_t), kc_all, vct_all.transpose(1, 0, 2),
                             by_channel(cache_win_kv[0]).reshape(bs, row_w, wb), new_w, past=past, n_blk=n_blk)
    bid = idx[..., :ts].transpose(0, 3, 1, 2)
    q5 = q_t.reshape(N_KV_HEADS, GROUP, HEAD_DIM, bs, ts).transpose(3, 4, 0, 1, 2)
    q5 = jnp.pad(q5, ((0, 0),) * 3 + ((0, Q_ROWS - GROUP), (0, 0)))
    new_s = jnp.pad(kvs.reshape((bs, ts) + kv_shape).transpose(0, 2, 3, 4, 1), ((0, 0),) * 4 + ((0, PAGE_SIZE - ts),))
    o_s = _sel_sample(page_table, bid.reshape(-1), q5, by_channel(cache_sel_kv[0]), new_s,
                      past=past, n_past_blk=n_past_blk, n_top=n_top)
    o_s = o_s[:, :, :, :GROUP].reshape(n_s, N_Q)
    g_s = jnp.repeat(g_t[N_HEADS:2 * N_HEADS].T, HEAD_DIM, axis=1)
    part = part[:, :, :ts].transpose(0, 2, 1).reshape(n_s, N_Q)
    xs = _out_sample(xs, part, o_s, g_s, w_out)
    cmp_s = kvc.reshape((1, bs, ts) + kv_shape)
    sel_s = kvs.reshape((1, bs, ts) + kv_shape)
    win_s = jnp.concatenate([cache_win_kv[0], kvw.reshape((bs, ts) + kv_shape)], axis=1)[None, :, ts:]

    xp, xs = ffn_pair(xp3.reshape(n_p, D_MODEL), xs, norm_ffb[1], w_ffb_gu, w_ffb_down, 1, final=True)
    return (xp.reshape(bp, tp, D_MODEL), xs.reshape(bs, ts, D_MODEL),
            conv_p[None], conv_s[None], cmp_p, cmp_s, sel_p, sel_s, win_p, win_s)
```

```python
import functools

import jax
import jax.numpy as jnp
from jax import lax
from jax.experimental import pallas as pl
from jax.experimental.pallas import tpu as pltpu

D_MODEL = 1024
D_FF = 2816
CONV_W = 3
N_HEADS = 16
HEAD_DIM = 64
N_KV_HEADS = 4
GROUP = N_HEADS // N_KV_HEADS
KV_W = N_KV_HEADS * HEAD_DIM
L_CMP = 32
L_SEL = 64
TOP_N = 16
WINDOW = 512
PAGE_SIZE = 128
NORM_EPS = 1e-6
FORCE_BONUS = 1e4
NEG = -1e30

LANES = 128
VMEM_LIMIT = 56 * 1024 * 1024

F32 = jnp.float32
BF16 = jnp.bfloat16


def _params(*sem):
    return pltpu.CompilerParams(dimension_semantics=sem, vmem_limit_bytes=VMEM_LIMIT)


def _rms(x, g):
    return x * lax.rsqrt(jnp.mean(x * x, axis=-1, keepdims=True) + NORM_EPS) * g


def _dot(a, b):
    return jnp.dot(a, b, preferred_element_type=F32)


def _dot_nt(a, b):
    return lax.dot_general(a, b, (((1,), (1,)), ((), ())), preferred_element_type=F32)


def _dot_tn(a, b):
    return lax.dot_general(a, b, (((0,), (0,)), ((), ())), preferred_element_type=F32)


def _cast_kernel(w_ref, o_ref):
    o_ref[...] = w_ref[...].astype(BF16)


def _to_bf16(w, layer, *, steps=4):
    _, r, c = w.shape
    tr = r // steps
    return pl.pallas_call(
        _cast_kernel,
        grid=(steps,),
        in_specs=[pl.BlockSpec((None, tr, c), lambda i: (layer, i, 0))],
        out_specs=pl.BlockSpec((tr, c), lambda i: (i, 0)),
        out_shape=jax.ShapeDtypeStruct((r, c), BF16),
        compiler_params=_params("parallel"),
        name="to_bf16",
    )(w)


def _ffn_kernel(x_ref, g_ref, wg_ref, wu_ref, wd_ref, gf_ref, o_ref, *, final):
    x = x_ref[...]
    xn = _rms(x, g_ref[...]).astype(BF16)
    gate = _dot(xn, wg_ref[...])
    up = _dot(xn, wu_ref[...])
    act = (gate * jax.nn.sigmoid(gate) * up).astype(BF16)
    y = x + 0.5 * _dot(act, wd_ref[...])
    o_ref[...] = _rms(y, gf_ref[...]) if final else y


def _resident(shape, index):
    return pl.BlockSpec(shape, index, pipeline_mode=pl.Buffered(1))


def _ffn(x, g, w_gu, w_down, g_final, *, tm, final=False):
    n = x.shape[0]
    return pl.pallas_call(
        functools.partial(_ffn_kernel, final=final),
        grid=(n // tm,),
        in_specs=[
            pl.BlockSpec((tm, D_MODEL), lambda i: (i, 0)),
            pl.BlockSpec((1, D_MODEL), lambda i: (0, 0)),
            _resident((D_MODEL, D_FF), lambda i: (0, 0)),
            _resident((D_MODEL, D_FF), lambda i: (0, 1)),
            _resident((D_FF, D_MODEL), lambda i: (0, 0)),
            pl.BlockSpec((1, D_MODEL), lambda i: (0, 0)),
        ],
        out_specs=pl.BlockSpec((tm, D_MODEL), lambda i: (i, 0)),
        out_shape=jax.ShapeDtypeStruct((n, D_MODEL), F32),
        compiler_params=_params("parallel"),
        name="ffn_final" if final else "ffn",
    )(x, g.reshape(1, D_MODEL), w_gu, w_gu, w_down, g_final.reshape(1, D_MODEL))


CARRY = 8


def _conv_prompt_kernel(x_ref, g_ref, win_ref, wc_ref, wout_ref, o_ref, st_ref, uext_ref, *, tm):
    @pl.when(pl.program_id(1) == 0)
    def _():
        uext_ref[0:CARRY, :] = jnp.zeros((CARRY, D_MODEL), F32)

    x = x_ref[0]
    h = _rms(x, g_ref[...]).astype(BF16)
    p = _dot(h, win_ref[...])
    bg = p[:, :D_MODEL]
    u = p[:, D_MODEL:2 * D_MODEL] * p[:, 2 * D_MODEL:]
    uext_ref[CARRY:CARRY + tm, :] = u
    wc = wc_ref[...]
    conv = (wc[0:1] * uext_ref[CARRY - 2:CARRY - 2 + tm, :]
            + wc[1:2] * uext_ref[CARRY - 1:CARRY - 1 + tm, :]
            + wc[2:3] * u)
    y = _dot((bg * conv).astype(BF16), wout_ref[...])
    o_ref[0] = x + y
    st_ref[0] = uext_ref[CARRY + tm - 2:CARRY + tm, :]
    uext_ref[0:CARRY, :] = uext_ref[tm:tm + CARRY, :]


def _conv_prompt(x, g, w_in, w_conv, w_out, *, tm=512):
    b, t, _ = x.shape
    return pl.pallas_call(
        functools.partial(_conv_prompt_kernel, tm=tm),
        grid=(b, t // tm),
        in_specs=[
            pl.BlockSpec((1, tm, D_MODEL), lambda bi, ti: (bi, ti, 0)),
            pl.BlockSpec((1, D_MODEL), lambda bi, ti: (0, 0)),
            pl.BlockSpec((D_MODEL, 3 * D_MODEL), lambda bi, ti: (0, 0)),
            pl.BlockSpec((CONV_W, D_MODEL), lambda bi, ti: (0, 0)),
            pl.BlockSpec((D_MODEL, D_MODEL), lambda bi, ti: (0, 0)),
        ],
        out_specs=[
            pl.BlockSpec((1, tm, D_MODEL), lambda bi, ti: (bi, ti, 0)),
            pl.BlockSpec((1, CONV_W - 1, D_MODEL), lambda bi, ti: (bi, 0, 0)),
        ],
        out_shape=[jax.ShapeDtypeStruct((b, t, D_MODEL), F32),
                   jax.ShapeDtypeStruct((b, CONV_W - 1, D_MODEL), F32)],
        scratch_shapes=[pltpu.VMEM((CARRY + tm, D_MODEL), F32)],
        compiler_params=_params("parallel", "arbitrary"),
        name="conv_prompt",
    )(x, g.reshape(1, D_MODEL), w_in, w_conv, w_out)


def _conv_sample_kernel(x_ref, g_ref, win_ref, wc_ref, wout_ref, s1_ref, s2_ref, o_ref, u_ref, *, t_len):
    x = x_ref[...]
    h = _rms(x, g_ref[...]).astype(BF16)
    p = _dot(h, win_ref[...])
    bg = p[:, :D_MODEL]
    u = p[:, D_MODEL:2 * D_MODEL] * p[:, 2 * D_MODEL:]
    pos = lax.broadcasted_iota(jnp.int32, u.shape, 0) % t_len
    u1 = jnp.where(pos >= 1, pltpu.roll(u, 1, axis=0), s1_ref[...])
    u2 = jnp.where(pos >= 2, pltpu.roll(u, 2, axis=0), s2_ref[...])
    wc = wc_ref[...]
    conv = wc[0:1] * u2 + wc[1:2] * u1 + wc[2:3] * u
    o_ref[...] = x + _dot((bg * conv).astype(BF16), wout_ref[...])
    u_ref[...] = u


def _conv_sample(x, state, g, w_in, w_conv, w_out):
    b, t, _ = x.shape
    n = b * t
    zeros = jnp.zeros((b, t, D_MODEL), F32)
    s1 = zeros.at[:, 0].set(state[:, 1]).reshape(n, D_MODEL)
    s2 = zeros.at[:, 0].set(state[:, 0]).at[:, 1].set(state[:, 1]).reshape(n, D_MODEL)
    y, u = pl.pallas_call(
        functools.partial(_conv_sample_kernel, t_len=t),
        out_shape=[jax.ShapeDtypeStruct((n, D_MODEL), F32), jax.ShapeDtypeStruct((n, D_MODEL), F32)],
        compiler_params=pltpu.CompilerParams(vmem_limit_bytes=VMEM_LIMIT),
        name="conv_sample",
    )(x.reshape(n, D_MODEL), g.reshape(1, D_MODEL), w_in, w_conv, w_out, s1, s2)
    return y.reshape(b, t, D_MODEL), u.reshape(b, t, D_MODEL)[:, t - (CONV_W - 1):]


N_Q = N_HEADS * HEAD_DIM
Q_SCALE = HEAD_DIM ** -0.5 * 1.4426950408889634


def _proj_common(x_ref, g_ref, wqt_ref, wgt_ref, qt_ref, gt_ref):
    h = _rms(x_ref[...], g_ref[...]).astype(BF16)
    qt_ref[...] = (_dot_nt(wqt_ref[...], h) * Q_SCALE).astype(BF16)
    gt_ref[...] = jax.nn.sigmoid(_dot_nt(wgt_ref[...], h))
    return h


def _proj_prompt_kernel(x_ref, g_ref, wqt_ref, wgt_ref, wkvt_ref, wk_ref,
                        qt_ref, gt_ref, kvct_ref, kvst_ref, kvwt_ref, ks_ref, kw_ref, vst_ref, vwt_ref):
    h = _proj_common(x_ref, g_ref, wqt_ref, wgt_ref, qt_ref, gt_ref)
    kv_t = _dot_nt(wkvt_ref[...], h)
    kvct_ref[0] = kv_t[:2 * KV_W]
    kvst_ref[0] = kv_t[2 * KV_W:4 * KV_W]
    kvwt_ref[0] = kv_t[4 * KV_W:]
    vst_ref[...] = kv_t[3 * KV_W:4 * KV_W].astype(BF16)
    vwt_ref[...] = kv_t[5 * KV_W:].astype(BF16)
    k = _dot(h, wk_ref[...]).astype(BF16)
    ks_ref[...] = k[:, :KV_W]
    kw_ref[...] = k[:, KV_W:]


def _proj_sample_kernel(x_ref, g_ref, wqt_ref, wgt_ref, wkv_ref, qt_ref, gt_ref, kvc_ref, kvs_ref, kvw_ref):
    h = _proj_common(x_ref, g_ref, wqt_ref, wgt_ref, qt_ref, gt_ref)
    kv = _dot(h, wkv_ref[...])
    kvc_ref[...] = kv[:, :2 * KV_W]
    kvs_ref[...] = kv[:, 2 * KV_W:4 * KV_W]
    kvw_ref[...] = kv[:, 4 * KV_W:]


def _nsa_weights(w_in):
    wkv = w_in[:, N_Q:N_Q + 6 * KV_W]
    wk = jnp.concatenate([wkv[:, 2 * KV_W:3 * KV_W], wkv[:, 4 * KV_W:5 * KV_W]], axis=1)
    wgt = jnp.pad(w_in[:, N_Q + 6 * KV_W:], ((0, 0), (0, LANES - 3 * N_HEADS))).T
    return dict(wqt=w_in[:, :N_Q].T.astype(BF16), wgt=wgt.astype(BF16), wkv=wkv.astype(BF16),
                wkvt=wkv.T.astype(BF16), wk=wk.astype(BF16))


def _nsa_project_prompt(x, g, w, *, seq_len, tm):
    n = x.shape[0]
    per_seq = seq_len // tm
    row = lambda w_: pl.BlockSpec((tm, w_), lambda i: (i, 0))
    col = lambda h: pl.BlockSpec((h, tm), lambda i: (0, i))
    seq_t = pl.BlockSpec((1, 2 * KV_W, tm), lambda i: (i // per_seq, 0, i % per_seq))
    full = lambda a: pl.BlockSpec(a.shape, lambda i: (0, 0))
    weights = [w["wqt"], w["wgt"], w["wkvt"], w["wk"]]
    return pl.pallas_call(
        _proj_prompt_kernel,
        grid=(n // tm,),
        in_specs=[row(D_MODEL), pl.BlockSpec((1, D_MODEL), lambda i: (0, 0))] + [full(a) for a in weights],
        out_specs=[col(N_Q), col(LANES), seq_t, seq_t, seq_t, row(KV_W), row(KV_W), col(KV_W), col(KV_W)],
        out_shape=[jax.ShapeDtypeStruct((N_Q, n), BF16), jax.ShapeDtypeStruct((LANES, n), F32)]
        + [jax.ShapeDtypeStruct((n // seq_len, 2 * KV_W, seq_len), F32)] * 3
        + [jax.ShapeDtypeStruct((n, KV_W), BF16)] * 2
        + [jax.ShapeDtypeStruct((KV_W, n), BF16)] * 2,
        compiler_params=_params("parallel"),
        name="nsa_project_prompt",
    )(x, g.reshape(1, D_MODEL), *weights)


def _nsa_project_sample(x, g, w):
    n = x.shape[0]
    return pl.pallas_call(
        _proj_sample_kernel,
        out_shape=[jax.ShapeDtypeStruct((N_Q, n), BF16), jax.ShapeDtypeStruct((LANES, n), F32)]
        + [jax.ShapeDtypeStruct((n, 2 * KV_W), F32)] * 3,
        compiler_params=pltpu.CompilerParams(vmem_limit_bytes=VMEM_LIMIT),
        name="nsa_project_sample",
    )(x, g.reshape(1, D_MODEL), w["wqt"], w["wgt"], w["wkv"])


def _compress_out(acck, accv, w2k_ref, w2vt_ref, kc_ref, vct_ref):
    kc_ref[...] = _dot(jax.nn.gelu(acck).astype(BF16), w2k_ref[...]).astype(BF16)
    vct_ref[...] = _dot_nt(w2vt_ref[...], jax.nn.gelu(accv).astype(BF16)).astype(BF16)


def _compress_kernel(x_ref, pe_ref, w1k_ref, w1v_ref, w2k_ref, w2vt_ref, kc_ref, vct_ref, acck_ref, accv_ref):
    l = pl.program_id(1)

    @pl.when(l == 0)
    def _():
        acck_ref[...] = jnp.zeros_like(acck_ref)
        accv_ref[...] = jnp.zeros_like(accv_ref)

    xb = x_ref[...] + pe_ref[...]
    acck_ref[...] += _dot(xb[:, :KV_W].astype(BF16), w1k_ref[...])
    accv_ref[...] += _dot(xb[:, KV_W:].astype(BF16), w1v_ref[...])

    @pl.when(l == L_CMP - 1)
    def _():
        _compress_out(acck_ref[...], accv_ref[...], w2k_ref, w2vt_ref, kc_ref, vct_ref)


PAGES_PER_STEP = 32
BLOCKS_PER_PAGE = PAGE_SIZE // L_CMP


def _compress_weights(pe, w1k, w2k, w1v, w2v):
    eye = jnp.eye(N_KV_HEADS, dtype=F32)
    bd1 = lambda w1: jax.vmap(lambda w: jnp.kron(eye, w))(w1).astype(BF16)
    pe_row = jnp.tile(pe, (1, 2 * N_KV_HEADS))
    r = jnp.arange(2 * PAGE_SIZE)
    src = (r % 8 // BLOCKS_PER_PAGE) * PAGE_SIZE + (r % BLOCKS_PER_PAGE) * L_CMP + r // 8
    perm = (src[:, None] == r[None, :]).astype(BF16)
    return dict(pe_row=pe_row.reshape(L_CMP, 1, 2 * KV_W), pe_page_t=jnp.tile(pe_row, (BLOCKS_PER_PAGE, 1)).T, perm=perm,
                w1k=bd1(w1k), w1v=bd1(w1v), w2k=jnp.kron(eye, w2k).astype(BF16), w2vt=jnp.kron(eye, w2v).T.astype(BF16))


def _compress(x, w, *, tm):
    m = x.shape[0]
    tm = min(tm, m)
    return pl.pallas_call(
        _compress_kernel,
        grid=(m // tm, L_CMP),
        in_specs=[
            pl.BlockSpec((tm, 2 * KV_W), lambda i, l: (i, l)),
            pl.BlockSpec((None, 1, 2 * KV_W), lambda i, l: (l, 0, 0)),
            pl.BlockSpec((None, KV_W, KV_W), lambda i, l: (l, 0, 0)),
            pl.BlockSpec((None, KV_W, KV_W), lambda i, l: (l, 0, 0)),
            pl.BlockSpec((KV_W, KV_W), lambda i, l: (0, 0)),
            pl.BlockSpec((KV_W, KV_W), lambda i, l: (0, 0)),
        ],
        out_specs=[pl.BlockSpec((tm, KV_W), lambda i, l: (i, 0)), pl.BlockSpec((KV_W, tm), lambda i, l: (0, i))],
        out_shape=[jax.ShapeDtypeStruct((m, KV_W), BF16), jax.ShapeDtypeStruct((KV_W, m), BF16)],
        scratch_shapes=[pltpu.VMEM((tm, KV_W), F32)] * 2,
        compiler_params=_params("parallel", "arbitrary"),
        name="compress",
    )(x, w["pe_row"], w["w1k"], w["w1v"], w["w2k"], w["w2vt"])


def _compress_pages_body(refs, n_pages):
    pages = refs[:n_pages]
    pe_ref, perm_ref, w1k_ref, w1v_ref, w2k_ref, w2vt_ref, kc_ref, vct_ref = refs[n_pages:]
    pe = pe_ref[...]
    perm = perm_ref[...]
    by_l = []
    for p in range(0, n_pages, 2):
        pair_t = jnp.concatenate([(pages[p][...] + pe).astype(BF16), (pages[p + 1][...] + pe).astype(BF16)], axis=1)
        by_l.append(_dot_nt(perm, pair_t))
    acck = jnp.zeros((n_pages * BLOCKS_PER_PAGE, KV_W), F32)
    accv = jnp.zeros((n_pages * BLOCKS_PER_PAGE, KV_W), F32)
    for l in range(L_CMP):
        xl = jnp.concatenate([y[l * 8:(l + 1) * 8] for y in by_l], axis=0).astype(BF16)
        acck = acck + _dot(xl[:, :KV_W], w1k_ref[l])
        accv = accv + _dot(xl[:, KV_W:], w1v_ref[l])
    _compress_out(acck, accv, w2k_ref, w2vt_ref, kc_ref, vct_ref)


def _compress_paged_kernel(pt_ref, *refs, n_pages):
    del pt_ref
    _compress_pages_body(refs, n_pages)


def _compress_seq_kernel(*refs, n_pages):
    _compress_pages_body(refs, n_pages)


def _compress_consts(w):
    return [w["pe_page_t"], w["perm"], w["w1k"], w["w1v"], w["w2k"], w["w2vt"]]


def _compress_paged(cache, page_table, w):
    b, n_pages = page_table.shape
    pps = min(PAGES_PER_STEP, n_pages)
    steps = n_pages // pps
    m_step = pps * BLOCKS_PER_PAGE
    m = b * n_pages * BLOCKS_PER_PAGE
    page = lambda j: pl.BlockSpec((None, 2 * KV_W, PAGE_SIZE), lambda bi, s, pt: (pt[bi, s * pps + j], 0, 0))
    full = lambda a: pl.BlockSpec(a.shape, lambda bi, s, pt: (0,) * a.ndim)
    consts = _compress_consts(w)
    return pl.pallas_call(
        functools.partial(_compress_paged_kernel, n_pages=pps),
        grid_spec=pltpu.PrefetchScalarGridSpec(
            num_scalar_prefetch=1,
            grid=(b, steps),
            in_specs=[page(j) for j in range(pps)] + [full(a) for a in consts],
            out_specs=[pl.BlockSpec((m_step, KV_W), lambda bi, s, pt: (bi * steps + s, 0)),
                       pl.BlockSpec((KV_W, m_step), lambda bi, s, pt: (0, bi * steps + s))],
        ),
        out_shape=[jax.ShapeDtypeStruct((m, KV_W), BF16), jax.ShapeDtypeStruct((KV_W, m), BF16)],
        compiler_params=_params("parallel", "arbitrary"),
        name="compress_paged",
    )(page_table, *([cache] * pps), *consts)


def _compress_seq(rows_t, w):
    b, _, t = rows_t.shape
    per_seq = t // PAGE_SIZE
    n_pages = b * per_seq
    pps = min(PAGES_PER_STEP, n_pages)
    m_step = pps * BLOCKS_PER_PAGE
    m = n_pages * BLOCKS_PER_PAGE
    page = lambda j: pl.BlockSpec((None, 2 * KV_W, PAGE_SIZE),
                                  lambda s: ((s * pps + j) // per_seq, 0, (s * pps + j) % per_seq))
    full = lambda a: pl.BlockSpec(a.shape, lambda s: (0,) * a.ndim)
    consts = _compress_consts(w)
    return pl.pallas_call(
        functools.partial(_compress_seq_kernel, n_pages=pps),
        grid=(n_pages // pps,),
        in_specs=[page(j) for j in range(pps)] + [full(a) for a in consts],
        out_specs=[pl.BlockSpec((m_step, KV_W), lambda s: (s, 0)), pl.BlockSpec((KV_W, m_step), lambda s: (0, s))],
        out_shape=[jax.ShapeDtypeStruct((m, KV_W), BF16), jax.ShapeDtypeStruct((KV_W, m), BF16)],
        compiler_params=_params("parallel"),
        name="compress_seq",
    )(*([rows_t] * pps), *consts)


TQ_PROMPT = 256
TQ_SAMPLE = 128
KCH = 512
ONES_ROWS = 16


def _head_queries(q_t, kv):
    zero = jnp.zeros((HEAD_DIM, q_t.shape[1]), BF16)
    cols = []
    for g in range(GROUP):
        h = kv * GROUP + g
        qg = q_t[h * HEAD_DIM:(h + 1) * HEAD_DIM, :]
        cols.append(jnp.concatenate([qg, zero] if kv % 2 == 0 else [zero, qg], axis=0))
    return jnp.concatenate(cols, axis=1)


def _slab(kv):
    return slice((kv // 2) * LANES, (kv // 2 + 1) * LANES)


def _head(kv):
    return slice(kv * HEAD_DIM, (kv + 1) * HEAD_DIM)


def _cmp_branch(kc, vc_t, x, keep):
    s = jnp.where(keep, _dot(kc, x), NEG)
    e = jnp.exp2(s - jnp.max(s, axis=0, keepdims=True))
    l = jnp.sum(e, axis=0, keepdims=True)
    e = jnp.where(keep, e, 0.0)
    return _dot(vc_t, e.astype(BF16)) / l, e / l


def _group_sum(p):
    tq = p.shape[1] // GROUP
    out = p[:, 0:tq]
    for g in range(1, GROUP):
        out = out + p[:, g * tq:(g + 1) * tq]
    return out


def _pair_sum(p):
    n = p.shape[0] // 2
    pair = (lax.broadcasted_iota(jnp.int32, (n, 2 * n), 1) // 2 == lax.broadcasted_iota(jnp.int32, (n, 2 * n), 0))
    return jnp.dot(pair.astype(F32), p, precision=lax.Precision.HIGHEST, preferred_element_type=F32)


def _with_ones(v_t):
    return jnp.concatenate([v_t, jnp.ones((ONES_ROWS, v_t.shape[1]), BF16)], axis=0)


def _topn_rows(score, n):
    nb = score.shape[0]
    row = lax.broadcasted_iota(jnp.int32, score.shape, 0).astype(F32)
    picked = jnp.zeros(score.shape, F32)
    ids = []
    for _ in range(n):
        top = jnp.max(score, axis=0, keepdims=True)
        first = jnp.min(jnp.where(score == top, row, float(nb)), axis=0, keepdims=True)
        hit = row == first
        picked = jnp.where(hit, 1.0, picked)
        score = jnp.where(hit, -jnp.inf, score)
        ids.append(first)
    return picked, ids


def _masked_softmax_pv(s, keep, v_t):
    tq = keep.shape[1]
    s = jnp.concatenate([jnp.where(keep, s[:, g * tq:(g + 1) * tq], NEG) for g in range(GROUP)], axis=1)
    e = jnp.exp2(s - jnp.max(s, axis=0, keepdims=True))
    o = _dot(_with_ones(v_t), e.astype(BF16))
    return o[:HEAD_DIM] / o[HEAD_DIM:HEAD_DIM + 1]


def _gate_mix(g_t, kv, branches):
    tq = g_t.shape[1]
    out = []
    for g in range(GROUP):
        h = kv * GROUP + g
        c = slice(g * tq, (g + 1) * tq)
        acc = None
        for br, o in branches:
            term = g_t[br * N_HEADS + h:br * N_HEADS + h + 1, :] * o[:, c]
            acc = term if acc is None else acc + term
        out.append(acc)
    return out


def _attn_prompt_kernel(x_ref, qt_ref, gt_ref, kc_ref, vct_ref, ks_ref, vst_ref, kw_ref, *rest, t_len, tq):
    win_blocks = WINDOW // tq + 1
    cols = GROUP * tq
    vwt_refs = rest[:win_blocks]
    wout_ref, o_ref, mask_ref, m_ref, acc_ref = rest[win_blocks:]
    t0 = pl.program_id(1) * tq
    n_sel = t_len // L_SEL
    n_cmp = t_len // L_CMP
    t_tok = t0 + lax.broadcasted_iota(jnp.int32, (1, tq), 1)
    t_col = t0 + lax.broadcasted_iota(jnp.int32, (1, cols), 1) % tq
    t_heads = t0 + lax.broadcasted_iota(jnp.int32, (1, N_KV_HEADS * tq), 1) % tq
    r_c = lax.broadcasted_iota(jnp.int32, (n_cmp, 1), 0)
    keep_c = (r_c + 1) * L_CMP - 1 <= t_col
    blk = lax.broadcasted_iota(jnp.int32, (n_sel, N_KV_HEADS * tq), 0)
    cur = t_heads // L_SEL
    forced = (blk == 0) | (blk == cur) | (blk == cur - 1)
    valid = blk <= cur
    key_in_blk = lax.broadcasted_iota(jnp.int32, (L_SEL, 1), 0)
    win_pos = t0 - WINDOW + lax.broadcasted_iota(jnp.int32, (win_blocks * tq, 1), 0)
    dist = t_tok - win_pos
    keep_w = (dist >= 0) & (dist < WINDOW) & (win_pos >= 0)
    n_chunks = t_len // KCH
    last_chunk = (t0 + tq - 1) // KCH
    q_t = qt_ref[...]
    g_t = gt_ref[...]

    o_cmp, imp = [], []
    for kv in range(N_KV_HEADS):
        o_c, p = _cmp_branch(kc_ref[0, :, _slab(kv)], vct_ref[0, _head(kv), :], _head_queries(q_t, kv), keep_c)
        o_cmp.append(o_c)
        imp.append(_group_sum(p))
    score = jnp.where(forced, FORCE_BONUS, jnp.where(valid, _pair_sum(jnp.concatenate(imp, axis=1)), -FORCE_BONUS))
    picked, _ = _topn_rows(score, min(TOP_N, n_sel))
    sel = jnp.where(valid, picked, 0.0)
    for kv in range(N_KV_HEADS):
        for j in range(n_sel):
            causal = j * L_SEL + key_in_blk <= t_tok
            mask_ref[kv, j * L_SEL:(j + 1) * L_SEL, :] = jnp.where(
                causal, jnp.broadcast_to(sel[j:j + 1, kv * tq:(kv + 1) * tq], (L_SEL, tq)), 0.0)

    m_ref[...] = jnp.full(m_ref.shape, NEG, F32)
    acc_ref[...] = jnp.zeros_like(acc_ref)
    for c in range(n_chunks):
        @pl.when(c <= last_chunk)
        def _():
            keys = slice(c * KCH, (c + 1) * KCH)
            for kv in range(N_KV_HEADS):
                s = _dot(ks_ref[0, keys, _slab(kv)], _head_queries(q_t, kv))
                keep = mask_ref[kv, keys, :] > 0.5
                s = jnp.concatenate([jnp.where(keep, s[:, g * tq:(g + 1) * tq], NEG) for g in range(GROUP)], axis=1)
                m_old = m_ref[kv]
                m_new = jnp.maximum(m_old, jnp.max(s, axis=0, keepdims=True))
                e = jnp.exp2(s - m_new).astype(BF16)
                acc_ref[kv] = jnp.exp2(m_old - m_new) * acc_ref[kv] + _dot(_with_ones(vst_ref[_head(kv), keys]), e)
                m_ref[kv] = m_new

    heads = []
    for kv in range(N_KV_HEADS):
        x = _head_queries(q_t, kv)
        s_w = []
        for j in range(win_blocks):
            start = pl.multiple_of(jnp.maximum(t0 - WINDOW + j * tq, 0), tq)
            s_w.append(_dot(kw_ref[0, pl.ds(start, tq), _slab(kv)], x))
        v_w = jnp.concatenate([r[_head(kv), :] for r in vwt_refs], axis=1)
        o_w = _masked_softmax_pv(jnp.concatenate(s_w, axis=0), keep_w, v_w)
        acc = acc_ref[kv]
        o_s = acc[:HEAD_DIM] / acc[HEAD_DIM:HEAD_DIM + 1]
        heads += _gate_mix(g_t, kv, ((0, o_cmp[kv]), (1, o_s), (2, o_w)))
    o = jnp.concatenate(heads, axis=0).T.astype(BF16)
    o_ref[0] = x_ref[0] + _dot(o, wout_ref[...])


def _attn_prompt(x, q_t, g_t, kc, vc_t, ks, vs_t, kw, vw_t, w_out, *, tq=TQ_PROMPT):
    b, t, _ = x.shape
    nt = t // tq
    n_cmp = t // L_CMP
    win_blocks = WINDOW // tq + 1
    cols = GROUP * tq
    tile = lambda w: pl.BlockSpec((1, tq, w), lambda bi, i: (bi, i, 0))
    tile_t = lambda h: pl.BlockSpec((h, tq), lambda bi, i: (0, bi * nt + i))
    win_t = lambda j: pl.BlockSpec((KV_W, tq), lambda bi, i: (0, bi * nt + jnp.maximum(i - (win_blocks - 1) + j, 0)))
    return pl.pallas_call(
        functools.partial(_attn_prompt_kernel, t_len=t, tq=tq),
        grid=(b, nt),
        in_specs=[tile(D_MODEL), tile_t(N_Q), tile_t(LANES),
                  pl.BlockSpec((1, n_cmp, KV_W), lambda bi, i: (bi, 0, 0)),
                  pl.BlockSpec((1, KV_W, n_cmp), lambda bi, i: (bi, 0, 0)),
                  pl.BlockSpec((1, t, KV_W), lambda bi, i: (bi, 0, 0)),
                  pl.BlockSpec((KV_W, t), lambda bi, i: (0, bi)),
                  pl.BlockSpec((1, t, KV_W), lambda bi, i: (bi, 0, 0))]
        + [win_t(j) for j in range(win_blocks)]
        + [pl.BlockSpec((N_Q, D_MODEL), lambda bi, i: (0, 0))],
        out_specs=tile(D_MODEL),
        out_shape=jax.ShapeDtypeStruct((b, t, D_MODEL), F32),
        scratch_shapes=[pltpu.VMEM((N_KV_HEADS, t, tq), F32), pltpu.VMEM((N_KV_HEADS, 1, cols), F32),
                        pltpu.VMEM((N_KV_HEADS, HEAD_DIM + ONES_ROWS, cols), F32)],
        compiler_params=_params("parallel", "arbitrary"),
        name="attn_prompt",
    )(x, q_t, g_t, kc, vc_t, ks, vs_t, kw, *([vw_t] * win_blocks), w_out)


NEW_ROWS = 16


def _attn_sample_kernel(qt_ref, gt_ref, kc_ref, vct_ref, win_ref, new_ref, part_ref, idx_ref, *, past, n_blk, half):
    tq = TQ_SAMPLE
    t_tok = past + lax.broadcasted_iota(jnp.int32, (1, tq), 1)
    t_col = past + lax.broadcasted_iota(jnp.int32, (1, GROUP * tq), 1) % tq
    t_heads = past + lax.broadcasted_iota(jnp.int32, (1, N_KV_HEADS * tq), 1) % tq
    r_c = lax.broadcasted_iota(jnp.int32, (2 * half, 1), 0)
    keep_c = (r_c < 2 * n_blk) & ((r_c + 1) * L_CMP - 1 <= t_col)
    blk = lax.broadcasted_iota(jnp.int32, (half, N_KV_HEADS * tq), 0)
    cur = t_heads // L_SEL
    forced = (blk == 0) | (blk == cur) | (blk == cur - 1)
    valid = blk <= cur
    wb = win_ref.shape[2]
    win_pos = past - wb + lax.broadcasted_iota(jnp.int32, (wb + NEW_ROWS, 1), 0)
    dist = t_tok - win_pos
    keep_w = (dist >= 0) & (dist < WINDOW) & (win_pos >= 0)
    q_t = qt_ref[0]
    g_t = gt_ref[0]
    o_cmp, imp = [], []
    for kv in range(N_KV_HEADS):
        o_c, p = _cmp_branch(kc_ref[0, :, _slab(kv)], vct_ref[0, _head(kv), :], _head_queries(q_t, kv), keep_c)
        o_cmp.append(o_c)
        imp.append(_group_sum(p))
    score = jnp.where(forced, FORCE_BONUS, jnp.where(valid, _pair_sum(jnp.concatenate(imp, axis=1)), -FORCE_BONUS))
    score = jnp.where(blk < n_blk, score, -jnp.inf)
    _, ids = _topn_rows(score, min(TOP_N, n_blk))
    for j, first in enumerate(ids):
        for kv in range(N_KV_HEADS):
            idx_ref[0, kv, j:j + 1, :] = first[:, kv * tq:(kv + 1) * tq].astype(jnp.int32)

    heads = []
    for kv in range(N_KV_HEADS):
        x = _head_queries(q_t, kv)
        mine = slice((kv % 2) * HEAD_DIM, (kv % 2 + 1) * HEAD_DIM)
        v_slab = slice(KV_W + (kv // 2) * LANES, KV_W + (kv // 2 + 1) * LANES)
        s = jnp.concatenate([_dot_tn(win_ref[0, _head(kv), :], x[mine].astype(F32)),
                             _dot(new_ref[0, :, _slab(kv)].astype(BF16), x)], axis=0)
        s = jnp.concatenate([jnp.where(keep_w, s[:, g * tq:(g + 1) * tq], NEG) for g in range(GROUP)], axis=1)
        e = jnp.exp2(s - jnp.max(s, axis=0, keepdims=True))
        p_w = e / jnp.sum(e, axis=0, keepdims=True)
        v_t = win_ref[0, KV_W + kv * HEAD_DIM:KV_W + (kv + 1) * HEAD_DIM, :]
        o_w = _dot(v_t.astype(BF16), p_w[:wb].astype(BF16)) + _dot_tn(new_ref[0, :, v_slab], p_w[wb:])[mine]
        heads += _gate_mix(g_t, kv, ((0, o_cmp[kv]), (2, o_w)))
    part_ref[0] = jnp.concatenate(heads, axis=0)


def _attn_sample(q_t, g_t, kc, vc_t, win, new, *, past, n_blk):
    b = q_t.shape[0]
    half = kc.shape[1] // 2
    n_top = min(TOP_N, n_blk)
    blk3 = lambda a: pl.BlockSpec((1,) + a.shape[1:], lambda bi: (bi, 0, 0))
    return pl.pallas_call(
        functools.partial(_attn_sample_kernel, past=past, n_blk=n_blk, half=half),
        grid=(b,),
        in_specs=[blk3(q_t), blk3(g_t), blk3(kc), blk3(vc_t), blk3(win), blk3(new)],
        out_specs=[pl.BlockSpec((1, N_Q, TQ_SAMPLE), lambda bi: (bi, 0, 0)),
                   pl.BlockSpec((1, N_KV_HEADS, n_top, TQ_SAMPLE), lambda bi: (bi, 0, 0, 0))],
        out_shape=[jax.ShapeDtypeStruct((b, N_Q, TQ_SAMPLE), F32),
                   jax.ShapeDtypeStruct((b, N_KV_HEADS, n_top, TQ_SAMPLE), jnp.int32)],
        compiler_params=_params("parallel"),
        name="attn_sample",
    )(q_t, g_t, kc, vc_t, win, new)


Q_ROWS = 16


def _sel_sample_kernel(pt_ref, bid_ref, q_ref, new_ref, cache_ref, o_ref, slab_buf, sem, *,
                       past, n_past_blk, n_top):
    n_slab = N_KV_HEADS * n_top
    n_t = pl.num_programs(1)
    step = pl.program_id(0) * n_t + pl.program_id(1)
    n_steps = pl.num_programs(0) * n_t
    blocks_per_page = PAGE_SIZE // L_SEL

    def slab_copies(at_step):
        slot = at_step % 2
        copies = []
        for i in range(n_slab):
            past_blk = jnp.minimum(bid_ref[at_step * n_slab + i], n_past_blk - 1)
            page = pt_ref[at_step // n_t, past_blk // blocks_per_page]
            copies.append(pltpu.make_async_copy(cache_ref.at[page, :, i // n_top], slab_buf.at[slot, i], sem.at[slot]))
        return copies

    @pl.when(step == 0)
    def _():
        for copy in slab_copies(step):
            copy.start()

    @pl.when(step + 1 < n_steps)
    def _():
        for copy in slab_copies(step + 1):
            copy.start()

    for copy in slab_copies(step):
        copy.wait()

    slot = step % 2
    t_pos = past + pl.program_id(1)
    lane = lax.broadcasted_iota(jnp.int32, (1, PAGE_SIZE), 1)
    for kv in range(N_KV_HEADS):
        k_t, v_t, keep = [], [], []
        for j in range(n_top):
            bid = bid_ref[step * n_slab + kv * n_top + j]
            from_past = bid < n_past_blk
            first_lane = jnp.where(from_past, (bid % blocks_per_page) * L_SEL, 0)
            k_t.append(jnp.where(from_past, slab_buf[slot, kv * n_top + j, 0], new_ref[0, kv]).astype(BF16))
            v_t.append(jnp.where(from_past, slab_buf[slot, kv * n_top + j, 1], new_ref[1, kv]).astype(BF16))
            in_blk = lane - first_lane
            keep.append((in_blk >= 0) & (in_blk < L_SEL) & (bid * L_SEL + in_blk <= t_pos))
        s = jnp.where(jnp.concatenate(keep, axis=1), _dot(q_ref[kv], jnp.concatenate(k_t, axis=1)), NEG)
        e = jnp.exp2(s - jnp.max(s, axis=1, keepdims=True))
        p = e / jnp.sum(e, axis=1, keepdims=True)
        o_ref[kv] = _dot_nt(p.astype(BF16), jnp.concatenate(v_t, axis=1))


def _sel_sample(page_table, blk_ids, q, cache, new_slabs, *, past, n_past_blk, n_top):
    b, t, n_kv = q.shape[:3]
    return pl.pallas_call(
        functools.partial(_sel_sample_kernel, past=past, n_past_blk=n_past_blk, n_top=n_top),
        grid_spec=pltpu.PrefetchScalarGridSpec(
            num_scalar_prefetch=2,
            grid=(b, t),
            in_specs=[pl.BlockSpec((None, None, n_kv, Q_ROWS, HEAD_DIM), lambda bi, ti, pt, bid: (bi, ti, 0, 0, 0)),
                      pl.BlockSpec((None, 2, n_kv, HEAD_DIM, PAGE_SIZE), lambda bi, ti, pt, bid: (bi, 0, 0, 0, 0)),
                      pl.BlockSpec(memory_space=pl.ANY)],
            out_specs=pl.BlockSpec((None, None, n_kv, Q_ROWS, HEAD_DIM), lambda bi, ti, pt, bid: (bi, ti, 0, 0, 0)),
            scratch_shapes=[pltpu.VMEM((2, n_kv * n_top, 2, HEAD_DIM, PAGE_SIZE), F32), pltpu.SemaphoreType.DMA((2,))],
        ),
        out_shape=jax.ShapeDtypeStruct((b, t, n_kv, Q_ROWS, HEAD_DIM), F32),
        compiler_params=_params("arbitrary", "arbitrary"),
        name="sel_sample",
    )(page_table, blk_ids, q, new_slabs, cache)


def _out_sample_kernel(x_ref, part_ref, os_ref, gs_ref, wout_ref, o_ref):
    o = (part_ref[...] + gs_ref[...] * os_ref[...]).astype(BF16)
    o_ref[...] = x_ref[...] + _dot(o, wout_ref[...])


def _out_sample(x, part, o_s, g_s, w_out):
    return pl.pallas_call(
        _out_sample_kernel,
        out_shape=jax.ShapeDtypeStruct(x.shape, F32),
        compiler_params=pltpu.CompilerParams(vmem_limit_bytes=VMEM_LIMIT),
        name="out_sample",
    )(x, part, o_s, g_s, w_out)


def kernel(x_prompt, x_sample, state_conv, cache_cmp_kv, cache_sel_kv, cache_win_kv, page_table, norm_ffa, w_ffa_gu, w_ffa_down, norm_mix, norm_ffb, w_ffb_gu, w_ffb_down, w_conv_in, w_conv, w_conv_out, w_nsa_in, pe_cmp, w_cmp_k1, w_cmp_k2, w_cmp_v1, w_cmp_v2, w_nsa_out, norm_final):
    bp, tp, _ = x_prompt.shape
    bs, ts, _ = x_sample.shape
    n_p, n_s = bp * tp, bs * ts
    xp = x_prompt.reshape(n_p, D_MODEL)
    xs = x_sample.reshape(n_s, D_MODEL)

    def ffn_pair(xp, xs, g, w_gu, w_down, layer, final=False):
        w_gu, w_down = _to_bf16(w_gu, layer), _to_bf16(w_down, layer)
        return (_ffn(xp, g, w_gu, w_down, norm_final, tm=512, final=final),
                _ffn(xs, g, w_gu, w_down, norm_final, tm=n_s, final=final))

    xp, xs = ffn_pair(xp, xs, norm_ffa[0], w_ffa_gu, w_ffa_down, 0)
    w_in, w_out = _to_bf16(w_conv_in, 0), _to_bf16(w_conv_out, 0)
    xp3, conv_p = _conv_prompt(xp.reshape(bp, tp, D_MODEL), norm_mix[0], w_in, w_conv[0], w_out)
    xs3, conv_s = _conv_sample(xs.reshape(bs, ts, D_MODEL), state_conv[0], norm_mix[0], w_in, w_conv[0], w_out)
    xp, xs = ffn_pair(xp3.reshape(n_p, D_MODEL), xs3.reshape(n_s, D_MODEL), norm_ffb[0], w_ffb_gu, w_ffb_down, 0)

    xp, xs = ffn_pair(xp, xs, norm_ffa[1], w_ffa_gu, w_ffa_down, 1)
    proj_w = _nsa_weights(w_nsa_in[0])
    cmp_w = _compress_weights(pe_cmp[0], w_cmp_k1[0], w_cmp_k2[0], w_cmp_v1[0], w_cmp_v2[0])
    w_out = _to_bf16(w_nsa_out, 0)
    kv_shape = (2, N_KV_HEADS, HEAD_DIM)
    row_w = 2 * KV_W

    q_t, g_t, kvc_t, kvs_t, kvw_t, ks, kw, vs_t, vw_t = _nsa_project_prompt(xp, norm_mix[1], proj_w, seq_len=tp, tm=512)
    n_cmp = tp // L_CMP
    kc, vc_t = _compress_seq(kvc_t, cmp_w)
    kc = kc.reshape(bp, n_cmp, KV_W)
    vc_t = vc_t.reshape(KV_W, bp, n_cmp).transpose(1, 0, 2)
    seq = lambda a: a.reshape(bp, tp, a.shape[-1])
    xp3 = _attn_prompt(seq(xp), q_t, g_t, kc, vc_t, seq(ks), vs_t, seq(kw), vw_t, w_out)
    by_row = lambda a: a.reshape((bp,) + kv_shape + (a.shape[-1],)).transpose(0, 4, 1, 2, 3)[None]
    cmp_p = by_row(kvc_t)
    sel_p = by_row(kvs_t)
    win_p = by_row(kvw_t[:, :, tp - min(WINDOW, tp):])

    q_t, g_t, kvc, kvs, kvw = _nsa_project_sample(xs, norm_mix[1], proj_w)
    n_pages = page_table.shape[1]
    past = n_pages * PAGE_SIZE
    n_new = -(-ts // L_SEL) * L_SEL
    n_past_blk = past // L_SEL
    n_blk = n_past_blk + n_new // L_SEL
    n_top = min(TOP_N, n_blk)
    lanes = lambda a: jnp.pad(a.reshape(a.shape[0], bs, ts).transpose(1, 0, 2), ((0, 0), (0, 0), (0, TQ_SAMPLE - ts)))
    by_channel = lambda c: c.transpose(0, 2, 3, 4, 1)
    kc_past, vct_past = _compress_paged(by_channel(cache_cmp_kv[0]).reshape(-1, row_w, PAGE_SIZE), page_table, cmp_w)
    new_c = jnp.pad(kvc.reshape(bs, ts, row_w), ((0, 0), (0, n_new - ts), (0, 0)))
    kc_new, vct_new = _compress(new_c.reshape(bs * n_new // L_CMP, L_CMP * row_w), cmp_w, tm=bs * n_new // L_CMP)
    half = -(-n_blk // LANES) * LANES
    pad = 2 * half - 2 * n_blk
    kc_all = jnp.pad(jnp.concatenate([kc_past.reshape(bs, -1, KV_W), kc_new.reshape(bs, -1, KV_W)], axis=1),
                     ((0, 0), (0, pad), (0, 0)))
    vct_all = jnp.pad(jnp.concatenate([vct_past.reshape(KV_W, bs, -1), vct_new.reshape(KV_W, bs, -1)], axis=2),
                      ((0, 0), (0, 0), (0, pad)))
    wb = cache_win_kv.shape[2]
    new_w = jnp.pad(kvw.reshape(bs, ts, row_w), ((0, 0), (0, NEW_ROWS - ts), (0, 0)))
    part, idx = _attn_sample(lanes(q_t), lanes(g_t), kc_all, vct_all.transpose(1, 0, 2),
                             by_channel(cache_win_kv[0]).reshape(bs, row_w, wb), new_w, past=past, n_blk=n_blk)
    bid = idx[..., :ts].transpose(0, 3, 1, 2)
    q5 = q_t.reshape(N_KV_HEADS, GROUP, HEAD_DIM, bs, ts).transpose(3, 4, 0, 1, 2)
    q5 = jnp.pad(q5, ((0, 0),) * 3 + ((0, Q_ROWS - GROUP), (0, 0)))
    new_s = jnp.pad(kvs.reshape((bs, ts) + kv_shape).transpose(0, 2, 3, 4, 1), ((0, 0),) * 4 + ((0, PAGE_SIZE - ts),))
    o_s = _sel_sample(page_table, bid.reshape(-1), q5, by_channel(cache_sel_kv[0]), new_s,
                      past=past, n_past_blk=n_past_blk, n_top=n_top)
    o_s = o_s[:, :, :, :GROUP].reshape(n_s, N_Q)
    g_s = jnp.repeat(g_t[N_HEADS:2 * N_HEADS].T, HEAD_DIM, axis=1)
    part = part[:, :, :ts].transpose(0, 2, 1).reshape(n_s, N_Q)
    xs = _out_sample(xs, part, o_s, g_s, w_out)
    cmp_s = kvc.reshape((1, bs, ts) + kv_shape)
    sel_s = kvs.reshape((1, bs, ts) + kv_shape)
    win_s = jnp.concatenate([cache_win_kv[0], kvw.reshape((bs, ts) + kv_shape)], axis=1)[None, :, ts:]

    xp, xs = ffn_pair(xp3.reshape(n_p, D_MODEL), xs, norm_ffb[1], w_ffb_gu, w_ffb_down, 1, final=True)
    return (xp.reshape(bp, tp, D_MODEL), xs.reshape(bs, ts, D_MODEL),
            conv_p[None], conv_s[None], cmp_p, cmp_s, sel_p, sel_s, win_p, win_s)
```

```python
import functools

import jax
import jax.numpy as jnp
from jax import lax
from jax.experimental import pallas as pl
from jax.experimental.pallas import tpu as pltpu

D_MODEL = 1024
D_FF = 2816
CONV_W = 3
N_HEADS = 16
HEAD_DIM = 64
N_KV_HEADS = 4
GROUP = N_HEADS // N_KV_HEADS
KV_W = N_KV_HEADS * HEAD_DIM
L_CMP = 32
L_SEL = 64
TOP_N = 16
WINDOW = 512
PAGE_SIZE = 128
NORM_EPS = 1e-6
FORCE_BONUS = 1e4
NEG = -1e30

LANES = 128
VMEM_LIMIT = 56 * 1024 * 1024

F32 = jnp.float32
BF16 = jnp.bfloat16


def _params(*sem):
    return pltpu.CompilerParams(dimension_semantics=sem, vmem_limit_bytes=VMEM_LIMIT)


def _rms(x, g):
    return x * lax.rsqrt(jnp.mean(x * x, axis=-1, keepdims=True) + NORM_EPS) * g


def _dot(a, b):
    return jnp.dot(a, b, preferred_element_type=F32)


def _dot_nt(a, b):
    return lax.dot_general(a, b, (((1,), (1,)), ((), ())), preferred_element_type=F32)


def _dot_tn(a, b):
    return lax.dot_general(a, b, (((0,), (0,)), ((), ())), preferred_element_type=F32)


def _cast_kernel(w_ref, o_ref):
    o_ref[...] = w_ref[...].astype(BF16)


def _to_bf16(w, layer, *, steps=4):
    _, r, c = w.shape
    tr = r // steps
    return pl.pallas_call(
        _cast_kernel,
        grid=(steps,),
        in_specs=[pl.BlockSpec((None, tr, c), lambda i: (layer, i, 0))],
        out_specs=pl.BlockSpec((tr, c), lambda i: (i, 0)),
        out_shape=jax.ShapeDtypeStruct((r, c), BF16),
        compiler_params=_params("parallel"),
        name="to_bf16",
    )(w)


def _ffn_kernel(x_ref, g_ref, wg_ref, wu_ref, wd_ref, gf_ref, o_ref, *, final):
    x = x_ref[...]
    xn = _rms(x, g_ref[...]).astype(BF16)
    gate = _dot(xn, wg_ref[...])
    up = _dot(xn, wu_ref[...])
    act = (gate * jax.nn.sigmoid(gate) * up).astype(BF16)
    y = x + 0.5 * _dot(act, wd_ref[...])
    o_ref[...] = _rms(y, gf_ref[...]) if final else y


def _resident(shape, index):
    return pl.BlockSpec(shape, index, pipeline_mode=pl.Buffered(1))


def _ffn(x, g, w_gu, w_down, g_final, *, tm, final=False):
    n = x.shape[0]
    return pl.pallas_call(
        functools.partial(_ffn_kernel, final=final),
        grid=(n // tm,),
        in_specs=[
            pl.BlockSpec((tm, D_MODEL), lambda i: (i, 0)),
            pl.BlockSpec((1, D_MODEL), lambda i: (0, 0)),
            _resident((D_MODEL, D_FF), lambda i: (0, 0)),
            _resident((D_MODEL, D_FF), lambda i: (0, 1)),
            _resident((D_FF, D_MODEL), lambda i: (0, 0)),
            pl.BlockSpec((1, D_MODEL), lambda i: (0, 0)),
        ],
        out_specs=pl.BlockSpec((tm, D_MODEL), lambda i: (i, 0)),
        out_shape=jax.ShapeDtypeStruct((n, D_MODEL), F32),
        compiler_params=_params("parallel"),
        name="ffn_final" if final else "ffn",
    )(x, g.reshape(1, D_MODEL), w_gu, w_gu, w_down, g_final.reshape(1, D_MODEL))


CARRY = 8


def _conv_prompt_kernel(x_ref, g_ref, win_ref, wc_ref, wout_ref, o_ref, st_ref, uext_ref, *, tm):
    @pl.when(pl.program_id(1) == 0)
    def _():
        uext_ref[0:CARRY, :] = jnp.zeros((CARRY, D_MODEL), F32)

    x = x_ref[0]
    h = _rms(x, g_ref[...]).astype(BF16)
    p = _dot(h, win_ref[...])
    bg = p[:, :D_MODEL]
    u = p[:, D_MODEL:2 * D_MODEL] * p[:, 2 * D_MODEL:]
    uext_ref[CARRY:CARRY + tm, :] = u
    wc = wc_ref[...]
    conv = (wc[0:1] * uext_ref[CARRY - 2:CARRY - 2 + tm, :]
            + wc[1:2] * uext_ref[CARRY - 1:CARRY - 1 + tm, :]
            + wc[2:3] * u)
    y = _dot((bg * conv).astype(BF16), wout_ref[...])
    o_ref[0] = x + y
    st_ref[0] = uext_ref[CARRY + tm - 2:CARRY + tm, :]
    uext_ref[0:CARRY, :] = uext_ref[tm:tm + CARRY, :]


def _conv_prompt(x, g, w_in, w_conv, w_out, *, tm=512):
    b, t, _ = x.shape
    return pl.pallas_call(
        functools.partial(_conv_prompt_kernel, tm=tm),
        grid=(b, t // tm),
        in_specs=[
            pl.BlockSpec((1, tm, D_MODEL), lambda bi, ti: (bi, ti, 0)),
            pl.BlockSpec((1, D_MODEL), lambda bi, ti: (0, 0)),
            pl.BlockSpec((D_MODEL, 3 * D_MODEL), lambda bi, ti: (0, 0)),
            pl.BlockSpec((CONV_W, D_MODEL), lambda bi, ti: (0, 0)),
            pl.BlockSpec((D_MODEL, D_MODEL), lambda bi, ti: (0, 0)),
        ],
        out_specs=[
            pl.BlockSpec((1, tm, D_MODEL), lambda bi, ti: (bi, ti, 0)),
            pl.BlockSpec((1, CONV_W - 1, D_MODEL), lambda bi, ti: (bi, 0, 0)),
        ],
        out_shape=[jax.ShapeDtypeStruct((b, t, D_MODEL), F32),
                   jax.ShapeDtypeStruct((b, CONV_W - 1, D_MODEL), F32)],
        scratch_shapes=[pltpu.VMEM((CARRY + tm, D_MODEL), F32)],
        compiler_params=_params("parallel", "arbitrary"),
        name="conv_prompt",
    )(x, g.reshape(1, D_MODEL), w_in, w_conv, w_out)


def _conv_sample_kernel(x_ref, g_ref, win_ref, wc_ref, wout_ref, s1_ref, s2_ref, o_ref, u_ref, *, t_len):
    x = x_ref[...]
    h = _rms(x, g_ref[...]).astype(BF16)
    p = _dot(h, win_ref[...])
    bg = p[:, :D_MODEL]
    u = p[:, D_MODEL:2 * D_MODEL] * p[:, 2 * D_MODEL:]
    pos = lax.broadcasted_iota(jnp.int32, u.shape, 0) % t_len
    u1 = jnp.where(pos >= 1, pltpu.roll(u, 1, axis=0), s1_ref[...])
    u2 = jnp.where(pos >= 2, pltpu.roll(u, 2, axis=0), s2_ref[...])
    wc = wc_ref[...]
    conv = wc[0:1] * u2 + wc[1:2] * u1 + wc[2:3] * u
    o_ref[...] = x + _dot((bg * conv).astype(BF16), wout_ref[...])
    u_ref[...] = u


def _conv_sample(x, state, g, w_in, w_conv, w_out):
    b, t, _ = x.shape
    n = b * t
    zeros = jnp.zeros((b, t, D_MODEL), F32)
    s1 = zeros.at[:, 0].set(state[:, 1]).reshape(n, D_MODEL)
    s2 = zeros.at[:, 0].set(state[:, 0]).at[:, 1].set(state[:, 1]).reshape(n, D_MODEL)
    y, u = pl.pallas_call(
        functools.partial(_conv_sample_kernel, t_len=t),
        out_shape=[jax.ShapeDtypeStruct((n, D_MODEL), F32), jax.ShapeDtypeStruct((n, D_MODEL), F32)],
        compiler_params=pltpu.CompilerParams(vmem_limit_bytes=VMEM_LIMIT),
        name="conv_sample",
    )(x.reshape(n, D_MODEL), g.reshape(1, D_MODEL), w_in, w_conv, w_out, s1, s2)
    return y.reshape(b, t, D_MODEL), u.reshape(b, t, D_MODEL)[:, t - (CONV_W - 1):]


N_Q = N_HEADS * HEAD_DIM
Q_SCALE = HEAD_DIM ** -0.5 * 1.4426950408889634


def _proj_common(x_ref, g_ref, wqt_ref, wgt_ref, qt_ref, gt_ref):
    h = _rms(x_ref[...], g_ref[...]).astype(BF16)
    qt_ref[...] = (_dot_nt(wqt_ref[...], h) * Q_SCALE).astype(BF16)
    gt_ref[...] = jax.nn.sigmoid(_dot_nt(wgt_ref[...], h))
    return h


def _proj_prompt_kernel(x_ref, g_ref, wqt_ref, wgt_ref, wkvt_ref, wk_ref,
                        qt_ref, gt_ref, kvct_ref, kvst_ref, kvwt_ref, ks_ref, kw_ref, vst_ref, vwt_ref):
    h = _proj_common(x_ref, g_ref, wqt_ref, wgt_ref, qt_ref, gt_ref)
    kv_t = _dot_nt(wkvt_ref[...], h)
    kvct_ref[0] = kv_t[:2 * KV_W]
    kvst_ref[0] = kv_t[2 * KV_W:4 * KV_W]
    kvwt_ref[0] = kv_t[4 * KV_W:]
    vst_ref[...] = kv_t[3 * KV_W:4 * KV_W].astype(BF16)
    vwt_ref[...] = kv_t[5 * KV_W:].astype(BF16)
    k = _dot(h, wk_ref[...]).astype(BF16)
    ks_ref[...] = k[:, :KV_W]
    kw_ref[...] = k[:, KV_W:]


def _proj_sample_kernel(x_ref, g_ref, wqt_ref, wgt_ref, wkv_ref, qt_ref, gt_ref, kvc_ref, kvs_ref, kvw_ref):
    h = _proj_common(x_ref, g_ref, wqt_ref, wgt_ref, qt_ref, gt_ref)
    kv = _dot(h, wkv_ref[...])
    kvc_ref[...] = kv[:, :2 * KV_W]
    kvs_ref[...] = kv[:, 2 * KV_W:4 * KV_W]
    kvw_ref[...] = kv[:, 4 * KV_W:]


def _nsa_weights(w_in):
    wkv = w_in[:, N_Q:N_Q + 6 * KV_W]
    wk = jnp.concatenate([wkv[:, 2 * KV_W:3 * KV_W], wkv[:, 4 * KV_W:5 * KV_W]], axis=1)
    wgt = jnp.pad(w_in[:, N_Q + 6 * KV_W:], ((0, 0), (0, LANES - 3 * N_HEADS))).T
    return dict(wqt=w_in[:, :N_Q].T.astype(BF16), wgt=wgt.astype(BF16), wkv=wkv.astype(BF16),
                wkvt=wkv.T.astype(BF16), wk=wk.astype(BF16))


def _nsa_project_prompt(x, g, w, *, seq_len, tm):
    n = x.shape[0]
    per_seq = seq_len // tm
    row = lambda w_: pl.BlockSpec((tm, w_), lambda i: (i, 0))
    col = lambda h: pl.BlockSpec((h, tm), lambda i: (0, i))
    seq_t = pl.BlockSpec((1, 2 * KV_W, tm), lambda i: (i // per_seq, 0, i % per_seq))
    full = lambda a: pl.BlockSpec(a.shape, lambda i: (0, 0))
    weights = [w["wqt"], w["wgt"], w["wkvt"], w["wk"]]
    return pl.pallas_call(
        _proj_prompt_kernel,
        grid=(n // tm,),
        in_specs=[row(D_MODEL), pl.BlockSpec((1, D_MODEL), lambda i: (0, 0))] + [full(a) for a in weights],
        out_specs=[col(N_Q), col(LANES), seq_t, seq_t, seq_t, row(KV_W), row(KV_W), col(KV_W), col(KV_W)],
        out_shape=[jax.ShapeDtypeStruct((N_Q, n), BF16), jax.ShapeDtypeStruct((LANES, n), F32)]
        + [jax.ShapeDtypeStruct((n // seq_len, 2 * KV_W, seq_len), F32)] * 3
        + [jax.ShapeDtypeStruct((n, KV_W), BF16)] * 2
        + [jax.ShapeDtypeStruct((KV_W, n), BF16)] * 2,
        compiler_params=_params("parallel"),
        name="nsa_project_prompt",
    )(x, g.reshape(1, D_MODEL), *weights)


def _nsa_project_sample(x, g, w):
    n = x.shape[0]
    return pl.pallas_call(
        _proj_sample_kernel,
        out_shape=[jax.ShapeDtypeStruct((N_Q, n), BF16), jax.ShapeDtypeStruct((LANES, n), F32)]
        + [jax.ShapeDtypeStruct((n, 2 * KV_W), F32)] * 3,
        compiler_params=pltpu.CompilerParams(vmem_limit_bytes=VMEM_LIMIT),
        name="nsa_project_sample",
    )(x, g.reshape(1, D_MODEL), w["wqt"], w["wgt"], w["wkv"])


def _compress_out(acck, accv, w2k_ref, w2vt_ref, kc_ref, vct_ref):
    kc_ref[...] = _dot(jax.nn.gelu(acck).astype(BF16), w2k_ref[...]).astype(BF16)
    vct_ref[...] = _dot_nt(w2vt_ref[...], jax.nn.gelu(accv).astype(BF16)).astype(BF16)


def _compress_kernel(x_ref, pe_ref, w1k_ref, w1v_ref, w2k_ref, w2vt_ref, kc_ref, vct_ref, acck_ref, accv_ref):
    l = pl.program_id(1)

    @pl.when(l == 0)
    def _():
        acck_ref[...] = jnp.zeros_like(acck_ref)
        accv_ref[...] = jnp.zeros_like(accv_ref)

    xb = x_ref[...] + pe_ref[...]
    acck_ref[...] += _dot(xb[:, :KV_W].astype(BF16), w1k_ref[...])
    accv_ref[...] += _dot(xb[:, KV_W:].astype(BF16), w1v_ref[...])

    @pl.when(l == L_CMP - 1)
    def _():
        _compress_out(acck_ref[...], accv_ref[...], w2k_ref, w2vt_ref, kc_ref, vct_ref)


PAGES_PER_STEP = 32
BLOCKS_PER_PAGE = PAGE_SIZE // L_CMP


def _compress_weights(pe, w1k, w2k, w1v, w2v):
    eye = jnp.eye(N_KV_HEADS, dtype=F32)
    bd1 = lambda w1: jax.vmap(lambda w: jnp.kron(eye, w))(w1).astype(BF16)
    pe_row = jnp.tile(pe, (1, 2 * N_KV_HEADS))
    r = jnp.arange(2 * PAGE_SIZE)
    src = (r % 8 // BLOCKS_PER_PAGE) * PAGE_SIZE + (r % BLOCKS_PER_PAGE) * L_CMP + r // 8
    perm = (src[:, None] == r[None, :]).astype(BF16)
    return dict(pe_row=pe_row.reshape(L_CMP, 1, 2 * KV_W), pe_page_t=jnp.tile(pe_row, (BLOCKS_PER_PAGE, 1)).T, perm=perm,
                w1k=bd1(w1k), w1v=bd1(w1v), w2k=jnp.kron(eye, w2k).astype(BF16), w2vt=jnp.kron(eye, w2v).T.astype(BF16))


def _compress(x, w, *, tm):
    m = x.shape[0]
    tm = min(tm, m)
    return pl.pallas_call(
        _compress_kernel,
        grid=(m // tm, L_CMP),
        in_specs=[
            pl.BlockSpec((tm, 2 * KV_W), lambda i, l: (i, l)),
            pl.BlockSpec((None, 1, 2 * KV_W), lambda i, l: (l, 0, 0)),
            pl.BlockSpec((None, KV_W, KV_W), lambda i, l: (l, 0, 0)),
            pl.BlockSpec((None, KV_W, KV_W), lambda i, l: (l, 0, 0)),
            pl.BlockSpec((KV_W, KV_W), lambda i, l: (0, 0)),
            pl.BlockSpec((KV_W, KV_W), lambda i, l: (0, 0)),
        ],
        out_specs=[pl.BlockSpec((tm, KV_W), lambda i, l: (i, 0)), pl.BlockSpec((KV_W, tm), lambda i, l: (0, i))],
        out_shape=[jax.ShapeDtypeStruct((m, KV_W), BF16), jax.ShapeDtypeStruct((KV_W, m), BF16)],
        scratch_shapes=[pltpu.VMEM((tm, KV_W), F32)] * 2,
        compiler_params=_params("parallel", "arbitrary"),
        name="compress",
    )(x, w["pe_row"], w["w1k"], w["w1v"], w["w2k"], w["w2vt"])


def _compress_pages_body(refs, n_pages):
    pages = refs[:n_pages]
    pe_ref, perm_ref, w1k_ref, w1v_ref, w2k_ref, w2vt_ref, kc_ref, vct_ref = refs[n_pages:]
    pe = pe_ref[...]
    perm = perm_ref[...]
    by_l = []
    for p in range(0, n_pages, 2):
        pair_t = jnp.concatenate([(pages[p][...] + pe).astype(BF16), (pages[p + 1][...] + pe).astype(BF16)], axis=1)
        by_l.append(_dot_nt(perm, pair_t))
    acck = jnp.zeros((n_pages * BLOCKS_PER_PAGE, KV_W), F32)
    accv = jnp.zeros((n_pages * BLOCKS_PER_PAGE, KV_W), F32)
    for l in range(L_CMP):
        xl = jnp.concatenate([y[l * 8:(l + 1) * 8] for y in by_l], axis=0).astype(BF16)
        acck = acck + _dot(xl[:, :KV_W], w1k_ref[l])
        accv = accv + _dot(xl[:, KV_W:], w1v_ref[l])
    _compress_out(acck, accv, w2k_ref, w2vt_ref, kc_ref, vct_ref)


def _compress_paged_kernel(pt_ref, *refs, n_pages):
    del pt_ref
    _compress_pages_body(refs, n_pages)


def _compress_seq_kernel(*refs, n_pages):
    _compress_pages_body(refs, n_pages)


def _compress_consts(w):
    return [w["pe_page_t"], w["perm"], w["w1k"], w["w1v"], w["w2k"], w["w2vt"]]


def _compress_paged(cache, page_table, w):
    b, n_pages = page_table.shape
    pps = min(PAGES_PER_STEP, n_pages)
    steps = n_pages // pps
    m_step = pps * BLOCKS_PER_PAGE
    m = b * n_pages * BLOCKS_PER_PAGE
    page = lambda j: pl.BlockSpec((None, 2 * KV_W, PAGE_SIZE), lambda bi, s, pt: (pt[bi, s * pps + j], 0, 0))
    full = lambda a: pl.BlockSpec(a.shape, lambda bi, s, pt: (0,) * a.ndim)
    consts = _compress_consts(w)
    return pl.pallas_call(
        functools.partial(_compress_paged_kernel, n_pages=pps),
        grid_spec=pltpu.PrefetchScalarGridSpec(
            num_scalar_prefetch=1,
            grid=(b, steps),
            in_specs=[page(j) for j in range(pps)] + [full(a) for a in consts],
            out_specs=[pl.BlockSpec((m_step, KV_W), lambda bi, s, pt: (bi * steps + s, 0)),
                       pl.BlockSpec((KV_W, m_step), lambda bi, s, pt: (0, bi * steps + s))],
        ),
        out_shape=[jax.ShapeDtypeStruct((m, KV_W), BF16), jax.ShapeDtypeStruct((KV_W, m), BF16)],
        compiler_params=_params("parallel", "arbitrary"),
        name="compress_paged",
    )(page_table, *([cache] * pps), *consts)


def _compress_seq(rows_t, w):
    b, _, t = rows_t.shape
    per_seq = t // PAGE_SIZE
    n_pages = b * per_seq
    pps = min(PAGES_PER_STEP, n_pages)
    m_step = pps * BLOCKS_PER_PAGE
    m = n_pages * BLOCKS_PER_PAGE
    page = lambda j: pl.BlockSpec((None, 2 * KV_W, PAGE_SIZE),
                                  lambda s: ((s * pps + j) // per_seq, 0, (s * pps + j) % per_seq))
    full = lambda a: pl.BlockSpec(a.shape, lambda s: (0,) * a.ndim)
    consts = _compress_consts(w)
    return pl.pallas_call(
        functools.partial(_compress_seq_kernel, n_pages=pps),
        grid=(n_pages // pps,),
        in_specs=[page(j) for j in range(pps)] + [full(a) for a in consts],
        out_specs=[pl.BlockSpec((m_step, KV_W), lambda s: (s, 0)), pl.BlockSpec((KV_W, m_step), lambda s: (0, s))],
        out_shape=[jax.ShapeDtypeStruct((m, KV_W), BF16), jax.ShapeDtypeStruct((KV_W, m), BF16)],
        compiler_params=_params("parallel"),
        name="compress_seq",
    )(*([rows_t] * pps), *consts)


TQ_PROMPT = 256
TQ_SAMPLE = 128
KCH = 512
ONES_ROWS = 16
BIG = 3.0e38


def _head_queries(q_t, kv):
    zero = jnp.zeros((HEAD_DIM, q_t.shape[1]), BF16)
    cols = []
    for g in range(GROUP):
        h = kv * GROUP + g
        qg = q_t[h * HEAD_DIM:(h + 1) * HEAD_DIM, :]
        cols.append(jnp.concatenate([qg, zero] if kv % 2 == 0 else [zero, qg], axis=0))
    return jnp.concatenate(cols, axis=1)


def _slab(kv):
    return slice((kv // 2) * LANES, (kv // 2 + 1) * LANES)


def _head(kv):
    return slice(kv * HEAD_DIM, (kv + 1) * HEAD_DIM)


def _cmp_branch(kc, vc_t, x, keep):
    s = jnp.where(keep, _dot(kc, x), NEG)
    e = jnp.exp2(s - jnp.max(s, axis=0, keepdims=True))
    l = jnp.sum(e, axis=0, keepdims=True)
    e = jnp.where(keep, e, 0.0)
    return _dot(vc_t, e.astype(BF16)) / l, e / l


def _group_sum(p):
    tq = p.shape[1] // GROUP
    out = p[:, 0:tq]
    for g in range(1, GROUP):
        out = out + p[:, g * tq:(g + 1) * tq]
    return out


def _pair_sum(p):
    n = p.shape[0] // 2
    pair = (lax.broadcasted_iota(jnp.int32, (n, 2 * n), 1) // 2 == lax.broadcasted_iota(jnp.int32, (n, 2 * n), 0))
    return jnp.dot(pair.astype(F32), p, precision=lax.Precision.HIGHEST, preferred_element_type=F32)


def _with_ones(v_t):
    return jnp.concatenate([v_t, jnp.ones((ONES_ROWS, v_t.shape[1]), BF16)], axis=0)


def _topn_rows(score, n):
    nb = score.shape[0]
    row = lax.broadcasted_iota(jnp.int32, score.shape, 0).astype(F32)
    picked = jnp.zeros(score.shape, F32)
    ids = []
    for _ in range(n):
        top = jnp.max(score, axis=0, keepdims=True)
        first = jnp.min(jnp.where(score == top, row, float(nb)), axis=0, keepdims=True)
        hit = row == first
        picked = jnp.where(hit, 1.0, picked)
        score = jnp.where(hit, -jnp.inf, score)
        ids.append(first)
    return picked, ids


def _masked_scores(s, cap):
    tq = cap.shape[1]
    s = jnp.concatenate([jnp.minimum(s[:, g * tq:(g + 1) * tq], cap) for g in range(GROUP)], axis=1)
    return s, jnp.max(s, axis=0, keepdims=True)


def _softmax_pv(s, top, v_t):
    o = _dot(_with_ones(v_t), jnp.exp2(s - top).astype(BF16))
    return o[:HEAD_DIM] / o[HEAD_DIM:HEAD_DIM + 1]


def _gate_mix(g_t, kv, branches):
    tq = g_t.shape[1]
    out = []
    for g in range(GROUP):
        h = kv * GROUP + g
        c = slice(g * tq, (g + 1) * tq)
        acc = None
        for br, o in branches:
            term = g_t[br * N_HEADS + h:br * N_HEADS + h + 1, :] * o[:, c]
            acc = term if acc is None else acc + term
        out.append(acc)
    return out


def _attn_prompt_kernel(x_ref, qt_ref, gt_ref, kc_ref, vct_ref, ks_ref, vst_ref, kw_ref, *rest, t_len, tq):
    win_blocks = WINDOW // tq + 1
    cols = GROUP * tq
    vwt_refs = rest[:win_blocks]
    wout_ref, o_ref, cap_ref, m_ref, acc_ref = rest[win_blocks:]
    t0 = pl.program_id(1) * tq
    n_sel = t_len // L_SEL
    n_cmp = t_len // L_CMP
    t_tok = t0 + lax.broadcasted_iota(jnp.int32, (1, tq), 1)
    t_col = t0 + lax.broadcasted_iota(jnp.int32, (1, cols), 1) % tq
    t_heads = t0 + lax.broadcasted_iota(jnp.int32, (1, N_KV_HEADS * tq), 1) % tq
    r_c = lax.broadcasted_iota(jnp.int32, (n_cmp, 1), 0)
    keep_c = (r_c + 1) * L_CMP - 1 <= t_col
    blk = lax.broadcasted_iota(jnp.int32, (n_sel, N_KV_HEADS * tq), 0)
    cur = t_heads // L_SEL
    forced = (blk == 0) | (blk == cur) | (blk == cur - 1)
    valid = blk <= cur
    key_in_blk = lax.broadcasted_iota(jnp.int32, (L_SEL, 1), 0)
    win_pos = t0 - WINDOW + lax.broadcasted_iota(jnp.int32, (win_blocks * tq, 1), 0)
    dist = t_tok - win_pos
    cap_w = jnp.where((dist >= 0) & (dist < WINDOW) & (win_pos >= 0), BIG, NEG)
    n_chunks = t_len // KCH
    last_chunk = (t0 + tq - 1) // KCH
    q_t = qt_ref[...]
    g_t = gt_ref[...]

    o_cmp, imp = [], []
    for kv in range(N_KV_HEADS):
        o_c, p = _cmp_branch(kc_ref[0, :, _slab(kv)], vct_ref[0, _head(kv), :], _head_queries(q_t, kv), keep_c)
        o_cmp.append(o_c)
        imp.append(_group_sum(p))
    score = jnp.where(forced, FORCE_BONUS, jnp.where(valid, _pair_sum(jnp.concatenate(imp, axis=1)), -FORCE_BONUS))
    picked, _ = _topn_rows(score, min(TOP_N, n_sel))
    sel_cap = jnp.where(valid & (picked > 0.5), BIG, NEG)
    for kv in range(N_KV_HEADS):
        for j in range(n_sel):
            causal = j * L_SEL + key_in_blk <= t_tok
            cap_ref[kv, j * L_SEL:(j + 1) * L_SEL, :] = jnp.where(
                causal, jnp.broadcast_to(sel_cap[j:j + 1, kv * tq:(kv + 1) * tq], (L_SEL, tq)), NEG)

    m_ref[...] = jnp.full(m_ref.shape, NEG, F32)
    acc_ref[...] = jnp.zeros_like(acc_ref)
    for c in range(n_chunks):
        @pl.when(c <= last_chunk)
        def _():
            keys = slice(c * KCH, (c + 1) * KCH)
            scores, maxes = [], []
            for kv in range(N_KV_HEADS):
                s = _dot(ks_ref[0, keys, _slab(kv)], _head_queries(q_t, kv))
                cap = cap_ref[kv, keys, :]
                s = jnp.concatenate([jnp.minimum(s[:, g * tq:(g + 1) * tq], cap) for g in range(GROUP)], axis=1)
                scores.append(s)
                maxes.append(jnp.maximum(m_ref[kv], jnp.max(s, axis=0, keepdims=True)))
            for kv in range(N_KV_HEADS):
                e = jnp.exp2(scores[kv] - maxes[kv]).astype(BF16)
                acc_ref[kv] = (jnp.exp2(m_ref[kv] - maxes[kv]) * acc_ref[kv]
                               + _dot(_with_ones(vst_ref[_head(kv), keys]), e))
                m_ref[kv] = maxes[kv]

    win = []
    for kv in range(N_KV_HEADS):
        x = _head_queries(q_t, kv)
        s_w = []
        for j in range(win_blocks):
            start = pl.multiple_of(jnp.maximum(t0 - WINDOW + j * tq, 0), tq)
            s_w.append(_dot(kw_ref[0, pl.ds(start, tq), _slab(kv)], x))
        win.append(_masked_scores(jnp.concatenate(s_w, axis=0), cap_w))
    heads = []
    for kv in range(N_KV_HEADS):
        o_w = _softmax_pv(*win[kv], jnp.concatenate([r[_head(kv), :] for r in vwt_refs], axis=1))
        acc = acc_ref[kv]
        o_s = acc[:HEAD_DIM] / acc[HEAD_DIM:HEAD_DIM + 1]
        heads += _gate_mix(g_t, kv, ((0, o_cmp[kv]), (1, o_s), (2, o_w)))
    o = jnp.concatenate(heads, axis=0).T.astype(BF16)
    o_ref[0] = x_ref[0] + _dot(o, wout_ref[...])


def _attn_prompt(x, q_t, g_t, kc, vc_t, ks, vs_t, kw, vw_t, w_out, *, tq=TQ_PROMPT):
    b, t, _ = x.shape
    nt = t // tq
    n_cmp = t // L_CMP
    win_blocks = WINDOW // tq + 1
    cols = GROUP * tq
    tile = lambda w: pl.BlockSpec((1, tq, w), lambda bi, i: (bi, i, 0))
    tile_t = lambda h: pl.BlockSpec((h, tq), lambda bi, i: (0, bi * nt + i))
    win_t = lambda j: pl.BlockSpec((KV_W, tq), lambda bi, i: (0, bi * nt + jnp.maximum(i - (win_blocks - 1) + j, 0)))
    return pl.pallas_call(
        functools.partial(_attn_prompt_kernel, t_len=t, tq=tq),
        grid=(b, nt),
        in_specs=[tile(D_MODEL), tile_t(N_Q), tile_t(LANES),
                  pl.BlockSpec((1, n_cmp, KV_W), lambda bi, i: (bi, 0, 0)),
                  pl.BlockSpec((1, KV_W, n_cmp), lambda bi, i: (bi, 0, 0)),
                  pl.BlockSpec((1, t, KV_W), lambda bi, i: (bi, 0, 0)),
                  pl.BlockSpec((KV_W, t), lambda bi, i: (0, bi)),
                  pl.BlockSpec((1, t, KV_W), lambda bi, i: (bi, 0, 0))]
        + [win_t(j) for j in range(win_blocks)]
        + [pl.BlockSpec((N_Q, D_MODEL), lambda bi, i: (0, 0))],
        out_specs=tile(D_MODEL),
        out_shape=jax.ShapeDtypeStruct((b, t, D_MODEL), F32),
        scratch_shapes=[pltpu.VMEM((N_KV_HEADS, t, tq), F32), pltpu.VMEM((N_KV_HEADS, 1, cols), F32),
                        pltpu.VMEM((N_KV_HEADS, HEAD_DIM + ONES_ROWS, cols), F32)],
        compiler_params=_params("parallel", "arbitrary"),
        name="attn_prompt",
    )(x, q_t, g_t, kc, vc_t, ks, vs_t, kw, *([vw_t] * win_blocks), w_out)


NEW_ROWS = 16


def _attn_sample_kernel(qt_ref, gt_ref, kc_ref, vct_ref, win_ref, new_ref, part_ref, idx_ref, *, past, n_blk, half):
    tq = TQ_SAMPLE
    t_tok = past + lax.broadcasted_iota(jnp.int32, (1, tq), 1)
    t_col = past + lax.broadcasted_iota(jnp.int32, (1, GROUP * tq), 1) % tq
    t_heads = past + lax.broadcasted_iota(jnp.int32, (1, N_KV_HEADS * tq), 1) % tq
    r_c = lax.broadcasted_iota(jnp.int32, (2 * half, 1), 0)
    keep_c = (r_c < 2 * n_blk) & ((r_c + 1) * L_CMP - 1 <= t_col)
    blk = lax.broadcasted_iota(jnp.int32, (half, N_KV_HEADS * tq), 0)
    cur = t_heads // L_SEL
    forced = (blk == 0) | (blk == cur) | (blk == cur - 1)
    valid = blk <= cur
    wb = win_ref.shape[2]
    win_pos = past - wb + lax.broadcasted_iota(jnp.int32, (wb + NEW_ROWS, 1), 0)
    dist = t_tok - win_pos
    keep_w = (dist >= 0) & (dist < WINDOW) & (win_pos >= 0)
    q_t = qt_ref[0]
    g_t = gt_ref[0]
    o_cmp, imp = [], []
    for kv in range(N_KV_HEADS):
        o_c, p = _cmp_branch(kc_ref[0, :, _slab(kv)], vct_ref[0, _head(kv), :], _head_queries(q_t, kv), keep_c)
        o_cmp.append(o_c)
        imp.append(_group_sum(p))
    score = jnp.where(forced, FORCE_BONUS, jnp.where(valid, _pair_sum(jnp.concatenate(imp, axis=1)), -FORCE_BONUS))
    score = jnp.where(blk < n_blk, score, -jnp.inf)
    _, ids = _topn_rows(score, min(TOP_N, n_blk))
    for j, first in enumerate(ids):
        for kv in range(N_KV_HEADS):
            idx_ref[0, kv, j:j + 1, :] = first[:, kv * tq:(kv + 1) * tq].astype(jnp.int32)

    heads = []
    for kv in range(N_KV_HEADS):
        x = _head_queries(q_t, kv)
        mine = slice((kv % 2) * HEAD_DIM, (kv % 2 + 1) * HEAD_DIM)
        v_slab = slice(KV_W + (kv // 2) * LANES, KV_W + (kv // 2 + 1) * LANES)
        s = jnp.concatenate([_dot_tn(win_ref[0, _head(kv), :], x[mine].astype(F32)),
                             _dot(new_ref[0, :, _slab(kv)].astype(BF16), x)], axis=0)
        s = jnp.concatenate([jnp.where(keep_w, s[:, g * tq:(g + 1) * tq], NEG) for g in range(GROUP)], axis=1)
        e = jnp.exp2(s - jnp.max(s, axis=0, keepdims=True))
        p_w = e / jnp.sum(e, axis=0, keepdims=True)
        v_t = win_ref[0, KV_W + kv * HEAD_DIM:KV_W + (kv + 1) * HEAD_DIM, :]
        o_w = _dot(v_t.astype(BF16), p_w[:wb].astype(BF16)) + _dot_tn(new_ref[0, :, v_slab], p_w[wb:])[mine]
        heads += _gate_mix(g_t, kv, ((0, o_cmp[kv]), (2, o_w)))
    part_ref[0] = jnp.concatenate(heads, axis=0)


def _attn_sample(q_t, g_t, kc, vc_t, win, new, *, past, n_blk):
    b = q_t.shape[0]
    half = kc.shape[1] // 2
    n_top = min(TOP_N, n_blk)
    blk3 = lambda a: pl.BlockSpec((1,) + a.shape[1:], lambda bi: (bi, 0, 0))
    return pl.pallas_call(
        functools.partial(_attn_sample_kernel, past=past, n_blk=n_blk, half=half),
        grid=(b,),
        in_specs=[blk3(q_t), blk3(g_t), blk3(kc), blk3(vc_t), blk3(win), blk3(new)],
        out_specs=[pl.BlockSpec((1, N_Q, TQ_SAMPLE), lambda bi: (bi, 0, 0)),
                   pl.BlockSpec((1, N_KV_HEADS, n_top, TQ_SAMPLE), lambda bi: (bi, 0, 0, 0))],
        out_shape=[jax.ShapeDtypeStruct((b, N_Q, TQ_SAMPLE), F32),
                   jax.ShapeDtypeStruct((b, N_KV_HEADS, n_top, TQ_SAMPLE), jnp.int32)],
        compiler_params=_params("parallel"),
        name="attn_sample",
    )(q_t, g_t, kc, vc_t, win, new)


Q_ROWS = 16


def _sel_sample_kernel(pt_ref, bid_ref, q_ref, new_ref, cache_ref, o_ref, slab_buf, sem, *,
                       past, n_past_blk, n_top):
    n_slab = N_KV_HEADS * n_top
    n_t = pl.num_programs(1)
    step = pl.program_id(0) * n_t + pl.program_id(1)
    n_steps = pl.num_programs(0) * n_t
    blocks_per_page = PAGE_SIZE // L_SEL

    def slab_copies(at_step):
        slot = at_step % 2
        copies = []
        for i in range(n_slab):
            past_blk = jnp.minimum(bid_ref[at_step * n_slab + i], n_past_blk - 1)
            page = pt_ref[at_step // n_t, past_blk // blocks_per_page]
            copies.append(pltpu.make_async_copy(cache_ref.at[page, :, i // n_top], slab_buf.at[slot, i], sem.at[slot]))
        return copies

    @pl.when(step == 0)
    def _():
        for copy in slab_copies(step):
            copy.start()

    @pl.when(step + 1 < n_steps)
    def _():
        for copy in slab_copies(step + 1):
            copy.start()

    for copy in slab_copies(step):
        copy.wait()

    slot = step % 2
    t_pos = past + pl.program_id(1)
    lane = lax.broadcasted_iota(jnp.int32, (1, PAGE_SIZE), 1)
    for kv in range(N_KV_HEADS):
        k_t, v_t, keep = [], [], []
        for j in range(n_top):
            bid = bid_ref[step * n_slab + kv * n_top + j]
            from_past = bid < n_past_blk
            first_lane = jnp.where(from_past, (bid % blocks_per_page) * L_SEL, 0)
            k_t.append(jnp.where(from_past, slab_buf[slot, kv * n_top + j, 0], new_ref[0, kv]).astype(BF16))
            v_t.append(jnp.where(from_past, slab_buf[slot, kv * n_top + j, 1], new_ref[1, kv]).astype(BF16))
            in_blk = lane - first_lane
            keep.append((in_blk >= 0) & (in_blk < L_SEL) & (bid * L_SEL + in_blk <= t_pos))
        s = jnp.where(jnp.concatenate(keep, axis=1), _dot(q_ref[kv], jnp.concatenate(k_t, axis=1)), NEG)
        e = jnp.exp2(s - jnp.max(s, axis=1, keepdims=True))
        p = e / jnp.sum(e, axis=1, keepdims=True)
        o_ref[kv] = _dot_nt(p.astype(BF16), jnp.concatenate(v_t, axis=1))


def _sel_sample(page_table, blk_ids, q, cache, new_slabs, *, past, n_past_blk, n_top):
    b, t, n_kv = q.shape[:3]
    return pl.pallas_call(
        functools.partial(_sel_sample_kernel, past=past, n_past_blk=n_past_blk, n_top=n_top),
        grid_spec=pltpu.PrefetchScalarGridSpec(
            num_scalar_prefetch=2,
            grid=(b, t),
            in_specs=[pl.BlockSpec((None, None, n_kv, Q_ROWS, HEAD_DIM), lambda bi, ti, pt, bid: (bi, ti, 0, 0, 0)),
                      pl.BlockSpec((None, 2, n_kv, HEAD_DIM, PAGE_SIZE), lambda bi, ti, pt, bid: (bi, 0, 0, 0, 0)),
                      pl.BlockSpec(memory_space=pl.ANY)],
            out_specs=pl.BlockSpec((None, None, n_kv, Q_ROWS, HEAD_DIM), lambda bi, ti, pt, bid: (bi, ti, 0, 0, 0)),
            scratch_shapes=[pltpu.VMEM((2, n_kv * n_top, 2, HEAD_DIM, PAGE_SIZE), F32), pltpu.SemaphoreType.DMA((2,))],
        ),
        out_shape=jax.ShapeDtypeStruct((b, t, n_kv, Q_ROWS, HEAD_DIM), F32),
        compiler_params=_params("arbitrary", "arbitrary"),
        name="sel_sample",
    )(page_table, blk_ids, q, new_slabs, cache)


def _out_sample_kernel(x_ref, part_ref, os_ref, gs_ref, wout_ref, o_ref):
    o = (part_ref[...] + gs_ref[...] * os_ref[...]).astype(BF16)
    o_ref[...] = x_ref[...] + _dot(o, wout_ref[...])


def _out_sample(x, part, o_s, g_s, w_out):
    return pl.pallas_call(
        _out_sample_kernel,
        out_shape=jax.ShapeDtypeStruct(x.shape, F32),
        compiler_params=pltpu.CompilerParams(vmem_limit_bytes=VMEM_LIMIT),
        name="out_sample",
    )(x, part, o_s, g_s, w_out)


def kernel(x_prompt, x_sample, state_conv, cache_cmp_kv, cache_sel_kv, cache_win_kv, page_table, norm_ffa, w_ffa_gu, w_ffa_down, norm_mix, norm_ffb, w_ffb_gu, w_ffb_down, w_conv_in, w_conv, w_conv_out, w_nsa_in, pe_cmp, w_cmp_k1, w_cmp_k2, w_cmp_v1, w_cmp_v2, w_nsa_out, norm_final):
    bp, tp, _ = x_prompt.shape
    bs, ts, _ = x_sample.shape
    n_p, n_s = bp * tp, bs * ts
    xp = x_prompt.reshape(n_p, D_MODEL)
    xs = x_sample.reshape(n_s, D_MODEL)

    def ffn_pair(xp, xs, g, w_gu, w_down, layer, final=False):
        w_gu, w_down = _to_bf16(w_gu, layer), _to_bf16(w_down, layer)
        return (_ffn(xp, g, w_gu, w_down, norm_final, tm=512, final=final),
                _ffn(xs, g, w_gu, w_down, norm_final, tm=n_s, final=final))

    xp, xs = ffn_pair(xp, xs, norm_ffa[0], w_ffa_gu, w_ffa_down, 0)
    w_in, w_out = _to_bf16(w_conv_in, 0), _to_bf16(w_conv_out, 0)
    xp3, conv_p = _conv_prompt(xp.reshape(bp, tp, D_MODEL), norm_mix[0], w_in, w_conv[0], w_out)
    xs3, conv_s = _conv_sample(xs.reshape(bs, ts, D_MODEL), state_conv[0], norm_mix[0], w_in, w_conv[0], w_out)
    xp, xs = ffn_pair(xp3.reshape(n_p, D_MODEL), xs3.reshape(n_s, D_MODEL), norm_ffb[0], w_ffb_gu, w_ffb_down, 0)

    xp, xs = ffn_pair(xp, xs, norm_ffa[1], w_ffa_gu, w_ffa_down, 1)
    proj_w = _nsa_weights(w_nsa_in[0])
    cmp_w = _compress_weights(pe_cmp[0], w_cmp_k1[0], w_cmp_k2[0], w_cmp_v1[0], w_cmp_v2[0])
    w_out = _to_bf16(w_nsa_out, 0)
    kv_shape = (2, N_KV_HEADS, HEAD_DIM)
    row_w = 2 * KV_W

    q_t, g_t, kvc_t, kvs_t, kvw_t, ks, kw, vs_t, vw_t = _nsa_project_prompt(xp, norm_mix[1], proj_w, seq_len=tp, tm=512)
    n_cmp = tp // L_CMP
    kc, vc_t = _compress_seq(kvc_t, cmp_w)
    kc = kc.reshape(bp, n_cmp, KV_W)
    vc_t = vc_t.reshape(KV_W, bp, n_cmp).transpose(1, 0, 2)
    seq = lambda a: a.reshape(bp, tp, a.shape[-1])
    xp3 = _attn_prompt(seq(xp), q_t, g_t, kc, vc_t, seq(ks), vs_t, seq(kw), vw_t, w_out)
    by_row = lambda a: a.reshape((bp,) + kv_shape + (a.shape[-1],)).transpose(0, 4, 1, 2, 3)[None]
    cmp_p = by_row(kvc_t)
    sel_p = by_row(kvs_t)
    win_p = by_row(kvw_t[:, :, tp - min(WINDOW, tp):])

    q_t, g_t, kvc, kvs, kvw = _nsa_project_sample(xs, norm_mix[1], proj_w)
    n_pages = page_table.shape[1]
    past = n_pages * PAGE_SIZE
    n_new = -(-ts // L_SEL) * L_SEL
    n_past_blk = past // L_SEL
    n_blk = n_past_blk + n_new // L_SEL
    n_top = min(TOP_N, n_blk)
    lanes = lambda a: jnp.pad(a.reshape(a.shape[0], bs, ts).transpose(1, 0, 2), ((0, 0), (0, 0), (0, TQ_SAMPLE - ts)))
    by_channel = lambda c: c.transpose(0, 2, 3, 4, 1)
    kc_past, vct_past = _compress_paged(by_channel(cache_cmp_kv[0]).reshape(-1, row_w, PAGE_SIZE), page_table, cmp_w)
    new_c = jnp.pad(kvc.reshape(bs, ts, row_w), ((0, 0), (0, n_new - ts), (0, 0)))
    kc_new, vct_new = _compress(new_c.reshape(bs * n_new // L_CMP, L_CMP * row_w), cmp_w, tm=bs * n_new // L_CMP)
    half = -(-n_blk // (LANES // 2)) * (LANES // 2)
    pad = 2 * half - 2 * n_blk
    kc_all = jnp.pad(jnp.concatenate([kc_past.reshape(bs, -1, KV_W), kc_new.reshape(bs, -1, KV_W)], axis=1),
                     ((0, 0), (0, pad), (0, 0)))
    vct_all = jnp.pad(jnp.concatenate([vct_past.reshape(KV_W, bs, -1), vct_new.reshape(KV_W, bs, -1)], axis=2),
                      ((0, 0), (0, 0), (0, pad)))
    wb = cache_win_kv.shape[2]
    new_w = jnp.pad(kvw.reshape(bs, ts, row_w), ((0, 0), (0, NEW_ROWS - ts), (0, 0)))
    part, idx = _attn_sample(lanes(q_t), lanes(g_t), kc_all, vct_all.transpose(1, 0, 2),
                             by_channel(cache_win_kv[0]).reshape(bs, row_w, wb), new_w, past=past, n_blk=n_blk)
    bid = idx[..., :ts].transpose(0, 3, 1, 2)
    q5 = q_t.reshape(N_KV_HEADS, GROUP, HEAD_DIM, bs, ts).transpose(3, 4, 0, 1, 2)
    q5 = jnp.pad(q5, ((0, 0),) * 3 + ((0, Q_ROWS - GROUP), (0, 0)))
    new_s = jnp.pad(kvs.reshape((bs, ts) + kv_shape).transpose(0, 2, 3, 4, 1), ((0, 0),) * 4 + ((0, PAGE_SIZE - ts),))
    o_s = _sel_sample(page_table, bid.reshape(-1), q5, by_channel(cache_sel_kv[0]), new_s,
                      past=past, n_past_blk=n_past_blk, n_top=n_top)
    o_s = o_s[:, :, :, :GROUP].reshape(n_s, N_Q)
    g_s = jnp.repeat(g_t[N_HEADS:2 * N_HEADS].T, HEAD_DIM, axis=1)
    part = part[:, :, :ts].transpose(0, 2, 1).reshape(n_s, N_Q)
    xs = _out_sample(xs, part, o_s, g_s, w_out)
    cmp_s = kvc.reshape((1, bs, ts) + kv_shape)
    sel_s = kvs.reshape((1, bs, ts) + kv_shape)
    win_s = jnp.concatenate([cache_win_kv[0], kvw.reshape((bs, ts) + kv_shape)], axis=1)[None, :, ts:]

    xp, xs = ffn_pair(xp3.reshape(n_p, D_MODEL), xs, norm_ffb[1], w_ffb_gu, w_ffb_down, 1, final=True)
    return (xp.reshape(bp, tp, D_MODEL), xs.reshape(bs, ts, D_MODEL),
            conv_p[None], conv_s[None], cmp_p, cmp_s, sel_p, sel_s, win_p, win_s)
```

```python
import functools

import jax
import jax.numpy as jnp
from jax import lax
from jax.experimental import pallas as pl
from jax.experimental.pallas import tpu as pltpu

D_MODEL = 1024
D_FF = 2816
CONV_W = 3
N_HEADS = 16
HEAD_DIM = 64
N_KV_HEADS = 4
GROUP = N_HEADS // N_KV_HEADS
KV_W = N_KV_HEADS * HEAD_DIM
L_CMP = 32
L_SEL = 64
TOP_N = 16
WINDOW = 512
PAGE_SIZE = 128
NORM_EPS = 1e-6
FORCE_BONUS = 1e4
NEG = -1e30

LANES = 128
VMEM_LIMIT = 56 * 1024 * 1024

F32 = jnp.float32
BF16 = jnp.bfloat16


def _params(*sem):
    return pltpu.CompilerParams(dimension_semantics=sem, vmem_limit_bytes=VMEM_LIMIT)


def _rms(x, g):
    return x * lax.rsqrt(jnp.mean(x * x, axis=-1, keepdims=True) + NORM_EPS) * g


def _dot(a, b):
    return jnp.dot(a, b, preferred_element_type=F32)


def _dot_nt(a, b):
    return lax.dot_general(a, b, (((1,), (1,)), ((), ())), preferred_element_type=F32)


def _dot_tn(a, b):
    return lax.dot_general(a, b, (((0,), (0,)), ((), ())), preferred_element_type=F32)


def _cast_kernel(w_ref, o_ref):
    o_ref[...] = w_ref[...].astype(BF16)


def _to_bf16(w, layer, *, steps=4):
    _, r, c = w.shape
    tr = r // steps
    return pl.pallas_call(
        _cast_kernel,
        grid=(steps,),
        in_specs=[pl.BlockSpec((None, tr, c), lambda i: (layer, i, 0))],
        out_specs=pl.BlockSpec((tr, c), lambda i: (i, 0)),
        out_shape=jax.ShapeDtypeStruct((r, c), BF16),
        compiler_params=_params("parallel"),
        name="to_bf16",
    )(w)


def _ffn_kernel(x_ref, g_ref, wg_ref, wu_ref, wd_ref, gf_ref, o_ref, *, final):
    x = x_ref[...]
    xn = _rms(x, g_ref[...]).astype(BF16)
    gate = _dot(xn, wg_ref[...])
    up = _dot(xn, wu_ref[...])
    act = (gate * jax.nn.sigmoid(gate) * up).astype(BF16)
    y = x + 0.5 * _dot(act, wd_ref[...])
    o_ref[...] = _rms(y, gf_ref[...]) if final else y


def _resident(shape, index):
    return pl.BlockSpec(shape, index, pipeline_mode=pl.Buffered(1))


def _ffn(x, g, w_gu, w_down, g_final, *, tm, final=False):
    n = x.shape[0]
    return pl.pallas_call(
        functools.partial(_ffn_kernel, final=final),
        grid=(n // tm,),
        in_specs=[
            pl.BlockSpec((tm, D_MODEL), lambda i: (i, 0)),
            pl.BlockSpec((1, D_MODEL), lambda i: (0, 0)),
            _resident((D_MODEL, D_FF), lambda i: (0, 0)),
            _resident((D_MODEL, D_FF), lambda i: (0, 1)),
            _resident((D_FF, D_MODEL), lambda i: (0, 0)),
            pl.BlockSpec((1, D_MODEL), lambda i: (0, 0)),
        ],
        out_specs=pl.BlockSpec((tm, D_MODEL), lambda i: (i, 0)),
        out_shape=jax.ShapeDtypeStruct((n, D_MODEL), F32),
        compiler_params=_params("parallel"),
        name="ffn_final" if final else "ffn",
    )(x, g.reshape(1, D_MODEL), w_gu, w_gu, w_down, g_final.reshape(1, D_MODEL))


CARRY = 8


def _conv_prompt_kernel(x_ref, g_ref, win_ref, wc_ref, wout_ref, o_ref, st_ref, uext_ref, *, tm):
    @pl.when(pl.program_id(1) == 0)
    def _():
        uext_ref[0:CARRY, :] = jnp.zeros((CARRY, D_MODEL), F32)

    x = x_ref[0]
    h = _rms(x, g_ref[...]).astype(BF16)
    p = _dot(h, win_ref[...])
    bg = p[:, :D_MODEL]
    u = p[:, D_MODEL:2 * D_MODEL] * p[:, 2 * D_MODEL:]
    uext_ref[CARRY:CARRY + tm, :] = u
    wc = wc_ref[...]
    conv = (wc[0:1] * uext_ref[CARRY - 2:CARRY - 2 + tm, :]
            + wc[1:2] * uext_ref[CARRY - 1:CARRY - 1 + tm, :]
            + wc[2:3] * u)
    y = _dot((bg * conv).astype(BF16), wout_ref[...])
    o_ref[0] = x + y
    st_ref[0] = uext_ref[CARRY + tm - 2:CARRY + tm, :]
    uext_ref[0:CARRY, :] = uext_ref[tm:tm + CARRY, :]


def _conv_prompt(x, g, w_in, w_conv, w_out, *, tm=512):
    b, t, _ = x.shape
    return pl.pallas_call(
        functools.partial(_conv_prompt_kernel, tm=tm),
        grid=(b, t // tm),
        in_specs=[
            pl.BlockSpec((1, tm, D_MODEL), lambda bi, ti: (bi, ti, 0)),
            pl.BlockSpec((1, D_MODEL), lambda bi, ti: (0, 0)),
            pl.BlockSpec((D_MODEL, 3 * D_MODEL), lambda bi, ti: (0, 0)),
            pl.BlockSpec((CONV_W, D_MODEL), lambda bi, ti: (0, 0)),
            pl.BlockSpec((D_MODEL, D_MODEL), lambda bi, ti: (0, 0)),
        ],
        out_specs=[
            pl.BlockSpec((1, tm, D_MODEL), lambda bi, ti: (bi, ti, 0)),
            pl.BlockSpec((1, CONV_W - 1, D_MODEL), lambda bi, ti: (bi, 0, 0)),
        ],
        out_shape=[jax.ShapeDtypeStruct((b, t, D_MODEL), F32),
                   jax.ShapeDtypeStruct((b, CONV_W - 1, D_MODEL), F32)],
        scratch_shapes=[pltpu.VMEM((CARRY + tm, D_MODEL), F32)],
        compiler_params=_params("parallel", "arbitrary"),
        name="conv_prompt",
    )(x, g.reshape(1, D_MODEL), w_in, w_conv, w_out)


def _conv_sample_kernel(x_ref, g_ref, win_ref, wc_ref, wout_ref, s1_ref, s2_ref, o_ref, u_ref, *, t_len):
    x = x_ref[...]
    h = _rms(x, g_ref[...]).astype(BF16)
    p = _dot(h, win_ref[...])
    bg = p[:, :D_MODEL]
    u = p[:, D_MODEL:2 * D_MODEL] * p[:, 2 * D_MODEL:]
    pos = lax.broadcasted_iota(jnp.int32, u.shape, 0) % t_len
    u1 = jnp.where(pos >= 1, pltpu.roll(u, 1, axis=0), s1_ref[...])
    u2 = jnp.where(pos >= 2, pltpu.roll(u, 2, axis=0), s2_ref[...])
    wc = wc_ref[...]
    conv = wc[0:1] * u2 + wc[1:2] * u1 + wc[2:3] * u
    o_ref[...] = x + _dot((bg * conv).astype(BF16), wout_ref[...])
    u_ref[...] = u


def _conv_sample(x, state, g, w_in, w_conv, w_out):
    b, t, _ = x.shape
    n = b * t
    zeros = jnp.zeros((b, t, D_MODEL), F32)
    s1 = zeros.at[:, 0].set(state[:, 1]).reshape(n, D_MODEL)
    s2 = zeros.at[:, 0].set(state[:, 0]).at[:, 1].set(state[:, 1]).reshape(n, D_MODEL)
    y, u = pl.pallas_call(
        functools.partial(_conv_sample_kernel, t_len=t),
        out_shape=[jax.ShapeDtypeStruct((n, D_MODEL), F32), jax.ShapeDtypeStruct((n, D_MODEL), F32)],
        compiler_params=pltpu.CompilerParams(vmem_limit_bytes=VMEM_LIMIT),
        name="conv_sample",
    )(x.reshape(n, D_MODEL), g.reshape(1, D_MODEL), w_in, w_conv, w_out, s1, s2)
    return y.reshape(b, t, D_MODEL), u.reshape(b, t, D_MODEL)[:, t - (CONV_W - 1):]


N_Q = N_HEADS * HEAD_DIM
Q_SCALE = HEAD_DIM ** -0.5 * 1.4426950408889634


def _proj_common(x_ref, g_ref, wqt_ref, wgt_ref, qt_ref, gt_ref):
    h = _rms(x_ref[...], g_ref[...]).astype(BF16)
    qt_ref[...] = (_dot_nt(wqt_ref[...], h) * Q_SCALE).astype(BF16)
    gt_ref[...] = jax.nn.sigmoid(_dot_nt(wgt_ref[...], h))
    return h


def _proj_prompt_kernel(x_ref, g_ref, wqt_ref, wgt_ref, wkvt_ref, wk_ref,
                        qt_ref, gt_ref, kvct_ref, kvst_ref, kvwt_ref, ks_ref, kw_ref, vst_ref, vwt_ref):
    h = _proj_common(x_ref, g_ref, wqt_ref, wgt_ref, qt_ref, gt_ref)
    kv_t = _dot_nt(wkvt_ref[...], h)
    kvct_ref[0] = kv_t[:2 * KV_W]
    kvst_ref[0] = kv_t[2 * KV_W:4 * KV_W]
    kvwt_ref[0] = kv_t[4 * KV_W:]
    vst_ref[...] = kv_t[3 * KV_W:4 * KV_W].astype(BF16)
    vwt_ref[...] = kv_t[5 * KV_W:].astype(BF16)
    k = _dot(h, wk_ref[...]).astype(BF16)
    ks_ref[...] = k[:, :KV_W]
    kw_ref[...] = k[:, KV_W:]


def _proj_sample_kernel(x_ref, g_ref, wqt_ref, wgt_ref, wkv_ref, qt_ref, gt_ref, kvc_ref, kvs_ref, kvw_ref):
    h = _proj_common(x_ref, g_ref, wqt_ref, wgt_ref, qt_ref, gt_ref)
    kv = _dot(h, wkv_ref[...])
    kvc_ref[...] = kv[:, :2 * KV_W]
    kvs_ref[...] = kv[:, 2 * KV_W:4 * KV_W]
    kvw_ref[...] = kv[:, 4 * KV_W:]


def _nsa_weights(w_in):
    wkv = w_in[:, N_Q:N_Q + 6 * KV_W]
    wk = jnp.concatenate([wkv[:, 2 * KV_W:3 * KV_W], wkv[:, 4 * KV_W:5 * KV_W]], axis=1)
    wgt = jnp.pad(w_in[:, N_Q + 6 * KV_W:], ((0, 0), (0, LANES - 3 * N_HEADS))).T
    return dict(wqt=w_in[:, :N_Q].T.astype(BF16), wgt=wgt.astype(BF16), wkv=wkv.astype(BF16),
                wkvt=wkv.T.astype(BF16), wk=wk.astype(BF16))


def _nsa_project_prompt(x, g, w, *, seq_len, tm):
    n = x.shape[0]
    per_seq = seq_len // tm
    row = lambda w_: pl.BlockSpec((tm, w_), lambda i: (i, 0))
    col = lambda h: pl.BlockSpec((h, tm), lambda i: (0, i))
    seq_t = pl.BlockSpec((1, 2 * KV_W, tm), lambda i: (i // per_seq, 0, i % per_seq))
    full = lambda a: pl.BlockSpec(a.shape, lambda i: (0, 0))
    weights = [w["wqt"], w["wgt"], w["wkvt"], w["wk"]]
    return pl.pallas_call(
        _proj_prompt_kernel,
        grid=(n // tm,),
        in_specs=[row(D_MODEL), pl.BlockSpec((1, D_MODEL), lambda i: (0, 0))] + [full(a) for a in weights],
        out_specs=[col(N_Q), col(LANES), seq_t, seq_t, seq_t, row(KV_W), row(KV_W), col(KV_W), col(KV_W)],
        out_shape=[jax.ShapeDtypeStruct((N_Q, n), BF16), jax.ShapeDtypeStruct((LANES, n), F32)]
        + [jax.ShapeDtypeStruct((n // seq_len, 2 * KV_W, seq_len), F32)] * 3
        + [jax.ShapeDtypeStruct((n, KV_W), BF16)] * 2
        + [jax.ShapeDtypeStruct((KV_W, n), BF16)] * 2,
        compiler_params=_params("parallel"),
        name="nsa_project_prompt",
    )(x, g.reshape(1, D_MODEL), *weights)


def _nsa_project_sample(x, g, w):
    n = x.shape[0]
    return pl.pallas_call(
        _proj_sample_kernel,
        out_shape=[jax.ShapeDtypeStruct((N_Q, n), BF16), jax.ShapeDtypeStruct((LANES, n), F32)]
        + [jax.ShapeDtypeStruct((n, 2 * KV_W), F32)] * 3,
        compiler_params=pltpu.CompilerParams(vmem_limit_bytes=VMEM_LIMIT),
        name="nsa_project_sample",
    )(x, g.reshape(1, D_MODEL), w["wqt"], w["wgt"], w["wkv"])


def _compress_out(acck, accv, w2k_ref, w2vt_ref, kc_ref, vct_ref):
    kc_ref[...] = _dot(jax.nn.gelu(acck).astype(BF16), w2k_ref[...]).astype(BF16)
    vct_ref[...] = _dot_nt(w2vt_ref[...], jax.nn.gelu(accv).astype(BF16)).astype(BF16)


def _compress_kernel(x_ref, pe_ref, w1k_ref, w1v_ref, w2k_ref, w2vt_ref, kc_ref, vct_ref, acck_ref, accv_ref):
    l = pl.program_id(1)

    @pl.when(l == 0)
    def _():
        acck_ref[...] = jnp.zeros_like(acck_ref)
        accv_ref[...] = jnp.zeros_like(accv_ref)

    xb = x_ref[...] + pe_ref[...]
    acck_ref[...] += _dot(xb[:, :KV_W].astype(BF16), w1k_ref[...])
    accv_ref[...] += _dot(xb[:, KV_W:].astype(BF16), w1v_ref[...])

    @pl.when(l == L_CMP - 1)
    def _():
        _compress_out(acck_ref[...], accv_ref[...], w2k_ref, w2vt_ref, kc_ref, vct_ref)


PAGES_PER_STEP = 32
BLOCKS_PER_PAGE = PAGE_SIZE // L_CMP


def _compress_weights(pe, w1k, w2k, w1v, w2v):
    eye = jnp.eye(N_KV_HEADS, dtype=F32)
    bd1 = lambda w1: jax.vmap(lambda w: jnp.kron(eye, w))(w1).astype(BF16)
    pe_row = jnp.tile(pe, (1, 2 * N_KV_HEADS))
    r = jnp.arange(2 * PAGE_SIZE)
    src = (r % 8 // BLOCKS_PER_PAGE) * PAGE_SIZE + (r % BLOCKS_PER_PAGE) * L_CMP + r // 8
    perm = (src[:, None] == r[None, :]).astype(BF16)
    return dict(pe_row=pe_row.reshape(L_CMP, 1, 2 * KV_W), pe_page_t=jnp.tile(pe_row, (BLOCKS_PER_PAGE, 1)).T, perm=perm,
                w1k=bd1(w1k), w1v=bd1(w1v), w2k=jnp.kron(eye, w2k).astype(BF16), w2vt=jnp.kron(eye, w2v).T.astype(BF16))


def _compress(x, w, *, tm):
    m = x.shape[0]
    tm = min(tm, m)
    return pl.pallas_call(
        _compress_kernel,
        grid=(m // tm, L_CMP),
        in_specs=[
            pl.BlockSpec((tm, 2 * KV_W), lambda i, l: (i, l)),
            pl.BlockSpec((None, 1, 2 * KV_W), lambda i, l: (l, 0, 0)),
            pl.BlockSpec((None, KV_W, KV_W), lambda i, l: (l, 0, 0)),
            pl.BlockSpec((None, KV_W, KV_W), lambda i, l: (l, 0, 0)),
            pl.BlockSpec((KV_W, KV_W), lambda i, l: (0, 0)),
            pl.BlockSpec((KV_W, KV_W), lambda i, l: (0, 0)),
        ],
        out_specs=[pl.BlockSpec((tm, KV_W), lambda i, l: (i, 0)), pl.BlockSpec((KV_W, tm), lambda i, l: (0, i))],
        out_shape=[jax.ShapeDtypeStruct((m, KV_W), BF16), jax.ShapeDtypeStruct((KV_W, m), BF16)],
        scratch_shapes=[pltpu.VMEM((tm, KV_W), F32)] * 2,
        compiler_params=_params("parallel", "arbitrary"),
        name="compress",
    )(x, w["pe_row"], w["w1k"], w["w1v"], w["w2k"], w["w2vt"])


def _compress_pages_body(refs, n_pages):
    pages = refs[:n_pages]
    pe_ref, perm_ref, w1k_ref, w1v_ref, w2k_ref, w2vt_ref, kc_ref, vct_ref = refs[n_pages:]
    pe = pe_ref[...]
    perm = perm_ref[...]
    by_l = []
    for p in range(0, n_pages, 2):
        pair_t = jnp.concatenate([(pages[p][...] + pe).astype(BF16), (pages[p + 1][...] + pe).astype(BF16)], axis=1)
        by_l.append(_dot_nt(perm, pair_t))
    acck = jnp.zeros((n_pages * BLOCKS_PER_PAGE, KV_W), F32)
    accv = jnp.zeros((n_pages * BLOCKS_PER_PAGE, KV_W), F32)
    for l in range(L_CMP):
        xl = jnp.concatenate([y[l * 8:(l + 1) * 8] for y in by_l], axis=0).astype(BF16)
        acck = acck + _dot(xl[:, :KV_W], w1k_ref[l])
        accv = accv + _dot(xl[:, KV_W:], w1v_ref[l])
    _compress_out(acck, accv, w2k_ref, w2vt_ref, kc_ref, vct_ref)


def _compress_paged_kernel(pt_ref, *refs, n_pages):
    del pt_ref
    _compress_pages_body(refs, n_pages)


def _compress_seq_kernel(*refs, n_pages):
    _compress_pages_body(refs, n_pages)


def _compress_consts(w):
    return [w["pe_page_t"], w["perm"], w["w1k"], w["w1v"], w["w2k"], w["w2vt"]]


def _compress_paged(cache, page_table, w):
    b, n_pages = page_table.shape
    pps = min(PAGES_PER_STEP, n_pages)
    steps = n_pages // pps
    m_step = pps * BLOCKS_PER_PAGE
    m = b * n_pages * BLOCKS_PER_PAGE
    page = lambda j: pl.BlockSpec((None, 2 * KV_W, PAGE_SIZE), lambda bi, s, pt: (pt[bi, s * pps + j], 0, 0))
    full = lambda a: pl.BlockSpec(a.shape, lambda bi, s, pt: (0,) * a.ndim)
    consts = _compress_consts(w)
    return pl.pallas_call(
        functools.partial(_compress_paged_kernel, n_pages=pps),
        grid_spec=pltpu.PrefetchScalarGridSpec(
            num_scalar_prefetch=1,
            grid=(b, steps),
            in_specs=[page(j) for j in range(pps)] + [full(a) for a in consts],
            out_specs=[pl.BlockSpec((m_step, KV_W), lambda bi, s, pt: (bi * steps + s, 0)),
                       pl.BlockSpec((KV_W, m_step), lambda bi, s, pt: (0, bi * steps + s))],
        ),
        out_shape=[jax.ShapeDtypeStruct((m, KV_W), BF16), jax.ShapeDtypeStruct((KV_W, m), BF16)],
        compiler_params=_params("parallel", "arbitrary"),
        name="compress_paged",
    )(page_table, *([cache] * pps), *consts)


def _compress_seq(rows_t, w):
    b, _, t = rows_t.shape
    per_seq = t // PAGE_SIZE
    n_pages = b * per_seq
    pps = min(PAGES_PER_STEP, n_pages)
    m_step = pps * BLOCKS_PER_PAGE
    m = n_pages * BLOCKS_PER_PAGE
    page = lambda j: pl.BlockSpec((None, 2 * KV_W, PAGE_SIZE),
                                  lambda s: ((s * pps + j) // per_seq, 0, (s * pps + j) % per_seq))
    full = lambda a: pl.BlockSpec(a.shape, lambda s: (0,) * a.ndim)
    consts = _compress_consts(w)
    return pl.pallas_call(
        functools.partial(_compress_seq_kernel, n_pages=pps),
        grid=(n_pages // pps,),
        in_specs=[page(j) for j in range(pps)] + [full(a) for a in consts],
        out_specs=[pl.BlockSpec((m_step, KV_W), lambda s: (s, 0)), pl.BlockSpec((KV_W, m_step), lambda s: (0, s))],
        out_shape=[jax.ShapeDtypeStruct((m, KV_W), BF16), jax.ShapeDtypeStruct((KV_W, m), BF16)],
        compiler_params=_params("parallel"),
        name="compress_seq",
    )(*([rows_t] * pps), *consts)


TQ_PROMPT = 256
TQ_SAMPLE = 128
KCH = 512
ONES_ROWS = 16
BIG = 3.0e38


def _head_queries(q_t, kv):
    zero = jnp.zeros((HEAD_DIM, q_t.shape[1]), BF16)
    cols = []
    for g in range(GROUP):
        h = kv * GROUP + g
        qg = q_t[h * HEAD_DIM:(h + 1) * HEAD_DIM, :]
        cols.append(jnp.concatenate([qg, zero] if kv % 2 == 0 else [zero, qg], axis=0))
    return jnp.concatenate(cols, axis=1)


def _slab(kv):
    return slice((kv // 2) * LANES, (kv // 2 + 1) * LANES)


def _head(kv):
    return slice(kv * HEAD_DIM, (kv + 1) * HEAD_DIM)


def _cmp_branch(kc, vc_t, x, keep):
    s = jnp.where(keep, _dot(kc, x), NEG)
    e = jnp.exp2(s - jnp.max(s, axis=0, keepdims=True))
    l = jnp.sum(e, axis=0, keepdims=True)
    e = jnp.where(keep, e, 0.0)
    return _dot(vc_t, e.astype(BF16)) / l, e / l


def _group_sum(p):
    tq = p.shape[1] // GROUP
    out = p[:, 0:tq]
    for g in range(1, GROUP):
        out = out + p[:, g * tq:(g + 1) * tq]
    return out


def _pair_sum(p):
    n = p.shape[0] // 2
    pair = (lax.broadcasted_iota(jnp.int32, (n, 2 * n), 1) // 2 == lax.broadcasted_iota(jnp.int32, (n, 2 * n), 0))
    return jnp.dot(pair.astype(F32), p, precision=lax.Precision.HIGHEST, preferred_element_type=F32)


def _with_ones(v_t):
    return jnp.concatenate([v_t, jnp.ones((ONES_ROWS, v_t.shape[1]), BF16)], axis=0)


def _topn_rows(score, n):
    nb = score.shape[0]
    row = lax.broadcasted_iota(jnp.int32, score.shape, 0).astype(F32)
    picked = jnp.zeros(score.shape, F32)
    ids = []
    for _ in range(n):
        top = jnp.max(score, axis=0, keepdims=True)
        first = jnp.min(jnp.where(score == top, row, float(nb)), axis=0, keepdims=True)
        hit = row == first
        picked = jnp.where(hit, 1.0, picked)
        score = jnp.where(hit, -jnp.inf, score)
        ids.append(first)
    return picked, ids


def _masked_scores(s, cap):
    tq = cap.shape[1]
    s = jnp.concatenate([jnp.minimum(s[:, g * tq:(g + 1) * tq], cap) for g in range(GROUP)], axis=1)
    return s, jnp.max(s, axis=0, keepdims=True)


def _softmax_pv(s, top, v_t):
    o = _dot(_with_ones(v_t), jnp.exp2(s - top).astype(BF16))
    return o[:HEAD_DIM] / o[HEAD_DIM:HEAD_DIM + 1]


def _gate_mix(g_t, kv, branches):
    tq = g_t.shape[1]
    out = []
    for g in range(GROUP):
        h = kv * GROUP + g
        c = slice(g * tq, (g + 1) * tq)
        acc = None
        for br, o in branches:
            term = g_t[br * N_HEADS + h:br * N_HEADS + h + 1, :] * o[:, c]
            acc = term if acc is None else acc + term
        out.append(acc)
    return out


def _attn_prompt_kernel(x_ref, qt_ref, gt_ref, kc_ref, vct_ref, ks_ref, vst_ref, kw_ref, *rest, t_len, tq):
    win_blocks = WINDOW // tq + 1
    cols = GROUP * tq
    vwt_refs = rest[:win_blocks]
    wout_ref, o_ref, m_ref, acc_ref = rest[win_blocks:]
    t0 = pl.program_id(1) * tq
    n_sel = t_len // L_SEL
    n_cmp = t_len // L_CMP
    t_tok = t0 + lax.broadcasted_iota(jnp.int32, (1, tq), 1)
    t_col = t0 + lax.broadcasted_iota(jnp.int32, (1, cols), 1) % tq
    t_heads = t0 + lax.broadcasted_iota(jnp.int32, (1, N_KV_HEADS * tq), 1) % tq
    r_c = lax.broadcasted_iota(jnp.int32, (n_cmp, 1), 0)
    keep_c = (r_c + 1) * L_CMP - 1 <= t_col
    blk = lax.broadcasted_iota(jnp.int32, (n_sel, N_KV_HEADS * tq), 0)
    cur = t_heads // L_SEL
    forced = (blk == 0) | (blk == cur) | (blk == cur - 1)
    valid = blk <= cur
    key_in_blk = lax.broadcasted_iota(jnp.int32, (L_SEL, 1), 0)
    win_pos = t0 - WINDOW + lax.broadcasted_iota(jnp.int32, (win_blocks * tq, 1), 0)
    dist = t_tok - win_pos
    cap_w = jnp.where((dist >= 0) & (dist < WINDOW) & (win_pos >= 0), BIG, NEG)
    n_chunks = t_len // KCH
    last_chunk = (t0 + tq - 1) // KCH
    q_t = qt_ref[...]
    g_t = gt_ref[...]

    o_cmp, imp = [], []
    for kv in range(N_KV_HEADS):
        o_c, p = _cmp_branch(kc_ref[0, :, _slab(kv)], vct_ref[0, _head(kv), :], _head_queries(q_t, kv), keep_c)
        o_cmp.append(o_c)
        imp.append(_group_sum(p))
    score = jnp.where(forced, FORCE_BONUS, jnp.where(valid, _pair_sum(jnp.concatenate(imp, axis=1)), -FORCE_BONUS))
    picked, _ = _topn_rows(score, min(TOP_N, n_sel))
    sel_cap = jnp.where(valid & (picked > 0.5), BIG, NEG)

    def key_caps(kv, c):
        rows = []
        for j in range(c * KCH // L_SEL, (c + 1) * KCH // L_SEL):
            causal = j * L_SEL + key_in_blk <= t_tok
            rows.append(jnp.where(causal, jnp.broadcast_to(sel_cap[j:j + 1, kv * tq:(kv + 1) * tq], (L_SEL, tq)), NEG))
        return jnp.concatenate(rows, axis=0)

    win = []
    for kv in range(N_KV_HEADS):
        x = _head_queries(q_t, kv)
        s_w = []
        for j in range(win_blocks):
            start = pl.multiple_of(jnp.maximum(t0 - WINDOW + j * tq, 0), tq)
            s_w.append(_dot(kw_ref[0, pl.ds(start, tq), _slab(kv)], x))
        win.append(_masked_scores(jnp.concatenate(s_w, axis=0), cap_w))
    o_win = [_softmax_pv(*win[kv], jnp.concatenate([r[_head(kv), :] for r in vwt_refs], axis=1))
             for kv in range(N_KV_HEADS)]

    m_ref[...] = jnp.full(m_ref.shape, NEG, F32)
    acc_ref[...] = jnp.zeros_like(acc_ref)

    def chunk(c):
        keys = slice(c * KCH, (c + 1) * KCH)
        scores, maxes = [], []
        for kv in range(N_KV_HEADS):
            s = _dot(ks_ref[0, keys, _slab(kv)], _head_queries(q_t, kv))
            cap = key_caps(kv, c)
            s = jnp.concatenate([jnp.minimum(s[:, g * tq:(g + 1) * tq], cap) for g in range(GROUP)], axis=1)
            scores.append(s)
            maxes.append(jnp.maximum(m_ref[kv], jnp.max(s, axis=0, keepdims=True)))
        for kv in range(N_KV_HEADS):
            e = jnp.exp2(scores[kv] - maxes[kv]).astype(BF16)
            acc_ref[kv] = (jnp.exp2(m_ref[kv] - maxes[kv]) * acc_ref[kv]
                           + _dot(_with_ones(vst_ref[_head(kv), keys]), e))
            m_ref[kv] = maxes[kv]

    chunk(0)
    for c in range(1, n_chunks):
        pl.when(c <= last_chunk)(functools.partial(chunk, c))

    heads = []
    for kv in range(N_KV_HEADS):
        acc = acc_ref[kv]
        o_s = acc[:HEAD_DIM] / acc[HEAD_DIM:HEAD_DIM + 1]
        heads += _gate_mix(g_t, kv, ((0, o_cmp[kv]), (1, o_s), (2, o_win[kv])))
    o = jnp.concatenate(heads, axis=0).T.astype(BF16)
    o_ref[0] = x_ref[0] + _dot(o, wout_ref[...])


def _attn_prompt(x, q_t, g_t, kc, vc_t, ks, vs_t, kw, vw_t, w_out, *, tq=TQ_PROMPT):
    b, t, _ = x.shape
    nt = t // tq
    n_cmp = t // L_CMP
    win_blocks = WINDOW // tq + 1
    cols = GROUP * tq
    tile = lambda w: pl.BlockSpec((1, tq, w), lambda bi, i: (bi, i, 0))
    tile_t = lambda h: pl.BlockSpec((h, tq), lambda bi, i: (0, bi * nt + i))
    win_t = lambda j: pl.BlockSpec((KV_W, tq), lambda bi, i: (0, bi * nt + jnp.maximum(i - (win_blocks - 1) + j, 0)))
    return pl.pallas_call(
        functools.partial(_attn_prompt_kernel, t_len=t, tq=tq),
        grid=(b, nt),
        in_specs=[tile(D_MODEL), tile_t(N_Q), tile_t(LANES),
                  pl.BlockSpec((1, n_cmp, KV_W), lambda bi, i: (bi, 0, 0)),
                  pl.BlockSpec((1, KV_W, n_cmp), lambda bi, i: (bi, 0, 0)),
                  pl.BlockSpec((1, t, KV_W), lambda bi, i: (bi, 0, 0)),
                  pl.BlockSpec((KV_W, t), lambda bi, i: (0, bi)),
                  pl.BlockSpec((1, t, KV_W), lambda bi, i: (bi, 0, 0))]
        + [win_t(j) for j in range(win_blocks)]
        + [pl.BlockSpec((N_Q, D_MODEL), lambda bi, i: (0, 0))],
        out_specs=tile(D_MODEL),
        out_shape=jax.ShapeDtypeStruct((b, t, D_MODEL), F32),
        scratch_shapes=[pltpu.VMEM((N_KV_HEADS, 1, cols), F32),
                        pltpu.VMEM((N_KV_HEADS, HEAD_DIM + ONES_ROWS, cols), F32)],
        compiler_params=_params("parallel", "arbitrary"),
        name="attn_prompt",
    )(x, q_t, g_t, kc, vc_t, ks, vs_t, kw, *([vw_t] * win_blocks), w_out)


NEW_ROWS = 16


def _attn_sample_kernel(qt_ref, gt_ref, kc_ref, vct_ref, win_ref, new_ref, part_ref, idx_ref, *, past, n_blk, half):
    tq = TQ_SAMPLE
    t_tok = past + lax.broadcasted_iota(jnp.int32, (1, tq), 1)
    t_col = past + lax.broadcasted_iota(jnp.int32, (1, GROUP * tq), 1) % tq
    t_heads = past + lax.broadcasted_iota(jnp.int32, (1, N_KV_HEADS * tq), 1) % tq
    r_c = lax.broadcasted_iota(jnp.int32, (2 * half, 1), 0)
    keep_c = (r_c < 2 * n_blk) & ((r_c + 1) * L_CMP - 1 <= t_col)
    blk = lax.broadcasted_iota(jnp.int32, (half, N_KV_HEADS * tq), 0)
    cur = t_heads // L_SEL
    forced = (blk == 0) | (blk == cur) | (blk == cur - 1)
    valid = blk <= cur
    wb = win_ref.shape[2]
    win_pos = past - wb + lax.broadcasted_iota(jnp.int32, (wb + NEW_ROWS, 1), 0)
    dist = t_tok - win_pos
    keep_w = (dist >= 0) & (dist < WINDOW) & (win_pos >= 0)
    q_t = qt_ref[0]
    g_t = gt_ref[0]
    o_cmp, imp = [], []
    for kv in range(N_KV_HEADS):
        o_c, p = _cmp_branch(kc_ref[0, :, _slab(kv)], vct_ref[0, _head(kv), :], _head_queries(q_t, kv), keep_c)
        o_cmp.append(o_c)
        imp.append(_group_sum(p))
    score = jnp.where(forced, FORCE_BONUS, jnp.where(valid, _pair_sum(jnp.concatenate(imp, axis=1)), -FORCE_BONUS))
    score = jnp.where(blk < n_blk, score, -jnp.inf)
    _, ids = _topn_rows(score, min(TOP_N, n_blk))
    for j, first in enumerate(ids):
        for kv in range(N_KV_HEADS):
            idx_ref[0, kv, j:j + 1, :] = first[:, kv * tq:(kv + 1) * tq].astype(jnp.int32)

    heads = []
    for kv in range(N_KV_HEADS):
        x = _head_queries(q_t, kv)
        mine = slice((kv % 2) * HEAD_DIM, (kv % 2 + 1) * HEAD_DIM)
        v_slab = slice(KV_W + (kv // 2) * LANES, KV_W + (kv // 2 + 1) * LANES)
        s = jnp.concatenate([_dot_tn(win_ref[0, _head(kv), :], x[mine].astype(F32)),
                             _dot(new_ref[0, :, _slab(kv)].astype(BF16), x)], axis=0)
        s = jnp.concatenate([jnp.where(keep_w, s[:, g * tq:(g + 1) * tq], NEG) for g in range(GROUP)], axis=1)
        e = jnp.exp2(s - jnp.max(s, axis=0, keepdims=True))
        p_w = e / jnp.sum(e, axis=0, keepdims=True)
        v_t = win_ref[0, KV_W + kv * HEAD_DIM:KV_W + (kv + 1) * HEAD_DIM, :]
        o_w = _dot(v_t.astype(BF16), p_w[:wb].astype(BF16)) + _dot_tn(new_ref[0, :, v_slab], p_w[wb:])[mine]
        heads += _gate_mix(g_t, kv, ((0, o_cmp[kv]), (2, o_w)))
    part_ref[0] = jnp.concatenate(heads, axis=0)


def _attn_sample(q_t, g_t, kc, vc_t, win, new, *, past, n_blk):
    b = q_t.shape[0]
    half = kc.shape[1] // 2
    n_top = min(TOP_N, n_blk)
    blk3 = lambda a: pl.BlockSpec((1,) + a.shape[1:], lambda bi: (bi, 0, 0))
    return pl.pallas_call(
        functools.partial(_attn_sample_kernel, past=past, n_blk=n_blk, half=half),
        grid=(b,),
        in_specs=[blk3(q_t), blk3(g_t), blk3(kc), blk3(vc_t), blk3(win), blk3(new)],
        out_specs=[pl.BlockSpec((1, N_Q, TQ_SAMPLE), lambda bi: (bi, 0, 0)),
                   pl.BlockSpec((1, N_KV_HEADS, n_top, TQ_SAMPLE), lambda bi: (bi, 0, 0, 0))],
        out_shape=[jax.ShapeDtypeStruct((b, N_Q, TQ_SAMPLE), F32),
                   jax.ShapeDtypeStruct((b, N_KV_HEADS, n_top, TQ_SAMPLE), jnp.int32)],
        compiler_params=_params("parallel"),
        name="attn_sample",
    )(q_t, g_t, kc, vc_t, win, new)


Q_ROWS = 16


def _sel_sample_kernel(pt_ref, bid_ref, q_ref, new_ref, cache_ref, o_ref, slab_buf, sem, *,
                       past, n_past_blk, n_top):
    n_slab = N_KV_HEADS * n_top
    n_t = pl.num_programs(1)
    step = pl.program_id(0) * n_t + pl.program_id(1)
    n_steps = pl.num_programs(0) * n_t
    blocks_per_page = PAGE_SIZE // L_SEL

    def slab_copies(at_step):
        slot = at_step % 2
        copies = []
        for i in range(n_slab):
            past_blk = jnp.minimum(bid_ref[at_step * n_slab + i], n_past_blk - 1)
            page = pt_ref[at_step // n_t, past_blk // blocks_per_page]
            copies.append(pltpu.make_async_copy(cache_ref.at[page, :, i // n_top], slab_buf.at[slot, i], sem.at[slot]))
        return copies

    @pl.when(step == 0)
    def _():
        for copy in slab_copies(step):
            copy.start()

    @pl.when(step + 1 < n_steps)
    def _():
        for copy in slab_copies(step + 1):
            copy.start()

    for copy in slab_copies(step):
        copy.wait()

    slot = step % 2
    t_pos = past + pl.program_id(1)
    lane = lax.broadcasted_iota(jnp.int32, (1, PAGE_SIZE), 1)
    for kv in range(N_KV_HEADS):
        k_t, v_t, keep = [], [], []
        for j in range(n_top):
            bid = bid_ref[step * n_slab + kv * n_top + j]
            from_past = bid < n_past_blk
            first_lane = jnp.where(from_past, (bid % blocks_per_page) * L_SEL, 0)
            k_t.append(jnp.where(from_past, slab_buf[slot, kv * n_top + j, 0], new_ref[0, kv]).astype(BF16))
            v_t.append(jnp.where(from_past, slab_buf[slot, kv * n_top + j, 1], new_ref[1, kv]).astype(BF16))
            in_blk = lane - first_lane
            keep.append((in_blk >= 0) & (in_blk < L_SEL) & (bid * L_SEL + in_blk <= t_pos))
        s = jnp.where(jnp.concatenate(keep, axis=1), _dot(q_ref[kv], jnp.concatenate(k_t, axis=1)), NEG)
        e = jnp.exp2(s - jnp.max(s, axis=1, keepdims=True))
        p = e / jnp.sum(e, axis=1, keepdims=True)
        o_ref[kv] = _dot_nt(p.astype(BF16), jnp.concatenate(v_t, axis=1))


def _sel_sample(page_table, blk_ids, q, cache, new_slabs, *, past, n_past_blk, n_top):
    b, t, n_kv = q.shape[:3]
    return pl.pallas_call(
        functools.partial(_sel_sample_kernel, past=past, n_past_blk=n_past_blk, n_top=n_top),
        grid_spec=pltpu.PrefetchScalarGridSpec(
            num_scalar_prefetch=2,
            grid=(b, t),
            in_specs=[pl.BlockSpec((None, None, n_kv, Q_ROWS, HEAD_DIM), lambda bi, ti, pt, bid: (bi, ti, 0, 0, 0)),
                      pl.BlockSpec((None, 2, n_kv, HEAD_DIM, PAGE_SIZE), lambda bi, ti, pt, bid: (bi, 0, 0, 0, 0)),
                      pl.BlockSpec(memory_space=pl.ANY)],
            out_specs=pl.BlockSpec((None, None, n_kv, Q_ROWS, HEAD_DIM), lambda bi, ti, pt, bid: (bi, ti, 0, 0, 0)),
            scratch_shapes=[pltpu.VMEM((2, n_kv * n_top, 2, HEAD_DIM, PAGE_SIZE), F32), pltpu.SemaphoreType.DMA((2,))],
        ),
        out_shape=jax.ShapeDtypeStruct((b, t, n_kv, Q_ROWS, HEAD_DIM), F32),
        compiler_params=_params("arbitrary", "arbitrary"),
        name="sel_sample",
    )(page_table, blk_ids, q, new_slabs, cache)


def _out_sample_kernel(x_ref, part_ref, os_ref, gs_ref, wout_ref, o_ref):
    o = (part_ref[...] + gs_ref[...] * os_ref[...]).astype(BF16)
    o_ref[...] = x_ref[...] + _dot(o, wout_ref[...])


def _out_sample(x, part, o_s, g_s, w_out):
    return pl.pallas_call(
        _out_sample_kernel,
        out_shape=jax.ShapeDtypeStruct(x.shape, F32),
        compiler_params=pltpu.CompilerParams(vmem_limit_bytes=VMEM_LIMIT),
        name="out_sample",
    )(x, part, o_s, g_s, w_out)


def kernel(x_prompt, x_sample, state_conv, cache_cmp_kv, cache_sel_kv, cache_win_kv, page_table, norm_ffa, w_ffa_gu, w_ffa_down, norm_mix, norm_ffb, w_ffb_gu, w_ffb_down, w_conv_in, w_conv, w_conv_out, w_nsa_in, pe_cmp, w_cmp_k1, w_cmp_k2, w_cmp_v1, w_cmp_v2, w_nsa_out, norm_final):
    bp, tp, _ = x_prompt.shape
    bs, ts, _ = x_sample.shape
    n_p, n_s = bp * tp, bs * ts
    xp = x_prompt.reshape(n_p, D_MODEL)
    xs = x_sample.reshape(n_s, D_MODEL)

    def ffn_pair(xp, xs, g, w_gu, w_down, layer, final=False):
        w_gu, w_down = _to_bf16(w_gu, layer), _to_bf16(w_down, layer)
        return (_ffn(xp, g, w_gu, w_down, norm_final, tm=512, final=final),
                _ffn(xs, g, w_gu, w_down, norm_final, tm=n_s, final=final))

    xp, xs = ffn_pair(xp, xs, norm_ffa[0], w_ffa_gu, w_ffa_down, 0)
    w_in, w_out = _to_bf16(w_conv_in, 0), _to_bf16(w_conv_out, 0)
    xp3, conv_p = _conv_prompt(xp.reshape(bp, tp, D_MODEL), norm_mix[0], w_in, w_conv[0], w_out)
    xs3, conv_s = _conv_sample(xs.reshape(bs, ts, D_MODEL), state_conv[0], norm_mix[0], w_in, w_conv[0], w_out)
    xp, xs = ffn_pair(xp3.reshape(n_p, D_MODEL), xs3.reshape(n_s, D_MODEL), norm_ffb[0], w_ffb_gu, w_ffb_down, 0)

    xp, xs = ffn_pair(xp, xs, norm_ffa[1], w_ffa_gu, w_ffa_down, 1)
    proj_w = _nsa_weights(w_nsa_in[0])
    cmp_w = _compress_weights(pe_cmp[0], w_cmp_k1[0], w_cmp_k2[0], w_cmp_v1[0], w_cmp_v2[0])
    w_out = _to_bf16(w_nsa_out, 0)
    kv_shape = (2, N_KV_HEADS, HEAD_DIM)
    row_w = 2 * KV_W

    q_t, g_t, kvc_t, kvs_t, kvw_t, ks, kw, vs_t, vw_t = _nsa_project_prompt(xp, norm_mix[1], proj_w, seq_len=tp, tm=512)
    n_cmp = tp // L_CMP
    kc, vc_t = _compress_seq(kvc_t, cmp_w)
    kc = kc.reshape(bp, n_cmp, KV_W)
    vc_t = vc_t.reshape(KV_W, bp, n_cmp).transpose(1, 0, 2)
    seq = lambda a: a.reshape(bp, tp, a.shape[-1])
    xp3 = _attn_prompt(seq(xp), q_t, g_t, kc, vc_t, seq(ks), vs_t, seq(kw), vw_t, w_out)
    by_row = lambda a: a.reshape((bp,) + kv_shape + (a.shape[-1],)).transpose(0, 4, 1, 2, 3)[None]
    cmp_p = by_row(kvc_t)
    sel_p = by_row(kvs_t)
    win_p = by_row(kvw_t[:, :, tp - min(WINDOW, tp):])

    q_t, g_t, kvc, kvs, kvw = _nsa_project_sample(xs, norm_mix[1], proj_w)
    n_pages = page_table.shape[1]
    past = n_pages * PAGE_SIZE
    n_new = -(-ts // L_SEL) * L_SEL
    n_past_blk = past // L_SEL
    n_blk = n_past_blk + n_new // L_SEL
    n_top = min(TOP_N, n_blk)
    lanes = lambda a: jnp.pad(a.reshape(a.shape[0], bs, ts).transpose(1, 0, 2), ((0, 0), (0, 0), (0, TQ_SAMPLE - ts)))
    by_channel = lambda c: c.transpose(0, 2, 3, 4, 1)
    kc_past, vct_past = _compress_paged(by_channel(cache_cmp_kv[0]).reshape(-1, row_w, PAGE_SIZE), page_table, cmp_w)
    new_c = jnp.pad(kvc.reshape(bs, ts, row_w), ((0, 0), (0, n_new - ts), (0, 0)))
    kc_new, vct_new = _compress(new_c.reshape(bs * n_new // L_CMP, L_CMP * row_w), cmp_w, tm=bs * n_new // L_CMP)
    half = -(-n_blk // (LANES // 2)) * (LANES // 2)
    pad = 2 * half - 2 * n_blk
    kc_all = jnp.pad(jnp.concatenate([kc_past.reshape(bs, -1, KV_W), kc_new.reshape(bs, -1, KV_W)], axis=1),
                     ((0, 0), (0, pad), (0, 0)))
    vct_all = jnp.pad(jnp.concatenate([vct_past.reshape(KV_W, bs, -1), vct_new.reshape(KV_W, bs, -1)], axis=2),
                      ((0, 0), (0, 0), (0, pad)))
    wb = cache_win_kv.shape[2]
    new_w = jnp.pad(kvw.reshape(bs, ts, row_w), ((0, 0), (0, NEW_ROWS - ts), (0, 0)))
    part, idx = _attn_sample(lanes(q_t), lanes(g_t), kc_all, vct_all.transpose(1, 0, 2),
                             by_channel(cache_win_kv[0]).reshape(bs, row_w, wb), new_w, past=past, n_blk=n_blk)
    bid = idx[..., :ts].transpose(0, 3, 1, 2)
    q5 = q_t.reshape(N_KV_HEADS, GROUP, HEAD_DIM, bs, ts).transpose(3, 4, 0, 1, 2)
    q5 = jnp.pad(q5, ((0, 0),) * 3 + ((0, Q_ROWS - GROUP), (0, 0)))
    new_s = jnp.pad(kvs.reshape((bs, ts) + kv_shape).transpose(0, 2, 3, 4, 1), ((0, 0),) * 4 + ((0, PAGE_SIZE - ts),))
    o_s = _sel_sample(page_table, bid.reshape(-1), q5, by_channel(cache_sel_kv[0]), new_s,
                      past=past, n_past_blk=n_past_blk, n_top=n_top)
    o_s = o_s[:, :, :, :GROUP].reshape(n_s, N_Q)
    g_s = jnp.repeat(g_t[N_HEADS:2 * N_HEADS].T, HEAD_DIM, axis=1)
    part = part[:, :, :ts].transpose(0, 2, 1).reshape(n_s, N_Q)
    xs = _out_sample(xs, part, o_s, g_s, w_out)
    cmp_s = kvc.reshape((1, bs, ts) + kv_shape)
    sel_s = kvs.reshape((1, bs, ts) + kv_shape)
    win_s = jnp.concatenate([cache_win_kv[0], kvw.reshape((bs, ts) + kv_shape)], axis=1)[None, :, ts:]

    xp, xs = ffn_pair(xp3.reshape(n_p, D_MODEL), xs, norm_ffb[1], w_ffb_gu, w_ffb_down, 1, final=True)
    return (xp.reshape(bp, tp, D_MODEL), xs.reshape(bs, ts, D_MODEL),
            conv_p[None], conv_s[None], cmp_p, cmp_s, sel_p, sel_s, win_p, win_s)
```

```python
import functools

import jax
import jax.numpy as jnp
from jax import lax
from jax.experimental import pallas as pl
from jax.experimental.pallas import tpu as pltpu

D_MODEL = 1024
D_FF = 2816
CONV_W = 3
N_HEADS = 16
HEAD_DIM = 64
N_KV_HEADS = 4
GROUP = N_HEADS // N_KV_HEADS
KV_W = N_KV_HEADS * HEAD_DIM
L_CMP = 32
L_SEL = 64
TOP_N = 16
WINDOW = 512
PAGE_SIZE = 128
NORM_EPS = 1e-6
FORCE_BONUS = 1e4
NEG = -1e30

LANES = 128
VMEM_LIMIT = 56 * 1024 * 1024

F32 = jnp.float32
BF16 = jnp.bfloat16


def _params(*sem):
    return pltpu.CompilerParams(dimension_semantics=sem, vmem_limit_bytes=VMEM_LIMIT)


def _rms(x, g):
    return x * lax.rsqrt(jnp.mean(x * x, axis=-1, keepdims=True) + NORM_EPS) * g


def _dot(a, b):
    return jnp.dot(a, b, preferred_element_type=F32)


def _dot_nt(a, b):
    return lax.dot_general(a, b, (((1,), (1,)), ((), ())), preferred_element_type=F32)


def _dot_tn(a, b):
    return lax.dot_general(a, b, (((0,), (0,)), ((), ())), preferred_element_type=F32)


def _cast_kernel(w_ref, o_ref):
    o_ref[...] = w_ref[...].astype(BF16)


def _to_bf16(w, layer, *, steps=4):
    _, r, c = w.shape
    tr = r // steps
    return pl.pallas_call(
        _cast_kernel,
        grid=(steps,),
        in_specs=[pl.BlockSpec((None, tr, c), lambda i: (layer, i, 0))],
        out_specs=pl.BlockSpec((tr, c), lambda i: (i, 0)),
        out_shape=jax.ShapeDtypeStruct((r, c), BF16),
        compiler_params=_params("parallel"),
        name="to_bf16",
    )(w)


def _ffn_kernel(x_ref, g_ref, wg_ref, wu_ref, wd_ref, gf_ref, o_ref, *, final):
    x = x_ref[...]
    xn = _rms(x, g_ref[...]).astype(BF16)
    gate = _dot(xn, wg_ref[...])
    up = _dot(xn, wu_ref[...])
    act = (gate * jax.nn.sigmoid(gate) * up).astype(BF16)
    y = x + 0.5 * _dot(act, wd_ref[...])
    o_ref[...] = _rms(y, gf_ref[...]) if final else y


def _resident(shape, index):
    return pl.BlockSpec(shape, index, pipeline_mode=pl.Buffered(1))


def _ffn(x, g, w_gu, w_down, g_final, *, tm, final=False):
    n = x.shape[0]
    return pl.pallas_call(
        functools.partial(_ffn_kernel, final=final),
        grid=(n // tm,),
        in_specs=[
            pl.BlockSpec((tm, D_MODEL), lambda i: (i, 0)),
            pl.BlockSpec((1, D_MODEL), lambda i: (0, 0)),
            _resident((D_MODEL, D_FF), lambda i: (0, 0)),
            _resident((D_MODEL, D_FF), lambda i: (0, 1)),
            _resident((D_FF, D_MODEL), lambda i: (0, 0)),
            pl.BlockSpec((1, D_MODEL), lambda i: (0, 0)),
        ],
        out_specs=pl.BlockSpec((tm, D_MODEL), lambda i: (i, 0)),
        out_shape=jax.ShapeDtypeStruct((n, D_MODEL), F32),
        compiler_params=_params("parallel"),
        name="ffn_final" if final else "ffn",
    )(x, g.reshape(1, D_MODEL), w_gu, w_gu, w_down, g_final.reshape(1, D_MODEL))


CARRY = 8


def _conv_prompt_kernel(x_ref, g_ref, win_ref, wc_ref, wout_ref, o_ref, st_ref, uext_ref, *, tm):
    @pl.when(pl.program_id(1) == 0)
    def _():
        uext_ref[0:CARRY, :] = jnp.zeros((CARRY, D_MODEL), F32)

    x = x_ref[0]
    h = _rms(x, g_ref[...]).astype(BF16)
    p = _dot(h, win_ref[...])
    bg = p[:, :D_MODEL]
    u = p[:, D_MODEL:2 * D_MODEL] * p[:, 2 * D_MODEL:]
    uext_ref[CARRY:CARRY + tm, :] = u
    wc = wc_ref[...]
    conv = (wc[0:1] * uext_ref[CARRY - 2:CARRY - 2 + tm, :]
            + wc[1:2] * uext_ref[CARRY - 1:CARRY - 1 + tm, :]
            + wc[2:3] * u)
    y = _dot((bg * conv).astype(BF16), wout_ref[...])
    o_ref[0] = x + y
    st_ref[0] = uext_ref[CARRY + tm - 2:CARRY + tm, :]
    uext_ref[0:CARRY, :] = uext_ref[tm:tm + CARRY, :]


def _conv_prompt(x, g, w_in, w_conv, w_out, *, tm=512):
    b, t, _ = x.shape
    return pl.pallas_call(
        functools.partial(_conv_prompt_kernel, tm=tm),
        grid=(b, t // tm),
        in_specs=[
            pl.BlockSpec((1, tm, D_MODEL), lambda bi, ti: (bi, ti, 0)),
            pl.BlockSpec((1, D_MODEL), lambda bi, ti: (0, 0)),
            pl.BlockSpec((D_MODEL, 3 * D_MODEL), lambda bi, ti: (0, 0)),
            pl.BlockSpec((CONV_W, D_MODEL), lambda bi, ti: (0, 0)),
            pl.BlockSpec((D_MODEL, D_MODEL), lambda bi, ti: (0, 0)),
        ],
        out_specs=[
            pl.BlockSpec((1, tm, D_MODEL), lambda bi, ti: (bi, ti, 0)),
            pl.BlockSpec((1, CONV_W - 1, D_MODEL), lambda bi, ti: (bi, 0, 0)),
        ],
        out_shape=[jax.ShapeDtypeStruct((b, t, D_MODEL), F32),
                   jax.ShapeDtypeStruct((b, CONV_W - 1, D_MODEL), F32)],
        scratch_shapes=[pltpu.VMEM((CARRY + tm, D_MODEL), F32)],
        compiler_params=_params("parallel", "arbitrary"),
        name="conv_prompt",
    )(x, g.reshape(1, D_MODEL), w_in, w_conv, w_out)


def _conv_sample_kernel(x_ref, g_ref, win_ref, wc_ref, wout_ref, s1_ref, s2_ref, o_ref, u_ref, *, t_len):
    x = x_ref[...]
    h = _rms(x, g_ref[...]).astype(BF16)
    p = _dot(h, win_ref[...])
    bg = p[:, :D_MODEL]
    u = p[:, D_MODEL:2 * D_MODEL] * p[:, 2 * D_MODEL:]
    pos = lax.broadcasted_iota(jnp.int32, u.shape, 0) % t_len
    u1 = jnp.where(pos >= 1, pltpu.roll(u, 1, axis=0), s1_ref[...])
    u2 = jnp.where(pos >= 2, pltpu.roll(u, 2, axis=0), s2_ref[...])
    wc = wc_ref[...]
    conv = wc[0:1] * u2 + wc[1:2] * u1 + wc[2:3] * u
    o_ref[...] = x + _dot((bg * conv).astype(BF16), wout_ref[...])
    u_ref[...] = u


def _conv_sample(x, state, g, w_in, w_conv, w_out):
    b, t, _ = x.shape
    n = b * t
    zeros = jnp.zeros((b, t, D_MODEL), F32)
    s1 = zeros.at[:, 0].set(state[:, 1]).reshape(n, D_MODEL)
    s2 = zeros.at[:, 0].set(state[:, 0]).at[:, 1].set(state[:, 1]).reshape(n, D_MODEL)
    y, u = pl.pallas_call(
        functools.partial(_conv_sample_kernel, t_len=t),
        out_shape=[jax.ShapeDtypeStruct((n, D_MODEL), F32), jax.ShapeDtypeStruct((n, D_MODEL), F32)],
        compiler_params=pltpu.CompilerParams(vmem_limit_bytes=VMEM_LIMIT),
        name="conv_sample",
    )(x.reshape(n, D_MODEL), g.reshape(1, D_MODEL), w_in, w_conv, w_out, s1, s2)
    return y.reshape(b, t, D_MODEL), u.reshape(b, t, D_MODEL)[:, t - (CONV_W - 1):]


N_Q = N_HEADS * HEAD_DIM
Q_SCALE = HEAD_DIM ** -0.5 * 1.4426950408889634


def _proj_common(x_ref, g_ref, wqt_ref, wgt_ref, qt_ref, gt_ref):
    h = _rms(x_ref[...], g_ref[...]).astype(BF16)
    qt_ref[...] = (_dot_nt(wqt_ref[...], h) * Q_SCALE).astype(BF16)
    gt_ref[...] = jax.nn.sigmoid(_dot_nt(wgt_ref[...], h))
    return h


def _proj_prompt_kernel(x_ref, g_ref, wqt_ref, wgt_ref, wkvt_ref, wk_ref,
                        qt_ref, gt_ref, kvct_ref, kvst_ref, kvwt_ref, ks_ref, kw_ref, vst_ref, vwt_ref):
    h = _proj_common(x_ref, g_ref, wqt_ref, wgt_ref, qt_ref, gt_ref)
    kv_t = _dot_nt(wkvt_ref[...], h)
    kvct_ref[0] = kv_t[:2 * KV_W]
    kvst_ref[0] = kv_t[2 * KV_W:4 * KV_W]
    kvwt_ref[0] = kv_t[4 * KV_W:]
    vst_ref[...] = kv_t[3 * KV_W:4 * KV_W].astype(BF16)
    vwt_ref[...] = kv_t[5 * KV_W:].astype(BF16)
    k = _dot(h, wk_ref[...]).astype(BF16)
    ks_ref[...] = k[:, :KV_W]
    kw_ref[...] = k[:, KV_W:]


def _proj_sample_kernel(x_ref, g_ref, wqt_ref, wgt_ref, wkv_ref, qt_ref, gt_ref, kvc_ref, kvs_ref, kvw_ref):
    h = _proj_common(x_ref, g_ref, wqt_ref, wgt_ref, qt_ref, gt_ref)
    kv = _dot(h, wkv_ref[...])
    kvc_ref[...] = kv[:, :2 * KV_W]
    kvs_ref[...] = kv[:, 2 * KV_W:4 * KV_W]
    kvw_ref[...] = kv[:, 4 * KV_W:]


def _nsa_weights(w_in):
    wkv = w_in[:, N_Q:N_Q + 6 * KV_W]
    wk = jnp.concatenate([wkv[:, 2 * KV_W:3 * KV_W], wkv[:, 4 * KV_W:5 * KV_W]], axis=1)
    wgt = jnp.pad(w_in[:, N_Q + 6 * KV_W:], ((0, 0), (0, LANES - 3 * N_HEADS))).T
    return dict(wqt=w_in[:, :N_Q].T.astype(BF16), wgt=wgt.astype(BF16), wkv=wkv.astype(BF16),
                wkvt=wkv.T.astype(BF16), wk=wk.astype(BF16))


def _nsa_project_prompt(x, g, w, *, seq_len, tm):
    n = x.shape[0]
    per_seq = seq_len // tm
    row = lambda w_: pl.BlockSpec((tm, w_), lambda i: (i, 0))
    col = lambda h: pl.BlockSpec((h, tm), lambda i: (0, i))
    seq_t = pl.BlockSpec((1, 2 * KV_W, tm), lambda i: (i // per_seq, 0, i % per_seq))
    full = lambda a: pl.BlockSpec(a.shape, lambda i: (0, 0))
    weights = [w["wqt"], w["wgt"], w["wkvt"], w["wk"]]
    return pl.pallas_call(
        _proj_prompt_kernel,
        grid=(n // tm,),
        in_specs=[row(D_MODEL), pl.BlockSpec((1, D_MODEL), lambda i: (0, 0))] + [full(a) for a in weights],
        out_specs=[col(N_Q), col(LANES), seq_t, seq_t, seq_t, row(KV_W), row(KV_W), col(KV_W), col(KV_W)],
        out_shape=[jax.ShapeDtypeStruct((N_Q, n), BF16), jax.ShapeDtypeStruct((LANES, n), F32)]
        + [jax.ShapeDtypeStruct((n // seq_len, 2 * KV_W, seq_len), F32)] * 3
        + [jax.ShapeDtypeStruct((n, KV_W), BF16)] * 2
        + [jax.ShapeDtypeStruct((KV_W, n), BF16)] * 2,
        compiler_params=_params("parallel"),
        name="nsa_project_prompt",
    )(x, g.reshape(1, D_MODEL), *weights)


def _nsa_project_sample(x, g, w):
    n = x.shape[0]
    return pl.pallas_call(
        _proj_sample_kernel,
        out_shape=[jax.ShapeDtypeStruct((N_Q, n), BF16), jax.ShapeDtypeStruct((LANES, n), F32)]
        + [jax.ShapeDtypeStruct((n, 2 * KV_W), F32)] * 3,
        compiler_params=pltpu.CompilerParams(vmem_limit_bytes=VMEM_LIMIT),
        name="nsa_project_sample",
    )(x, g.reshape(1, D_MODEL), w["wqt"], w["wgt"], w["wkv"])


def _compress_out(acck, accv, w2k_ref, w2vt_ref, kc_ref, vct_ref):
    kc_ref[...] = _dot(jax.nn.gelu(acck).astype(BF16), w2k_ref[...]).astype(BF16)
    vct_ref[...] = _dot_nt(w2vt_ref[...], jax.nn.gelu(accv).astype(BF16)).astype(BF16)


def _compress_kernel(x_ref, pe_ref, w1k_ref, w1v_ref, w2k_ref, w2vt_ref, kc_ref, vct_ref, acck_ref, accv_ref):
    l = pl.program_id(1)

    @pl.when(l == 0)
    def _():
        acck_ref[...] = jnp.zeros_like(acck_ref)
        accv_ref[...] = jnp.zeros_like(accv_ref)

    xb = x_ref[...] + pe_ref[...]
    acck_ref[...] += _dot(xb[:, :KV_W].astype(BF16), w1k_ref[...])
    accv_ref[...] += _dot(xb[:, KV_W:].astype(BF16), w1v_ref[...])

    @pl.when(l == L_CMP - 1)
    def _():
        _compress_out(acck_ref[...], accv_ref[...], w2k_ref, w2vt_ref, kc_ref, vct_ref)


PAGES_PER_STEP = 32
BLOCKS_PER_PAGE = PAGE_SIZE // L_CMP


def _compress_weights(pe, w1k, w2k, w1v, w2v):
    eye = jnp.eye(N_KV_HEADS, dtype=F32)
    bd1 = lambda w1: jax.vmap(lambda w: jnp.kron(eye, w))(w1).astype(BF16)
    pe_row = jnp.tile(pe, (1, 2 * N_KV_HEADS))
    r = jnp.arange(2 * PAGE_SIZE)
    src = (r % 8 // BLOCKS_PER_PAGE) * PAGE_SIZE + (r % BLOCKS_PER_PAGE) * L_CMP + r // 8
    perm = (src[:, None] == r[None, :]).astype(BF16)
    return dict(pe_row=pe_row.reshape(L_CMP, 1, 2 * KV_W), pe_page_t=jnp.tile(pe_row, (BLOCKS_PER_PAGE, 1)).T, perm=perm,
                w1k=bd1(w1k), w1v=bd1(w1v), w2k=jnp.kron(eye, w2k).astype(BF16), w2vt=jnp.kron(eye, w2v).T.astype(BF16))


def _compress(x, w, *, tm):
    m = x.shape[0]
    tm = min(tm, m)
    return pl.pallas_call(
        _compress_kernel,
        grid=(m // tm, L_CMP),
        in_specs=[
            pl.BlockSpec((tm, 2 * KV_W), lambda i, l: (i, l)),
            pl.BlockSpec((None, 1, 2 * KV_W), lambda i, l: (l, 0, 0)),
            pl.BlockSpec((None, KV_W, KV_W), lambda i, l: (l, 0, 0)),
            pl.BlockSpec((None, KV_W, KV_W), lambda i, l: (l, 0, 0)),
            pl.BlockSpec((KV_W, KV_W), lambda i, l: (0, 0)),
            pl.BlockSpec((KV_W, KV_W), lambda i, l: (0, 0)),
        ],
        out_specs=[pl.BlockSpec((tm, KV_W), lambda i, l: (i, 0)), pl.BlockSpec((KV_W, tm), lambda i, l: (0, i))],
        out_shape=[jax.ShapeDtypeStruct((m, KV_W), BF16), jax.ShapeDtypeStruct((KV_W, m), BF16)],
        scratch_shapes=[pltpu.VMEM((tm, KV_W), F32)] * 2,
        compiler_params=_params("parallel", "arbitrary"),
        name="compress",
    )(x, w["pe_row"], w["w1k"], w["w1v"], w["w2k"], w["w2vt"])


def _compress_pages_body(refs, n_pages):
    pages = refs[:n_pages]
    pe_ref, perm_ref, w1k_ref, w1v_ref, w2k_ref, w2vt_ref, kc_ref, vct_ref = refs[n_pages:]
    pe = pe_ref[...]
    perm = perm_ref[...]
    by_l = []
    for p in range(0, n_pages, 2):
        pair_t = jnp.concatenate([(pages[p][...] + pe).astype(BF16), (pages[p + 1][...] + pe).astype(BF16)], axis=1)
        by_l.append(_dot_nt(perm, pair_t))
    acck = jnp.zeros((n_pages * BLOCKS_PER_PAGE, KV_W), F32)
    accv = jnp.zeros((n_pages * BLOCKS_PER_PAGE, KV_W), F32)
    for l in range(L_CMP):
        xl = jnp.concatenate([y[l * 8:(l + 1) * 8] for y in by_l], axis=0).astype(BF16)
        acck = acck + _dot(xl[:, :KV_W], w1k_ref[l])
        accv = accv + _dot(xl[:, KV_W:], w1v_ref[l])
    _compress_out(acck, accv, w2k_ref, w2vt_ref, kc_ref, vct_ref)


def _compress_paged_kernel(pt_ref, *refs, n_pages):
    del pt_ref
    _compress_pages_body(refs, n_pages)


def _compress_seq_kernel(*refs, n_pages):
    _compress_pages_body(refs, n_pages)


def _compress_consts(w):
    return [w["pe_page_t"], w["perm"], w["w1k"], w["w1v"], w["w2k"], w["w2vt"]]


def _compress_paged(cache, page_table, w):
    b, n_pages = page_table.shape
    pps = min(PAGES_PER_STEP, n_pages)
    steps = n_pages // pps
    m_step = pps * BLOCKS_PER_PAGE
    m = b * n_pages * BLOCKS_PER_PAGE
    page = lambda j: pl.BlockSpec((None, 2 * KV_W, PAGE_SIZE), lambda bi, s, pt: (pt[bi, s * pps + j], 0, 0))
    full = lambda a: pl.BlockSpec(a.shape, lambda bi, s, pt: (0,) * a.ndim)
    consts = _compress_consts(w)
    return pl.pallas_call(
        functools.partial(_compress_paged_kernel, n_pages=pps),
        grid_spec=pltpu.PrefetchScalarGridSpec(
            num_scalar_prefetch=1,
            grid=(b, steps),
            in_specs=[page(j) for j in range(pps)] + [full(a) for a in consts],
            out_specs=[pl.BlockSpec((m_step, KV_W), lambda bi, s, pt: (bi * steps + s, 0)),
                       pl.BlockSpec((KV_W, m_step), lambda bi, s, pt: (0, bi * steps + s))],
        ),
        out_shape=[jax.ShapeDtypeStruct((m, KV_W), BF16), jax.ShapeDtypeStruct((KV_W, m), BF16)],
        compiler_params=_params("parallel", "arbitrary"),
        name="compress_paged",
    )(page_table, *([cache] * pps), *consts)


def _compress_seq(rows_t, w):
    b, _, t = rows_t.shape
    per_seq = t // PAGE_SIZE
    n_pages = b * per_seq
    pps = min(PAGES_PER_STEP, n_pages)
    m_step = pps * BLOCKS_PER_PAGE
    m = n_pages * BLOCKS_PER_PAGE
    page = lambda j: pl.BlockSpec((None, 2 * KV_W, PAGE_SIZE),
                                  lambda s: ((s * pps + j) // per_seq, 0, (s * pps + j) % per_seq))
    full = lambda a: pl.BlockSpec(a.shape, lambda s: (0,) * a.ndim)
    consts = _compress_consts(w)
    return pl.pallas_call(
        functools.partial(_compress_seq_kernel, n_pages=pps),
        grid=(n_pages // pps,),
        in_specs=[page(j) for j in range(pps)] + [full(a) for a in consts],
        out_specs=[pl.BlockSpec((m_step, KV_W), lambda s: (s, 0)), pl.BlockSpec((KV_W, m_step), lambda s: (0, s))],
        out_shape=[jax.ShapeDtypeStruct((m, KV_W), BF16), jax.ShapeDtypeStruct((KV_W, m), BF16)],
        compiler_params=_params("parallel"),
        name="compress_seq",
    )(*([rows_t] * pps), *consts)


TQ_PROMPT = 256
TQ_SAMPLE = 128
KCH = 512
ONES_ROWS = 16
BIG = 3.0e38


def _head_queries(q_t, kv):
    zero = jnp.zeros((HEAD_DIM, q_t.shape[1]), BF16)
    cols = []
    for g in range(GROUP):
        h = kv * GROUP + g
        qg = q_t[h * HEAD_DIM:(h + 1) * HEAD_DIM, :]
        cols.append(jnp.concatenate([qg, zero] if kv % 2 == 0 else [zero, qg], axis=0))
    return jnp.concatenate(cols, axis=1)


def _slab(kv):
    return slice((kv // 2) * LANES, (kv // 2 + 1) * LANES)


def _head(kv):
    return slice(kv * HEAD_DIM, (kv + 1) * HEAD_DIM)


def _cmp_branch(kc, vc_t, x, keep):
    s = jnp.where(keep, _dot(kc, x), NEG)
    e = jnp.exp2(s - jnp.max(s, axis=0, keepdims=True))
    l = jnp.sum(e, axis=0, keepdims=True)
    e = jnp.where(keep, e, 0.0)
    return _dot(vc_t, e.astype(BF16)) / l, e / l


def _group_sum(p):
    tq = p.shape[1] // GROUP
    out = p[:, 0:tq]
    for g in range(1, GROUP):
        out = out + p[:, g * tq:(g + 1) * tq]
    return out


def _pair_sum(p):
    n = p.shape[0] // 2
    pair = (lax.broadcasted_iota(jnp.int32, (n, 2 * n), 1) // 2 == lax.broadcasted_iota(jnp.int32, (n, 2 * n), 0))
    return jnp.dot(pair.astype(F32), p, precision=lax.Precision.HIGHEST, preferred_element_type=F32)


def _with_ones(v_t):
    return jnp.concatenate([v_t, jnp.ones((ONES_ROWS, v_t.shape[1]), BF16)], axis=0)


def _topn_rows(score, n):
    nb = score.shape[0]
    row = lax.broadcasted_iota(jnp.int32, score.shape, 0).astype(F32)
    picked = jnp.zeros(score.shape, F32)
    ids = []
    for _ in range(n):
        top = jnp.max(score, axis=0, keepdims=True)
        first = jnp.min(jnp.where(score == top, row, float(nb)), axis=0, keepdims=True)
        hit = row == first
        picked = jnp.where(hit, 1.0, picked)
        score = jnp.where(hit, -jnp.inf, score)
        ids.append(first)
    return picked, ids


def _masked_scores(s, cap):
    tq = cap.shape[1]
    s = jnp.concatenate([jnp.minimum(s[:, g * tq:(g + 1) * tq], cap) for g in range(GROUP)], axis=1)
    return s, jnp.max(s, axis=0, keepdims=True)


def _softmax_pv(s, top, v_t):
    o = _dot(_with_ones(v_t), jnp.exp2(s - top).astype(BF16))
    return o[:HEAD_DIM] / o[HEAD_DIM:HEAD_DIM + 1]


def _gate_mix(g_t, kv, branches):
    tq = g_t.shape[1]
    out = []
    for g in range(GROUP):
        h = kv * GROUP + g
        c = slice(g * tq, (g + 1) * tq)
        acc = None
        for br, o in branches:
            term = g_t[br * N_HEADS + h:br * N_HEADS + h + 1, :] * o[:, c]
            acc = term if acc is None else acc + term
        out.append(acc)
    return out


def _attn_prompt_kernel(x_ref, qt_ref, gt_ref, kc_ref, vct_ref, ks_ref, vst_ref, kw_ref, *rest, t_len, tq):
    win_blocks = WINDOW // tq + 1
    cols = GROUP * tq
    vwt_refs = rest[:win_blocks]
    wout_ref, o_ref, m_ref, acc_ref = rest[win_blocks:]
    t0 = pl.program_id(1) * tq
    n_sel = t_len // L_SEL
    n_cmp = t_len // L_CMP
    t_tok = t0 + lax.broadcasted_iota(jnp.int32, (1, tq), 1)
    t_col = t0 + lax.broadcasted_iota(jnp.int32, (1, cols), 1) % tq
    t_heads = t0 + lax.broadcasted_iota(jnp.int32, (1, N_KV_HEADS * tq), 1) % tq
    r_c = lax.broadcasted_iota(jnp.int32, (n_cmp, 1), 0)
    keep_c = (r_c + 1) * L_CMP - 1 <= t_col
    blk = lax.broadcasted_iota(jnp.int32, (n_sel, N_KV_HEADS * tq), 0)
    cur = t_heads // L_SEL
    forced = (blk == 0) | (blk == cur) | (blk == cur - 1)
    valid = blk <= cur
    key_in_blk = lax.broadcasted_iota(jnp.int32, (L_SEL, 1), 0)
    win_pos = t0 - WINDOW + lax.broadcasted_iota(jnp.int32, (win_blocks * tq, 1), 0)
    dist = t_tok - win_pos
    cap_w = jnp.where((dist >= 0) & (dist < WINDOW) & (win_pos >= 0), BIG, NEG)
    n_chunks = t_len // KCH
    last_chunk = (t0 + tq - 1) // KCH
    q_t = qt_ref[...]
    g_t = gt_ref[...]

    o_cmp, imp = [], []
    for kv in range(N_KV_HEADS):
        o_c, p = _cmp_branch(kc_ref[0, :, _slab(kv)], vct_ref[0, _head(kv), :], _head_queries(q_t, kv), keep_c)
        o_cmp.append(o_c)
        imp.append(_group_sum(p))
    score = jnp.where(forced, FORCE_BONUS, jnp.where(valid, _pair_sum(jnp.concatenate(imp, axis=1)), -FORCE_BONUS))
    picked, _ = _topn_rows(score, min(TOP_N, n_sel))
    sel_cap = jnp.where(valid & (picked > 0.5), BIG, NEG)

    def key_caps(kv, first_key, n_keys):
        rows = []
        for j in range(first_key // L_SEL, (first_key + n_keys) // L_SEL):
            causal = j * L_SEL + key_in_blk <= t_tok
            rows.append(jnp.where(causal, jnp.broadcast_to(sel_cap[j:j + 1, kv * tq:(kv + 1) * tq], (L_SEL, tq)), NEG))
        return jnp.concatenate(rows, axis=0)

    win = []
    for kv in range(N_KV_HEADS):
        x = _head_queries(q_t, kv)
        s_w = []
        for j in range(win_blocks):
            start = pl.multiple_of(jnp.maximum(t0 - WINDOW + j * tq, 0), tq)
            s_w.append(_dot(kw_ref[0, pl.ds(start, tq), _slab(kv)], x))
        win.append(_masked_scores(jnp.concatenate(s_w, axis=0), cap_w))
    o_win = [_softmax_pv(*win[kv], jnp.concatenate([r[_head(kv), :] for r in vwt_refs], axis=1))
             for kv in range(N_KV_HEADS)]

    m_ref[...] = jnp.full(m_ref.shape, NEG, F32)
    acc_ref[...] = jnp.zeros_like(acc_ref)

    def chunk(c, n_keys):
        keys = slice(c * KCH, c * KCH + n_keys)
        scores, maxes = [], []
        for kv in range(N_KV_HEADS):
            s = _dot(ks_ref[0, keys, _slab(kv)], _head_queries(q_t, kv))
            cap = key_caps(kv, c * KCH, n_keys)
            s = jnp.concatenate([jnp.minimum(s[:, g * tq:(g + 1) * tq], cap) for g in range(GROUP)], axis=1)
            scores.append(s)
            maxes.append(jnp.maximum(m_ref[kv], jnp.max(s, axis=0, keepdims=True)))
        for kv in range(N_KV_HEADS):
            e = jnp.exp2(scores[kv] - maxes[kv]).astype(BF16)
            acc_ref[kv] = (jnp.exp2(m_ref[kv] - maxes[kv]) * acc_ref[kv]
                           + _dot(_with_ones(vst_ref[_head(kv), keys]), e))
            m_ref[kv] = maxes[kv]

    chunk(0, KCH)
    assert tq >= KCH or KCH == 2 * tq
    ends_chunk = (t0 + tq) % KCH == 0
    for c in range(1, n_chunks):
        pl.when((c < last_chunk) | ((c == last_chunk) & ends_chunk))(functools.partial(chunk, c, KCH))
        if tq < KCH:
            pl.when((c == last_chunk) & jnp.logical_not(ends_chunk))(functools.partial(chunk, c, tq))

    heads = []
    for kv in range(N_KV_HEADS):
        acc = acc_ref[kv]
        o_s = acc[:HEAD_DIM] / acc[HEAD_DIM:HEAD_DIM + 1]
        heads += _gate_mix(g_t, kv, ((0, o_cmp[kv]), (1, o_s), (2, o_win[kv])))
    o = jnp.concatenate(heads, axis=0).T.astype(BF16)
    o_ref[0] = x_ref[0] + _dot(o, wout_ref[...])


def _attn_prompt(x, q_t, g_t, kc, vc_t, ks, vs_t, kw, vw_t, w_out, *, tq=TQ_PROMPT):
    b, t, _ = x.shape
    nt = t // tq
    n_cmp = t // L_CMP
    win_blocks = WINDOW // tq + 1
    cols = GROUP * tq
    tile = lambda w: pl.BlockSpec((1, tq, w), lambda bi, i: (bi, i, 0))
    tile_t = lambda h: pl.BlockSpec((h, tq), lambda bi, i: (0, bi * nt + i))
    win_t = lambda j: pl.BlockSpec((KV_W, tq), lambda bi, i: (0, bi * nt + jnp.maximum(i - (win_blocks - 1) + j, 0)))
    return pl.pallas_call(
        functools.partial(_attn_prompt_kernel, t_len=t, tq=tq),
        grid=(b, nt),
        in_specs=[tile(D_MODEL), tile_t(N_Q), tile_t(LANES),
                  pl.BlockSpec((1, n_cmp, KV_W), lambda bi, i: (bi, 0, 0)),
                  pl.BlockSpec((1, KV_W, n_cmp), lambda bi, i: (bi, 0, 0)),
                  pl.BlockSpec((1, t, KV_W), lambda bi, i: (bi, 0, 0)),
                  pl.BlockSpec((KV_W, t), lambda bi, i: (0, bi)),
                  pl.BlockSpec((1, t, KV_W), lambda bi, i: (bi, 0, 0))]
        + [win_t(j) for j in range(win_blocks)]
        + [pl.BlockSpec((N_Q, D_MODEL), lambda bi, i: (0, 0))],
        out_specs=tile(D_MODEL),
        out_shape=jax.ShapeDtypeStruct((b, t, D_MODEL), F32),
        scratch_shapes=[pltpu.VMEM((N_KV_HEADS, 1, cols), F32),
                        pltpu.VMEM((N_KV_HEADS, HEAD_DIM + ONES_ROWS, cols), F32)],
        compiler_params=_params("parallel", "arbitrary"),
        name="attn_prompt",
    )(x, q_t, g_t, kc, vc_t, ks, vs_t, kw, *([vw_t] * win_blocks), w_out)


NEW_ROWS = 16


def _attn_sample_kernel(qt_ref, gt_ref, kc_ref, vct_ref, win_ref, new_ref, part_ref, idx_ref, *, past, n_blk, half):
    tq = TQ_SAMPLE
    t_tok = past + lax.broadcasted_iota(jnp.int32, (1, tq), 1)
    t_col = past + lax.broadcasted_iota(jnp.int32, (1, GROUP * tq), 1) % tq
    t_heads = past + lax.broadcasted_iota(jnp.int32, (1, N_KV_HEADS * tq), 1) % tq
    r_c = lax.broadcasted_iota(jnp.int32, (2 * half, 1), 0)
    keep_c = (r_c < 2 * n_blk) & ((r_c + 1) * L_CMP - 1 <= t_col)
    blk = lax.broadcasted_iota(jnp.int32, (half, N_KV_HEADS * tq), 0)
    cur = t_heads // L_SEL
    forced = (blk == 0) | (blk == cur) | (blk == cur - 1)
    valid = blk <= cur
    wb = win_ref.shape[2]
    win_pos = past - wb + lax.broadcasted_iota(jnp.int32, (wb + NEW_ROWS, 1), 0)
    dist = t_tok - win_pos
    keep_w = (dist >= 0) & (dist < WINDOW) & (win_pos >= 0)
    q_t = qt_ref[0]
    g_t = gt_ref[0]
    o_cmp, imp = [], []
    for kv in range(N_KV_HEADS):
        o_c, p = _cmp_branch(kc_ref[0, :, _slab(kv)], vct_ref[0, _head(kv), :], _head_queries(q_t, kv), keep_c)
        o_cmp.append(o_c)
        imp.append(_group_sum(p))
    score = jnp.where(forced, FORCE_BONUS, jnp.where(valid, _pair_sum(jnp.concatenate(imp, axis=1)), -FORCE_BONUS))
    score = jnp.where(blk < n_blk, score, -jnp.inf)
    _, ids = _topn_rows(score, min(TOP_N, n_blk))
    for j, first in enumerate(ids):
        for kv in range(N_KV_HEADS):
            idx_ref[0, kv, j:j + 1, :] = first[:, kv * tq:(kv + 1) * tq].astype(jnp.int32)

    heads = []
    for kv in range(N_KV_HEADS):
        x = _head_queries(q_t, kv)
        mine = slice((kv % 2) * HEAD_DIM, (kv % 2 + 1) * HEAD_DIM)
        v_slab = slice(KV_W + (kv // 2) * LANES, KV_W + (kv // 2 + 1) * LANES)
        s = jnp.concatenate([_dot_tn(win_ref[0, _head(kv), :], x[mine].astype(F32)),
                             _dot(new_ref[0, :, _slab(kv)].astype(BF16), x)], axis=0)
        s = jnp.concatenate([jnp.where(keep_w, s[:, g * tq:(g + 1) * tq], NEG) for g in range(GROUP)], axis=1)
        e = jnp.exp2(s - jnp.max(s, axis=0, keepdims=True))
        p_w = e / jnp.sum(e, axis=0, keepdims=True)
        v_t = win_ref[0, KV_W + kv * HEAD_DIM:KV_W + (kv + 1) * HEAD_DIM, :]
        o_w = _dot(v_t.astype(BF16), p_w[:wb].astype(BF16)) + _dot_tn(new_ref[0, :, v_slab], p_w[wb:])[mine]
        heads += _gate_mix(g_t, kv, ((0, o_cmp[kv]), (2, o_w)))
    part_ref[0] = jnp.concatenate(heads, axis=0)


def _attn_sample(q_t, g_t, kc, vc_t, win, new, *, past, n_blk):
    b = q_t.shape[0]
    half = kc.shape[1] // 2
    n_top = min(TOP_N, n_blk)
    blk3 = lambda a: pl.BlockSpec((1,) + a.shape[1:], lambda bi: (bi, 0, 0))
    return pl.pallas_call(
        functools.partial(_attn_sample_kernel, past=past, n_blk=n_blk, half=half),
        grid=(b,),
        in_specs=[blk3(q_t), blk3(g_t), blk3(kc), blk3(vc_t), blk3(win), blk3(new)],
        out_specs=[pl.BlockSpec((1, N_Q, TQ_SAMPLE), lambda bi: (bi, 0, 0)),
                   pl.BlockSpec((1, N_KV_HEADS, n_top, TQ_SAMPLE), lambda bi: (bi, 0, 0, 0))],
        out_shape=[jax.ShapeDtypeStruct((b, N_Q, TQ_SAMPLE), F32),
                   jax.ShapeDtypeStruct((b, N_KV_HEADS, n_top, TQ_SAMPLE), jnp.int32)],
        compiler_params=_params("parallel"),
        name="attn_sample",
    )(q_t, g_t, kc, vc_t, win, new)


Q_ROWS = 16


def _sel_sample_kernel(pt_ref, bid_ref, q_ref, new_ref, cache_ref, o_ref, slab_buf, sem, *,
                       past, n_past_blk, n_top):
    n_slab = N_KV_HEADS * n_top
    n_t = pl.num_programs(1)
    step = pl.program_id(0) * n_t + pl.program_id(1)
    n_steps = pl.num_programs(0) * n_t
    blocks_per_page = PAGE_SIZE // L_SEL

    def slab_copies(at_step):
        slot = at_step % 2
        copies = []
        for i in range(n_slab):
            past_blk = jnp.minimum(bid_ref[at_step * n_slab + i], n_past_blk - 1)
            page = pt_ref[at_step // n_t, past_blk // blocks_per_page]
            copies.append(pltpu.make_async_copy(cache_ref.at[page, :, i // n_top], slab_buf.at[slot, i], sem.at[slot]))
        return copies

    @pl.when(step == 0)
    def _():
        for copy in slab_copies(step):
            copy.start()

    @pl.when(step + 1 < n_steps)
    def _():
        for copy in slab_copies(step + 1):
            copy.start()

    for copy in slab_copies(step):
        copy.wait()

    slot = step % 2
    t_pos = past + pl.program_id(1)
    lane = lax.broadcasted_iota(jnp.int32, (1, PAGE_SIZE), 1)
    for kv in range(N_KV_HEADS):
        k_t, v_t, keep = [], [], []
        for j in range(n_top):
            bid = bid_ref[step * n_slab + kv * n_top + j]
            from_past = bid < n_past_blk
            first_lane = jnp.where(from_past, (bid % blocks_per_page) * L_SEL, 0)
            k_t.append(jnp.where(from_past, slab_buf[slot, kv * n_top + j, 0], new_ref[0, kv]).astype(BF16))
            v_t.append(jnp.where(from_past, slab_buf[slot, kv * n_top + j, 1], new_ref[1, kv]).astype(BF16))
            in_blk = lane - first_lane
            keep.append((in_blk >= 0) & (in_blk < L_SEL) & (bid * L_SEL + in_blk <= t_pos))
        s = jnp.where(jnp.concatenate(keep, axis=1), _dot(q_ref[kv], jnp.concatenate(k_t, axis=1)), NEG)
        e = jnp.exp2(s - jnp.max(s, axis=1, keepdims=True))
        p = e / jnp.sum(e, axis=1, keepdims=True)
        o_ref[kv] = _dot_nt(p.astype(BF16), jnp.concatenate(v_t, axis=1))


def _sel_sample(page_table, blk_ids, q, cache, new_slabs, *, past, n_past_blk, n_top):
    b, t, n_kv = q.shape[:3]
    return pl.pallas_call(
        functools.partial(_sel_sample_kernel, past=past, n_past_blk=n_past_blk, n_top=n_top),
        grid_spec=pltpu.PrefetchScalarGridSpec(
            num_scalar_prefetch=2,
            grid=(b, t),
            in_specs=[pl.BlockSpec((None, None, n_kv, Q_ROWS, HEAD_DIM), lambda bi, ti, pt, bid: (bi, ti, 0, 0, 0)),
                      pl.BlockSpec((None, 2, n_kv, HEAD_DIM, PAGE_SIZE), lambda bi, ti, pt, bid: (bi, 0, 0, 0, 0)),
                      pl.BlockSpec(memory_space=pl.ANY)],
            out_specs=pl.BlockSpec((None, None, n_kv, Q_ROWS, HEAD_DIM), lambda bi, ti, pt, bid: (bi, ti, 0, 0, 0)),
            scratch_shapes=[pltpu.VMEM((2, n_kv * n_top, 2, HEAD_DIM, PAGE_SIZE), F32), pltpu.SemaphoreType.DMA((2,))],
        ),
        out_shape=jax.ShapeDtypeStruct((b, t, n_kv, Q_ROWS, HEAD_DIM), F32),
        compiler_params=_params("arbitrary", "arbitrary"),
        name="sel_sample",
    )(page_table, blk_ids, q, new_slabs, cache)


def _out_sample_kernel(x_ref, part_ref, os_ref, gs_ref, wout_ref, o_ref):
    o = (part_ref[...] + gs_ref[...] * os_ref[...]).astype(BF16)
    o_ref[...] = x_ref[...] + _dot(o, wout_ref[...])


def _out_sample(x, part, o_s, g_s, w_out):
    return pl.pallas_call(
        _out_sample_kernel,
        out_shape=jax.ShapeDtypeStruct(x.shape, F32),
        compiler_params=pltpu.CompilerParams(vmem_limit_bytes=VMEM_LIMIT),
        name="out_sample",
    )(x, part, o_s, g_s, w_out)


def kernel(x_prompt, x_sample, state_conv, cache_cmp_kv, cache_sel_kv, cache_win_kv, page_table, norm_ffa, w_ffa_gu, w_ffa_down, norm_mix, norm_ffb, w_ffb_gu, w_ffb_down, w_conv_in, w_conv, w_conv_out, w_nsa_in, pe_cmp, w_cmp_k1, w_cmp_k2, w_cmp_v1, w_cmp_v2, w_nsa_out, norm_final):
    bp, tp, _ = x_prompt.shape
    bs, ts, _ = x_sample.shape
    n_p, n_s = bp * tp, bs * ts
    xp = x_prompt.reshape(n_p, D_MODEL)
    xs = x_sample.reshape(n_s, D_MODEL)

    def ffn_pair(xp, xs, g, w_gu, w_down, layer, final=False):
        w_gu, w_down = _to_bf16(w_gu, layer), _to_bf16(w_down, layer)
        return (_ffn(xp, g, w_gu, w_down, norm_final, tm=512, final=final),
                _ffn(xs, g, w_gu, w_down, norm_final, tm=n_s, final=final))

    xp, xs = ffn_pair(xp, xs, norm_ffa[0], w_ffa_gu, w_ffa_down, 0)
    w_in, w_out = _to_bf16(w_conv_in, 0), _to_bf16(w_conv_out, 0)
    xp3, conv_p = _conv_prompt(xp.reshape(bp, tp, D_MODEL), norm_mix[0], w_in, w_conv[0], w_out)
    xs3, conv_s = _conv_sample(xs.reshape(bs, ts, D_MODEL), state_conv[0], norm_mix[0], w_in, w_conv[0], w_out)
    xp, xs = ffn_pair(xp3.reshape(n_p, D_MODEL), xs3.reshape(n_s, D_MODEL), norm_ffb[0], w_ffb_gu, w_ffb_down, 0)

    xp, xs = ffn_pair(xp, xs, norm_ffa[1], w_ffa_gu, w_ffa_down, 1)
    proj_w = _nsa_weights(w_nsa_in[0])
    cmp_w = _compress_weights(pe_cmp[0], w_cmp_k1[0], w_cmp_k2[0], w_cmp_v1[0], w_cmp_v2[0])
    w_out = _to_bf16(w_nsa_out, 0)
    kv_shape = (2, N_KV_HEADS, HEAD_DIM)
    row_w = 2 * KV_W

    q_t, g_t, kvc_t, kvs_t, kvw_t, ks, kw, vs_t, vw_t = _nsa_project_prompt(xp, norm_mix[1], proj_w, seq_len=tp, tm=512)
    n_cmp = tp // L_CMP
    kc, vc_t = _compress_seq(kvc_t, cmp_w)
    kc = kc.reshape(bp, n_cmp, KV_W)
    vc_t = vc_t.reshape(KV_W, bp, n_cmp).transpose(1, 0, 2)
    seq = lambda a: a.reshape(bp, tp, a.shape[-1])
    xp3 = _attn_prompt(seq(xp), q_t, g_t, kc, vc_t, seq(ks), vs_t, seq(kw), vw_t, w_out)
    by_row = lambda a: a.reshape((bp,) + kv_shape + (a.shape[-1],)).transpose(0, 4, 1, 2, 3)[None]
    cmp_p = by_row(kvc_t)
    sel_p = by_row(kvs_t)
    win_p = by_row(kvw_t[:, :, tp - min(WINDOW, tp):])

    q_t, g_t, kvc, kvs, kvw = _nsa_project_sample(xs, norm_mix[1], proj_w)
    n_pages = page_table.shape[1]
    past = n_pages * PAGE_SIZE
    n_new = -(-ts // L_SEL) * L_SEL
    n_past_blk = past // L_SEL
    n_blk = n_past_blk + n_new // L_SEL
    n_top = min(TOP_N, n_blk)
    lanes = lambda a: jnp.pad(a.reshape(a.shape[0], bs, ts).transpose(1, 0, 2), ((0, 0), (0, 0), (0, TQ_SAMPLE - ts)))
    by_channel = lambda c: c.transpose(0, 2, 3, 4, 1)
    kc_past, vct_past = _compress_paged(by_channel(cache_cmp_kv[0]).reshape(-1, row_w, PAGE_SIZE), page_table, cmp_w)
    new_c = jnp.pad(kvc.reshape(bs, ts, row_w), ((0, 0), (0, n_new - ts), (0, 0)))
    kc_new, vct_new = _compress(new_c.reshape(bs * n_new // L_CMP, L_CMP * row_w), cmp_w, tm=bs * n_new // L_CMP)
    half = -(-n_blk // (LANES // 2)) * (LANES // 2)
    pad = 2 * half - 2 * n_blk
    kc_all = jnp.pad(jnp.concatenate([kc_past.reshape(bs, -1, KV_W), kc_new.reshape(bs, -1, KV_W)], axis=1),
                     ((0, 0), (0, pad), (0, 0)))
    vct_all = jnp.pad(jnp.concatenate([vct_past.reshape(KV_W, bs, -1), vct_new.reshape(KV_W, bs, -1)], axis=2),
                      ((0, 0), (0, 0), (0, pad)))
    wb = cache_win_kv.shape[2]
    new_w = jnp.pad(kvw.reshape(bs, ts, row_w), ((0, 0), (0, NEW_ROWS - ts), (0, 0)))
    part, idx = _attn_sample(lanes(q_t), lanes(g_t), kc_all, vct_all.transpose(1, 0, 2),
                             by_channel(cache_win_kv[0]).reshape(bs, row_w, wb), new_w, past=past, n_blk=n_blk)
    bid = idx[..., :ts].transpose(0, 3, 1, 2)
    q5 = q_t.reshape(N_KV_HEADS, GROUP, HEAD_DIM, bs, ts).transpose(3, 4, 0, 1, 2)
    q5 = jnp.pad(q5, ((0, 0),) * 3 + ((0, Q_ROWS - GROUP), (0, 0)))
    new_s = jnp.pad(kvs.reshape((bs, ts) + kv_shape).transpose(0, 2, 3, 4, 1), ((0, 0),) * 4 + ((0, PAGE_SIZE - ts),))
    o_s = _sel_sample(page_table, bid.reshape(-1), q5, by_channel(cache_sel_kv[0]), new_s,
                      past=past, n_past_blk=n_past_blk, n_top=n_top)
    o_s = o_s[:, :, :, :GROUP].reshape(n_s, N_Q)
    g_s = jnp.repeat(g_t[N_HEADS:2 * N_HEADS].T, HEAD_DIM, axis=1)
    part = part[:, :, :ts].transpose(0, 2, 1).reshape(n_s, N_Q)
    xs = _out_sample(xs, part, o_s, g_s, w_out)
    cmp_s = kvc.reshape((1, bs, ts) + kv_shape)
    sel_s = kvs.reshape((1, bs, ts) + kv_shape)
    win_s = jnp.concatenate([cache_win_kv[0], kvw.reshape((bs, ts) + kv_shape)], axis=1)[None, :, ts:]

    xp, xs = ffn_pair(xp3.reshape(n_p, D_MODEL), xs, norm_ffb[1], w_ffb_gu, w_ffb_down, 1, final=True)
    return (xp.reshape(bp, tp, D_MODEL), xs.reshape(bs, ts, D_MODEL),
            conv_p[None], conv_s[None], cmp_p, cmp_s, sel_p, sel_s, win_p, win_s)
```

```python
import functools

import jax
import jax.numpy as jnp
from jax import lax
from jax.experimental import pallas as pl
from jax.experimental.pallas import tpu as pltpu

D_MODEL = 1024
D_FF = 2816
CONV_W = 3
N_HEADS = 16
HEAD_DIM = 64
N_KV_HEADS = 4
GROUP = N_HEADS // N_KV_HEADS
KV_W = N_KV_HEADS * HEAD_DIM
L_CMP = 32
L_SEL = 64
TOP_N = 16
WINDOW = 512
PAGE_SIZE = 128
NORM_EPS = 1e-6
FORCE_BONUS = 1e4
NEG = -1e30

LANES = 128
VMEM_LIMIT = 56 * 1024 * 1024

F32 = jnp.float32
BF16 = jnp.bfloat16


def _params(*sem):
    return pltpu.CompilerParams(dimension_semantics=sem, vmem_limit_bytes=VMEM_LIMIT)


def _rms(x, g):
    return x * lax.rsqrt(jnp.mean(x * x, axis=-1, keepdims=True) + NORM_EPS) * g


def _dot(a, b):
    return jnp.dot(a, b, preferred_element_type=F32)


def _dot_nt(a, b):
    return lax.dot_general(a, b, (((1,), (1,)), ((), ())), preferred_element_type=F32)


def _dot_tn(a, b):
    return lax.dot_general(a, b, (((0,), (0,)), ((), ())), preferred_element_type=F32)


def _cast_kernel(w_ref, o_ref):
    o_ref[...] = w_ref[...].astype(BF16)


def _to_bf16(w, layer, *, steps=4):
    _, r, c = w.shape
    tr = r // steps
    return pl.pallas_call(
        _cast_kernel,
        grid=(steps,),
        in_specs=[pl.BlockSpec((None, tr, c), lambda i: (layer, i, 0))],
        out_specs=pl.BlockSpec((tr, c), lambda i: (i, 0)),
        out_shape=jax.ShapeDtypeStruct((r, c), BF16),
        compiler_params=_params("parallel"),
        name="to_bf16",
    )(w)


def _ffn_kernel(x_ref, g_ref, wg_ref, wu_ref, wd_ref, gf_ref, o_ref, *, final):
    x = x_ref[...]
    xn = _rms(x, g_ref[...]).astype(BF16)
    gate = _dot(xn, wg_ref[...])
    up = _dot(xn, wu_ref[...])
    act = (gate * jax.nn.sigmoid(gate) * up).astype(BF16)
    y = x + 0.5 * _dot(act, wd_ref[...])
    o_ref[...] = _rms(y, gf_ref[...]) if final else y


def _resident(shape, index):
    return pl.BlockSpec(shape, index, pipeline_mode=pl.Buffered(1))


def _ffn(x, g, w_gu, w_down, g_final, *, tm, final=False):
    n = x.shape[0]
    return pl.pallas_call(
        functools.partial(_ffn_kernel, final=final),
        grid=(n // tm,),
        in_specs=[
            pl.BlockSpec((tm, D_MODEL), lambda i: (i, 0)),
            pl.BlockSpec((1, D_MODEL), lambda i: (0, 0)),
            _resident((D_MODEL, D_FF), lambda i: (0, 0)),
            _resident((D_MODEL, D_FF), lambda i: (0, 1)),
            _resident((D_FF, D_MODEL), lambda i: (0, 0)),
            pl.BlockSpec((1, D_MODEL), lambda i: (0, 0)),
        ],
        out_specs=pl.BlockSpec((tm, D_MODEL), lambda i: (i, 0)),
        out_shape=jax.ShapeDtypeStruct((n, D_MODEL), F32),
        compiler_params=_params("parallel"),
        name="ffn_final" if final else "ffn",
    )(x, g.reshape(1, D_MODEL), w_gu, w_gu, w_down, g_final.reshape(1, D_MODEL))


CARRY = 8


def _conv_prompt_kernel(x_ref, g_ref, win_ref, wc_ref, wout_ref, o_ref, st_ref, uext_ref, *, tm):
    @pl.when(pl.program_id(1) == 0)
    def _():
        uext_ref[0:CARRY, :] = jnp.zeros((CARRY, D_MODEL), F32)

    x = x_ref[0]
    h = _rms(x, g_ref[...]).astype(BF16)
    p = _dot(h, win_ref[...])
    bg = p[:, :D_MODEL]
    u = p[:, D_MODEL:2 * D_MODEL] * p[:, 2 * D_MODEL:]
    uext_ref[CARRY:CARRY + tm, :] = u
    wc = wc_ref[...]
    conv = (wc[0:1] * uext_ref[CARRY - 2:CARRY - 2 + tm, :]
            + wc[1:2] * uext_ref[CARRY - 1:CARRY - 1 + tm, :]
            + wc[2:3] * u)
    y = _dot((bg * conv).astype(BF16), wout_ref[...])
    o_ref[0] = x + y
    st_ref[0] = uext_ref[CARRY + tm - 2:CARRY + tm, :]
    uext_ref[0:CARRY, :] = uext_ref[tm:tm + CARRY, :]


def _conv_prompt(x, g, w_in, w_conv, w_out, *, tm=1024):
    b, t, _ = x.shape
    return pl.pallas_call(
        functools.partial(_conv_prompt_kernel, tm=tm),
        grid=(b, t // tm),
        in_specs=[
            pl.BlockSpec((1, tm, D_MODEL), lambda bi, ti: (bi, ti, 0)),
            pl.BlockSpec((1, D_MODEL), lambda bi, ti: (0, 0)),
            pl.BlockSpec((D_MODEL, 3 * D_MODEL), lambda bi, ti: (0, 0)),
            pl.BlockSpec((CONV_W, D_MODEL), lambda bi, ti: (0, 0)),
            pl.BlockSpec((D_MODEL, D_MODEL), lambda bi, ti: (0, 0)),
        ],
        out_specs=[
            pl.BlockSpec((1, tm, D_MODEL), lambda bi, ti: (bi, ti, 0)),
            pl.BlockSpec((1, CONV_W - 1, D_MODEL), lambda bi, ti: (bi, 0, 0)),
        ],
        out_shape=[jax.ShapeDtypeStruct((b, t, D_MODEL), F32),
                   jax.ShapeDtypeStruct((b, CONV_W - 1, D_MODEL), F32)],
        scratch_shapes=[pltpu.VMEM((CARRY + tm, D_MODEL), F32)],
        compiler_params=_params("parallel", "arbitrary"),
        name="conv_prompt",
    )(x, g.reshape(1, D_MODEL), w_in, w_conv, w_out)


def _conv_sample_kernel(x_ref, g_ref, win_ref, wc_ref, wout_ref, s1_ref, s2_ref, o_ref, u_ref, *, t_len):
    x = x_ref[...]
    h = _rms(x, g_ref[...]).astype(BF16)
    p = _dot(h, win_ref[...])
    bg = p[:, :D_MODEL]
    u = p[:, D_MODEL:2 * D_MODEL] * p[:, 2 * D_MODEL:]
    pos = lax.broadcasted_iota(jnp.int32, u.shape, 0) % t_len
    u1 = jnp.where(pos >= 1, pltpu.roll(u, 1, axis=0), s1_ref[...])
    u2 = jnp.where(pos >= 2, pltpu.roll(u, 2, axis=0), s2_ref[...])
    wc = wc_ref[...]
    conv = wc[0:1] * u2 + wc[1:2] * u1 + wc[2:3] * u
    o_ref[...] = x + _dot((bg * conv).astype(BF16), wout_ref[...])
    u_ref[...] = u


def _conv_sample(x, state, g, w_in, w_conv, w_out):
    b, t, _ = x.shape
    n = b * t
    zeros = jnp.zeros((b, t, D_MODEL), F32)
    s1 = zeros.at[:, 0].set(state[:, 1]).reshape(n, D_MODEL)
    s2 = zeros.at[:, 0].set(state[:, 0]).at[:, 1].set(state[:, 1]).reshape(n, D_MODEL)
    y, u = pl.pallas_call(
        functools.partial(_conv_sample_kernel, t_len=t),
        out_shape=[jax.ShapeDtypeStruct((n, D_MODEL), F32), jax.ShapeDtypeStruct((n, D_MODEL), F32)],
        compiler_params=pltpu.CompilerParams(vmem_limit_bytes=VMEM_LIMIT),
        name="conv_sample",
    )(x.reshape(n, D_MODEL), g.reshape(1, D_MODEL), w_in, w_conv, w_out, s1, s2)
    return y.reshape(b, t, D_MODEL), u.reshape(b, t, D_MODEL)[:, t - (CONV_W - 1):]


N_Q = N_HEADS * HEAD_DIM
Q_SCALE = HEAD_DIM ** -0.5 * 1.4426950408889634


def _proj_common(x_ref, g_ref, wqt_ref, wgt_ref, qt_ref, gt_ref):
    h = _rms(x_ref[...], g_ref[...]).astype(BF16)
    qt_ref[...] = (_dot_nt(wqt_ref[...], h) * Q_SCALE).astype(BF16)
    gt_ref[...] = jax.nn.sigmoid(_dot_nt(wgt_ref[...], h))
    return h


def _proj_prompt_kernel(x_ref, g_ref, wqt_ref, wgt_ref, wkvt_ref, wk_ref,
                        qt_ref, gt_ref, kvct_ref, kvst_ref, kvwt_ref, ks_ref, kw_ref, vst_ref, vwt_ref):
    h = _proj_common(x_ref, g_ref, wqt_ref, wgt_ref, qt_ref, gt_ref)
    kv_t = _dot_nt(wkvt_ref[...], h)
    kvct_ref[0] = kv_t[:2 * KV_W]
    kvst_ref[0] = kv_t[2 * KV_W:4 * KV_W]
    kvwt_ref[0] = kv_t[4 * KV_W:]
    vst_ref[...] = kv_t[3 * KV_W:4 * KV_W].astype(BF16)
    vwt_ref[...] = kv_t[5 * KV_W:].astype(BF16)
    k = _dot(h, wk_ref[...]).astype(BF16)
    ks_ref[...] = k[:, :KV_W]
    kw_ref[...] = k[:, KV_W:]


def _proj_sample_kernel(x_ref, g_ref, wqt_ref, wgt_ref, wkv_ref, qt_ref, gt_ref, kvc_ref, kvs_ref, kvw_ref):
    h = _proj_common(x_ref, g_ref, wqt_ref, wgt_ref, qt_ref, gt_ref)
    kv = _dot(h, wkv_ref[...])
    kvc_ref[...] = kv[:, :2 * KV_W]
    kvs_ref[...] = kv[:, 2 * KV_W:4 * KV_W]
    kvw_ref[...] = kv[:, 4 * KV_W:]


def _nsa_weights(w_in):
    wkv = w_in[:, N_Q:N_Q + 6 * KV_W]
    wk = jnp.concatenate([wkv[:, 2 * KV_W:3 * KV_W], wkv[:, 4 * KV_W:5 * KV_W]], axis=1)
    wgt = jnp.pad(w_in[:, N_Q + 6 * KV_W:], ((0, 0), (0, LANES - 3 * N_HEADS))).T
    return dict(wqt=w_in[:, :N_Q].T.astype(BF16), wgt=wgt.astype(BF16), wkv=wkv.astype(BF16),
                wkvt=wkv.T.astype(BF16), wk=wk.astype(BF16))


def _nsa_project_prompt(x, g, w, *, seq_len, tm):
    n = x.shape[0]
    per_seq = seq_len // tm
    row = lambda w_: pl.BlockSpec((tm, w_), lambda i: (i, 0))
    col = lambda h: pl.BlockSpec((h, tm), lambda i: (0, i))
    seq_t = pl.BlockSpec((1, 2 * KV_W, tm), lambda i: (i // per_seq, 0, i % per_seq))
    full = lambda a: pl.BlockSpec(a.shape, lambda i: (0, 0))
    weights = [w["wqt"], w["wgt"], w["wkvt"], w["wk"]]
    return pl.pallas_call(
        _proj_prompt_kernel,
        grid=(n // tm,),
        in_specs=[row(D_MODEL), pl.BlockSpec((1, D_MODEL), lambda i: (0, 0))] + [full(a) for a in weights],
        out_specs=[col(N_Q), col(LANES), seq_t, seq_t, seq_t, row(KV_W), row(KV_W), col(KV_W), col(KV_W)],
        out_shape=[jax.ShapeDtypeStruct((N_Q, n), BF16), jax.ShapeDtypeStruct((LANES, n), F32)]
        + [jax.ShapeDtypeStruct((n // seq_len, 2 * KV_W, seq_len), F32)] * 3
        + [jax.ShapeDtypeStruct((n, KV_W), BF16)] * 2
        + [jax.ShapeDtypeStruct((KV_W, n), BF16)] * 2,
        compiler_params=_params("parallel"),
        name="nsa_project_prompt",
    )(x, g.reshape(1, D_MODEL), *weights)


def _nsa_project_sample(x, g, w):
    n = x.shape[0]
    return pl.pallas_call(
        _proj_sample_kernel,
        out_shape=[jax.ShapeDtypeStruct((N_Q, n), BF16), jax.ShapeDtypeStruct((LANES, n), F32)]
        + [jax.ShapeDtypeStruct((n, 2 * KV_W), F32)] * 3,
        compiler_params=pltpu.CompilerParams(vmem_limit_bytes=VMEM_LIMIT),
        name="nsa_project_sample",
    )(x, g.reshape(1, D_MODEL), w["wqt"], w["wgt"], w["wkv"])


def _compress_out(acck, accv, w2k_ref, w2vt_ref, kc_ref, vct_ref):
    kc_ref[...] = _dot(jax.nn.gelu(acck).astype(BF16), w2k_ref[...]).astype(BF16)
    vct_ref[...] = _dot_nt(w2vt_ref[...], jax.nn.gelu(accv).astype(BF16)).astype(BF16)


def _compress_kernel(x_ref, pe_ref, w1k_ref, w1v_ref, w2k_ref, w2vt_ref, kc_ref, vct_ref, acck_ref, accv_ref):
    l = pl.program_id(1)

    @pl.when(l == 0)
    def _():
        acck_ref[...] = jnp.zeros_like(acck_ref)
        accv_ref[...] = jnp.zeros_like(accv_ref)

    xb = x_ref[...] + pe_ref[...]
    acck_ref[...] += _dot(xb[:, :KV_W].astype(BF16), w1k_ref[...])
    accv_ref[...] += _dot(xb[:, KV_W:].astype(BF16), w1v_ref[...])

    @pl.when(l == L_CMP - 1)
    def _():
        _compress_out(acck_ref[...], accv_ref[...], w2k_ref, w2vt_ref, kc_ref, vct_ref)


PAGES_PER_STEP = 32
BLOCKS_PER_PAGE = PAGE_SIZE // L_CMP


def _compress_weights(pe, w1k, w2k, w1v, w2v):
    eye = jnp.eye(N_KV_HEADS, dtype=F32)
    bd1 = lambda w1: jax.vmap(lambda w: jnp.kron(eye, w))(w1).astype(BF16)
    pe_row = jnp.tile(pe, (1, 2 * N_KV_HEADS))
    r = jnp.arange(2 * PAGE_SIZE)
    src = (r % 8 // BLOCKS_PER_PAGE) * PAGE_SIZE + (r % BLOCKS_PER_PAGE) * L_CMP + r // 8
    perm = (src[:, None] == r[None, :]).astype(BF16)
    return dict(pe_row=pe_row.reshape(L_CMP, 1, 2 * KV_W), pe_page_t=jnp.tile(pe_row, (BLOCKS_PER_PAGE, 1)).T, perm=perm,
                w1k=bd1(w1k), w1v=bd1(w1v), w2k=jnp.kron(eye, w2k).astype(BF16), w2vt=jnp.kron(eye, w2v).T.astype(BF16))


def _compress(x, w, *, tm):
    m = x.shape[0]
    tm = min(tm, m)
    return pl.pallas_call(
        _compress_kernel,
        grid=(m // tm, L_CMP),
        in_specs=[
            pl.BlockSpec((tm, 2 * KV_W), lambda i, l: (i, l)),
            pl.BlockSpec((None, 1, 2 * KV_W), lambda i, l: (l, 0, 0)),
            pl.BlockSpec((None, KV_W, KV_W), lambda i, l: (l, 0, 0)),
            pl.BlockSpec((None, KV_W, KV_W), lambda i, l: (l, 0, 0)),
            pl.BlockSpec((KV_W, KV_W), lambda i, l: (0, 0)),
            pl.BlockSpec((KV_W, KV_W), lambda i, l: (0, 0)),
        ],
        out_specs=[pl.BlockSpec((tm, KV_W), lambda i, l: (i, 0)), pl.BlockSpec((KV_W, tm), lambda i, l: (0, i))],
        out_shape=[jax.ShapeDtypeStruct((m, KV_W), BF16), jax.ShapeDtypeStruct((KV_W, m), BF16)],
        scratch_shapes=[pltpu.VMEM((tm, KV_W), F32)] * 2,
        compiler_params=_params("parallel", "arbitrary"),
        name="compress",
    )(x, w["pe_row"], w["w1k"], w["w1v"], w["w2k"], w["w2vt"])


def _compress_pages_body(refs, n_pages):
    pages = refs[:n_pages]
    pe_ref, perm_ref, w1k_ref, w1v_ref, w2k_ref, w2vt_ref, kc_ref, vct_ref = refs[n_pages:]
    pe = pe_ref[...]
    perm = perm_ref[...]
    by_l = []
    for p in range(0, n_pages, 2):
        pair_t = jnp.concatenate([(pages[p][...] + pe).astype(BF16), (pages[p + 1][...] + pe).astype(BF16)], axis=1)
        by_l.append(_dot_nt(perm, pair_t))
    acck = jnp.zeros((n_pages * BLOCKS_PER_PAGE, KV_W), F32)
    accv = jnp.zeros((n_pages * BLOCKS_PER_PAGE, KV_W), F32)
    for l in range(L_CMP):
        xl = jnp.concatenate([y[l * 8:(l + 1) * 8] for y in by_l], axis=0).astype(BF16)
        acck = acck + _dot(xl[:, :KV_W], w1k_ref[l])
        accv = accv + _dot(xl[:, KV_W:], w1v_ref[l])
    _compress_out(acck, accv, w2k_ref, w2vt_ref, kc_ref, vct_ref)


def _compress_paged_kernel(pt_ref, *refs, n_pages):
    del pt_ref
    _compress_pages_body(refs, n_pages)


def _compress_seq_kernel(*refs, n_pages):
    _compress_pages_body(refs, n_pages)


def _compress_consts(w):
    return [w["pe_page_t"], w["perm"], w["w1k"], w["w1v"], w["w2k"], w["w2vt"]]


def _compress_paged(cache, page_table, w):
    b, n_pages = page_table.shape
    pps = min(PAGES_PER_STEP, n_pages)
    steps = n_pages // pps
    m_step = pps * BLOCKS_PER_PAGE
    m = b * n_pages * BLOCKS_PER_PAGE
    page = lambda j: pl.BlockSpec((None, 2 * KV_W, PAGE_SIZE), lambda bi, s, pt: (pt[bi, s * pps + j], 0, 0))
    full = lambda a: pl.BlockSpec(a.shape, lambda bi, s, pt: (0,) * a.ndim)
    consts = _compress_consts(w)
    return pl.pallas_call(
        functools.partial(_compress_paged_kernel, n_pages=pps),
        grid_spec=pltpu.PrefetchScalarGridSpec(
            num_scalar_prefetch=1,
            grid=(b, steps),
            in_specs=[page(j) for j in range(pps)] + [full(a) for a in consts],
            out_specs=[pl.BlockSpec((m_step, KV_W), lambda bi, s, pt: (bi * steps + s, 0)),
                       pl.BlockSpec((KV_W, m_step), lambda bi, s, pt: (0, bi * steps + s))],
        ),
        out_shape=[jax.ShapeDtypeStruct((m, KV_W), BF16), jax.ShapeDtypeStruct((KV_W, m), BF16)],
        compiler_params=_params("parallel", "arbitrary"),
        name="compress_paged",
    )(page_table, *([cache] * pps), *consts)


def _compress_seq(rows_t, w):
    b, _, t = rows_t.shape
    per_seq = t // PAGE_SIZE
    n_pages = b * per_seq
    pps = min(PAGES_PER_STEP, n_pages)
    m_step = pps * BLOCKS_PER_PAGE
    m = n_pages * BLOCKS_PER_PAGE
    page = lambda j: pl.BlockSpec((None, 2 * KV_W, PAGE_SIZE),
                                  lambda s: ((s * pps + j) // per_seq, 0, (s * pps + j) % per_seq))
    full = lambda a: pl.BlockSpec(a.shape, lambda s: (0,) * a.ndim)
    consts = _compress_consts(w)
    return pl.pallas_call(
        functools.partial(_compress_seq_kernel, n_pages=pps),
        grid=(n_pages // pps,),
        in_specs=[page(j) for j in range(pps)] + [full(a) for a in consts],
        out_specs=[pl.BlockSpec((m_step, KV_W), lambda s: (s, 0)), pl.BlockSpec((KV_W, m_step), lambda s: (0, s))],
        out_shape=[jax.ShapeDtypeStruct((m, KV_W), BF16), jax.ShapeDtypeStruct((KV_W, m), BF16)],
        compiler_params=_params("parallel"),
        name="compress_seq",
    )(*([rows_t] * pps), *consts)


TQ_PROMPT = 256
TQ_SAMPLE = 128
KCH = 512
ONES_ROWS = 16
BIG = 3.0e38


def _head_queries(q_t, kv):
    zero = jnp.zeros((HEAD_DIM, q_t.shape[1]), BF16)
    cols = []
    for g in range(GROUP):
        h = kv * GROUP + g
        qg = q_t[h * HEAD_DIM:(h + 1) * HEAD_DIM, :]
        cols.append(jnp.concatenate([qg, zero] if kv % 2 == 0 else [zero, qg], axis=0))
    return jnp.concatenate(cols, axis=1)


def _slab(kv):
    return slice((kv // 2) * LANES, (kv // 2 + 1) * LANES)


def _head(kv):
    return slice(kv * HEAD_DIM, (kv + 1) * HEAD_DIM)


def _cmp_branch(kc, vc_t, x, keep):
    s = jnp.where(keep, _dot(kc, x), NEG)
    e = jnp.exp2(s - jnp.max(s, axis=0, keepdims=True))
    l = jnp.sum(e, axis=0, keepdims=True)
    e = jnp.where(keep, e, 0.0)
    return _dot(vc_t, e.astype(BF16)) / l, e / l


def _group_sum(p):
    tq = p.shape[1] // GROUP
    out = p[:, 0:tq]
    for g in range(1, GROUP):
        out = out + p[:, g * tq:(g + 1) * tq]
    return out


def _pair_sum(p):
    n = p.shape[0] // 2
    pair = (lax.broadcasted_iota(jnp.int32, (n, 2 * n), 1) // 2 == lax.broadcasted_iota(jnp.int32, (n, 2 * n), 0))
    return jnp.dot(pair.astype(F32), p, precision=lax.Precision.HIGHEST, preferred_element_type=F32)


def _with_ones(v_t):
    return jnp.concatenate([v_t, jnp.ones((ONES_ROWS, v_t.shape[1]), BF16)], axis=0)


def _topn_rows(score, n):
    nb = score.shape[0]
    row = lax.broadcasted_iota(jnp.int32, score.shape, 0).astype(F32)
    picked = jnp.zeros(score.shape, F32)
    ids = []
    for _ in range(n):
        top = jnp.max(score, axis=0, keepdims=True)
        first = jnp.min(jnp.where(score == top, row, float(nb)), axis=0, keepdims=True)
        hit = row == first
        picked = jnp.where(hit, 1.0, picked)
        score = jnp.where(hit, -jnp.inf, score)
        ids.append(first)
    return picked, ids


def _masked_scores(s, cap):
    tq = cap.shape[1]
    s = jnp.concatenate([jnp.minimum(s[:, g * tq:(g + 1) * tq], cap) for g in range(GROUP)], axis=1)
    return s, jnp.max(s, axis=0, keepdims=True)


def _softmax_pv(s, top, v_t):
    o = _dot(_with_ones(v_t), jnp.exp2(s - top).astype(BF16))
    return o[:HEAD_DIM] / o[HEAD_DIM:HEAD_DIM + 1]


def _gate_mix(g_t, kv, branches):
    tq = g_t.shape[1]
    out = []
    for g in range(GROUP):
        h = kv * GROUP + g
        c = slice(g * tq, (g + 1) * tq)
        acc = None
        for br, o in branches:
            term = g_t[br * N_HEADS + h:br * N_HEADS + h + 1, :] * o[:, c]
            acc = term if acc is None else acc + term
        out.append(acc)
    return out


def _attn_prompt_kernel(x_ref, qt_ref, gt_ref, kc_ref, vct_ref, ks_ref, vst_ref, kw_ref, *rest, t_len, tq):
    win_blocks = WINDOW // tq + 1
    cols = GROUP * tq
    vwt_refs = rest[:win_blocks]
    wout_ref, o_ref, m_ref, acc_ref = rest[win_blocks:]
    t0 = pl.program_id(1) * tq
    n_sel = t_len // L_SEL
    n_cmp = t_len // L_CMP
    t_tok = t0 + lax.broadcasted_iota(jnp.int32, (1, tq), 1)
    t_col = t0 + lax.broadcasted_iota(jnp.int32, (1, cols), 1) % tq
    t_heads = t0 + lax.broadcasted_iota(jnp.int32, (1, N_KV_HEADS * tq), 1) % tq
    r_c = lax.broadcasted_iota(jnp.int32, (n_cmp, 1), 0)
    keep_c = (r_c + 1) * L_CMP - 1 <= t_col
    blk = lax.broadcasted_iota(jnp.int32, (n_sel, N_KV_HEADS * tq), 0)
    cur = t_heads // L_SEL
    forced = (blk == 0) | (blk == cur) | (blk == cur - 1)
    valid = blk <= cur
    key_in_blk = lax.broadcasted_iota(jnp.int32, (L_SEL, 1), 0)
    win_pos = t0 - WINDOW + lax.broadcasted_iota(jnp.int32, (win_blocks * tq, 1), 0)
    dist = t_tok - win_pos
    cap_w = jnp.where((dist >= 0) & (dist < WINDOW) & (win_pos >= 0), BIG, NEG)
    n_chunks = t_len // KCH
    last_chunk = (t0 + tq - 1) // KCH
    q_t = qt_ref[...]
    g_t = gt_ref[...]

    o_cmp, imp = [], []
    for kv in range(N_KV_HEADS):
        o_c, p = _cmp_branch(kc_ref[0, :, _slab(kv)], vct_ref[0, _head(kv), :], _head_queries(q_t, kv), keep_c)
        o_cmp.append(o_c)
        imp.append(_group_sum(p))
    score = jnp.where(forced, FORCE_BONUS, jnp.where(valid, _pair_sum(jnp.concatenate(imp, axis=1)), -FORCE_BONUS))
    picked, _ = _topn_rows(score, min(TOP_N, n_sel))
    sel_cap = jnp.where(valid & (picked > 0.5), BIG, NEG)

    def key_caps(kv, first_key, n_keys):
        rows = []
        for j in range(first_key // L_SEL, (first_key + n_keys) // L_SEL):
            causal = j * L_SEL + key_in_blk <= t_tok
            rows.append(jnp.where(causal, jnp.broadcast_to(sel_cap[j:j + 1, kv * tq:(kv + 1) * tq], (L_SEL, tq)), NEG))
        return jnp.concatenate(rows, axis=0)

    win = []
    for kv in range(N_KV_HEADS):
        x = _head_queries(q_t, kv)
        s_w = []
        for j in range(win_blocks):
            start = pl.multiple_of(jnp.maximum(t0 - WINDOW + j * tq, 0), tq)
            s_w.append(_dot(kw_ref[0, pl.ds(start, tq), _slab(kv)], x))
        win.append(_masked_scores(jnp.concatenate(s_w, axis=0), cap_w))
    o_win = [_softmax_pv(*win[kv], jnp.concatenate([r[_head(kv), :] for r in vwt_refs], axis=1))
             for kv in range(N_KV_HEADS)]

    m_ref[...] = jnp.full(m_ref.shape, NEG, F32)
    acc_ref[...] = jnp.zeros_like(acc_ref)

    def chunk(c, n_keys):
        keys = slice(c * KCH, c * KCH + n_keys)
        scores, maxes = [], []
        for kv in range(N_KV_HEADS):
            s = _dot(ks_ref[0, keys, _slab(kv)], _head_queries(q_t, kv))
            cap = key_caps(kv, c * KCH, n_keys)
            s = jnp.concatenate([jnp.minimum(s[:, g * tq:(g + 1) * tq], cap) for g in range(GROUP)], axis=1)
            scores.append(s)
            maxes.append(jnp.maximum(m_ref[kv], jnp.max(s, axis=0, keepdims=True)))
        for kv in range(N_KV_HEADS):
            e = jnp.exp2(scores[kv] - maxes[kv]).astype(BF16)
            acc_ref[kv] = (jnp.exp2(m_ref[kv] - maxes[kv]) * acc_ref[kv]
                           + _dot(_with_ones(vst_ref[_head(kv), keys]), e))
            m_ref[kv] = maxes[kv]

    chunk(0, KCH)
    assert tq >= KCH or KCH == 2 * tq
    ends_chunk = (t0 + tq) % KCH == 0
    for c in range(1, n_chunks):
        pl.when((c < last_chunk) | ((c == last_chunk) & ends_chunk))(functools.partial(chunk, c, KCH))
        if tq < KCH:
            pl.when((c == last_chunk) & jnp.logical_not(ends_chunk))(functools.partial(chunk, c, tq))

    heads = []
    for kv in range(N_KV_HEADS):
        acc = acc_ref[kv]
        o_s = acc[:HEAD_DIM] / acc[HEAD_DIM:HEAD_DIM + 1]
        heads += _gate_mix(g_t, kv, ((0, o_cmp[kv]), (1, o_s), (2, o_win[kv])))
    o = jnp.concatenate(heads, axis=0).T.astype(BF16)
    o_ref[0] = x_ref[0] + _dot(o, wout_ref[...])


def _attn_prompt(x, q_t, g_t, kc, vc_t, ks, vs_t, kw, vw_t, w_out, *, tq=TQ_PROMPT):
    b, t, _ = x.shape
    nt = t // tq
    n_cmp = t // L_CMP
    win_blocks = WINDOW // tq + 1
    cols = GROUP * tq
    tile = lambda w: pl.BlockSpec((1, tq, w), lambda bi, i: (bi, i, 0))
    tile_t = lambda h: pl.BlockSpec((h, tq), lambda bi, i: (0, bi * nt + i))
    win_t = lambda j: pl.BlockSpec((KV_W, tq), lambda bi, i: (0, bi * nt + jnp.maximum(i - (win_blocks - 1) + j, 0)))
    return pl.pallas_call(
        functools.partial(_attn_prompt_kernel, t_len=t, tq=tq),
        grid=(b, nt),
        in_specs=[tile(D_MODEL), tile_t(N_Q), tile_t(LANES),
                  pl.BlockSpec((1, n_cmp, KV_W), lambda bi, i: (bi, 0, 0)),
                  pl.BlockSpec((1, KV_W, n_cmp), lambda bi, i: (bi, 0, 0)),
                  pl.BlockSpec((1, t, KV_W), lambda bi, i: (bi, 0, 0)),
                  pl.BlockSpec((KV_W, t), lambda bi, i: (0, bi)),
                  pl.BlockSpec((1, t, KV_W), lambda bi, i: (bi, 0, 0))]
        + [win_t(j) for j in range(win_blocks)]
        + [pl.BlockSpec((N_Q, D_MODEL), lambda bi, i: (0, 0))],
        out_specs=tile(D_MODEL),
        out_shape=jax.ShapeDtypeStruct((b, t, D_MODEL), F32),
        scratch_shapes=[pltpu.VMEM((N_KV_HEADS, 1, cols), F32),
                        pltpu.VMEM((N_KV_HEADS, HEAD_DIM + ONES_ROWS, cols), F32)],
        compiler_params=_params("parallel", "arbitrary"),
        name="attn_prompt",
    )(x, q_t, g_t, kc, vc_t, ks, vs_t, kw, *([vw_t] * win_blocks), w_out)


NEW_ROWS = 16


def _attn_sample_kernel(qt_ref, gt_ref, kc_ref, vct_ref, win_ref, new_ref, part_ref, idx_ref, *, past, n_blk, half):
    tq = TQ_SAMPLE
    t_tok = past + lax.broadcasted_iota(jnp.int32, (1, tq), 1)
    t_col = past + lax.broadcasted_iota(jnp.int32, (1, GROUP * tq), 1) % tq
    t_heads = past + lax.broadcasted_iota(jnp.int32, (1, N_KV_HEADS * tq), 1) % tq
    r_c = lax.broadcasted_iota(jnp.int32, (2 * half, 1), 0)
    keep_c = (r_c < 2 * n_blk) & ((r_c + 1) * L_CMP - 1 <= t_col)
    blk = lax.broadcasted_iota(jnp.int32, (half, N_KV_HEADS * tq), 0)
    cur = t_heads // L_SEL
    forced = (blk == 0) | (blk == cur) | (blk == cur - 1)
    valid = blk <= cur
    wb = win_ref.shape[2]
    win_pos = past - wb + lax.broadcasted_iota(jnp.int32, (wb + NEW_ROWS, 1), 0)
    dist = t_tok - win_pos
    keep_w = (dist >= 0) & (dist < WINDOW) & (win_pos >= 0)
    q_t = qt_ref[0]
    g_t = gt_ref[0]
    o_cmp, imp = [], []
    for kv in range(N_KV_HEADS):
        o_c, p = _cmp_branch(kc_ref[0, :, _slab(kv)], vct_ref[0, _head(kv), :], _head_queries(q_t, kv), keep_c)
        o_cmp.append(o_c)
        imp.append(_group_sum(p))
    score = jnp.where(forced, FORCE_BONUS, jnp.where(valid, _pair_sum(jnp.concatenate(imp, axis=1)), -FORCE_BONUS))
    score = jnp.where(blk < n_blk, score, -jnp.inf)
    _, ids = _topn_rows(score, min(TOP_N, n_blk))
    for j, first in enumerate(ids):
        for kv in range(N_KV_HEADS):
            idx_ref[0, kv, j:j + 1, :] = first[:, kv * tq:(kv + 1) * tq].astype(jnp.int32)

    heads = []
    for kv in range(N_KV_HEADS):
        x = _head_queries(q_t, kv)
        mine = slice((kv % 2) * HEAD_DIM, (kv % 2 + 1) * HEAD_DIM)
        v_slab = slice(KV_W + (kv // 2) * LANES, KV_W + (kv // 2 + 1) * LANES)
        s = jnp.concatenate([_dot_tn(win_ref[0, _head(kv), :], x[mine].astype(F32)),
                             _dot(new_ref[0, :, _slab(kv)].astype(BF16), x)], axis=0)
        s = jnp.concatenate([jnp.where(keep_w, s[:, g * tq:(g + 1) * tq], NEG) for g in range(GROUP)], axis=1)
        e = jnp.exp2(s - jnp.max(s, axis=0, keepdims=True))
        p_w = e / jnp.sum(e, axis=0, keepdims=True)
        v_t = win_ref[0, KV_W + kv * HEAD_DIM:KV_W + (kv + 1) * HEAD_DIM, :]
        o_w = _dot(v_t.astype(BF16), p_w[:wb].astype(BF16)) + _dot_tn(new_ref[0, :, v_slab], p_w[wb:])[mine]
        heads += _gate_mix(g_t, kv, ((0, o_cmp[kv]), (2, o_w)))
    part_ref[0] = jnp.concatenate(heads, axis=0)


def _attn_sample(q_t, g_t, kc, vc_t, win, new, *, past, n_blk):
    b = q_t.shape[0]
    half = kc.shape[1] // 2
    n_top = min(TOP_N, n_blk)
    blk3 = lambda a: pl.BlockSpec((1,) + a.shape[1:], lambda bi: (bi, 0, 0))
    return pl.pallas_call(
        functools.partial(_attn_sample_kernel, past=past, n_blk=n_blk, half=half),
        grid=(b,),
        in_specs=[blk3(q_t), blk3(g_t), blk3(kc), blk3(vc_t), blk3(win), blk3(new)],
        out_specs=[pl.BlockSpec((1, N_Q, TQ_SAMPLE), lambda bi: (bi, 0, 0)),
                   pl.BlockSpec((1, N_KV_HEADS, n_top, TQ_SAMPLE), lambda bi: (bi, 0, 0, 0))],
        out_shape=[jax.ShapeDtypeStruct((b, N_Q, TQ_SAMPLE), F32),
                   jax.ShapeDtypeStruct((b, N_KV_HEADS, n_top, TQ_SAMPLE), jnp.int32)],
        compiler_params=_params("parallel"),
        name="attn_sample",
    )(q_t, g_t, kc, vc_t, win, new)


Q_ROWS = 16


def _sel_sample_kernel(pt_ref, bid_ref, q_ref, new_ref, cache_ref, o_ref, slab_buf, sem, *,
                       past, n_past_blk, n_top):
    n_slab = N_KV_HEADS * n_top
    n_t = pl.num_programs(1)
    step = pl.program_id(0) * n_t + pl.program_id(1)
    n_steps = pl.num_programs(0) * n_t
    blocks_per_page = PAGE_SIZE // L_SEL

    def slab_copies(at_step):
        slot = at_step % 2
        copies = []
        for i in range(n_slab):
            past_blk = jnp.minimum(bid_ref[at_step * n_slab + i], n_past_blk - 1)
            page = pt_ref[at_step // n_t, past_blk // blocks_per_page]
            copies.append(pltpu.make_async_copy(cache_ref.at[page, :, i // n_top], slab_buf.at[slot, i], sem.at[slot]))
        return copies

    @pl.when(step == 0)
    def _():
        for copy in slab_copies(step):
            copy.start()

    @pl.when(step + 1 < n_steps)
    def _():
        for copy in slab_copies(step + 1):
            copy.start()

    for copy in slab_copies(step):
        copy.wait()

    slot = step % 2
    t_pos = past + pl.program_id(1)
    lane = lax.broadcasted_iota(jnp.int32, (1, PAGE_SIZE), 1)
    for kv in range(N_KV_HEADS):
        k_t, v_t, keep = [], [], []
        for j in range(n_top):
            bid = bid_ref[step * n_slab + kv * n_top + j]
            from_past = bid < n_past_blk
            first_lane = jnp.where(from_past, (bid % blocks_per_page) * L_SEL, 0)
            k_t.append(jnp.where(from_past, slab_buf[slot, kv * n_top + j, 0], new_ref[0, kv]).astype(BF16))
            v_t.append(jnp.where(from_past, slab_buf[slot, kv * n_top + j, 1], new_ref[1, kv]).astype(BF16))
            in_blk = lane - first_lane
            keep.append((in_blk >= 0) & (in_blk < L_SEL) & (bid * L_SEL + in_blk <= t_pos))
        s = jnp.where(jnp.concatenate(keep, axis=1), _dot(q_ref[kv], jnp.concatenate(k_t, axis=1)), NEG)
        e = jnp.exp2(s - jnp.max(s, axis=1, keepdims=True))
        p = e / jnp.sum(e, axis=1, keepdims=True)
        o_ref[kv] = _dot_nt(p.astype(BF16), jnp.concatenate(v_t, axis=1))


def _sel_sample(page_table, blk_ids, q, cache, new_slabs, *, past, n_past_blk, n_top):
    b, t, n_kv = q.shape[:3]
    return pl.pallas_call(
        functools.partial(_sel_sample_kernel, past=past, n_past_blk=n_past_blk, n_top=n_top),
        grid_spec=pltpu.PrefetchScalarGridSpec(
            num_scalar_prefetch=2,
            grid=(b, t),
            in_specs=[pl.BlockSpec((None, None, n_kv, Q_ROWS, HEAD_DIM), lambda bi, ti, pt, bid: (bi, ti, 0, 0, 0)),
                      pl.BlockSpec((None, 2, n_kv, HEAD_DIM, PAGE_SIZE), lambda bi, ti, pt, bid: (bi, 0, 0, 0, 0)),
                      pl.BlockSpec(memory_space=pl.ANY)],
            out_specs=pl.BlockSpec((None, None, n_kv, Q_ROWS, HEAD_DIM), lambda bi, ti, pt, bid: (bi, ti, 0, 0, 0)),
            scratch_shapes=[pltpu.VMEM((2, n_kv * n_top, 2, HEAD_DIM, PAGE_SIZE), F32), pltpu.SemaphoreType.DMA((2,))],
        ),
        out_shape=jax.ShapeDtypeStruct((b, t, n_kv, Q_ROWS, HEAD_DIM), F32),
        compiler_params=_params("arbitrary", "arbitrary"),
        name="sel_sample",
    )(page_table, blk_ids, q, new_slabs, cache)


def _out_sample_kernel(x_ref, part_ref, os_ref, gs_ref, wout_ref, o_ref):
    o = (part_ref[...] + gs_ref[...] * os_ref[...]).astype(BF16)
    o_ref[...] = x_ref[...] + _dot(o, wout_ref[...])


def _out_sample(x, part, o_s, g_s, w_out):
    return pl.pallas_call(
        _out_sample_kernel,
        out_shape=jax.ShapeDtypeStruct(x.shape, F32),
        compiler_params=pltpu.CompilerParams(vmem_limit_bytes=VMEM_LIMIT),
        name="out_sample",
    )(x, part, o_s, g_s, w_out)


def kernel(x_prompt, x_sample, state_conv, cache_cmp_kv, cache_sel_kv, cache_win_kv, page_table, norm_ffa, w_ffa_gu, w_ffa_down, norm_mix, norm_ffb, w_ffb_gu, w_ffb_down, w_conv_in, w_conv, w_conv_out, w_nsa_in, pe_cmp, w_cmp_k1, w_cmp_k2, w_cmp_v1, w_cmp_v2, w_nsa_out, norm_final):
    bp, tp, _ = x_prompt.shape
    bs, ts, _ = x_sample.shape
    n_p, n_s = bp * tp, bs * ts
    xp = x_prompt.reshape(n_p, D_MODEL)
    xs = x_sample.reshape(n_s, D_MODEL)

    def ffn_pair(xp, xs, g, w_gu, w_down, layer, final=False):
        w_gu, w_down = _to_bf16(w_gu, layer), _to_bf16(w_down, layer)
        return (_ffn(xp, g, w_gu, w_down, norm_final, tm=512, final=final),
                _ffn(xs, g, w_gu, w_down, norm_final, tm=n_s, final=final))

    xp, xs = ffn_pair(xp, xs, norm_ffa[0], w_ffa_gu, w_ffa_down, 0)
    w_in, w_out = _to_bf16(w_conv_in, 0), _to_bf16(w_conv_out, 0)
    xp3, conv_p = _conv_prompt(xp.reshape(bp, tp, D_MODEL), norm_mix[0], w_in, w_conv[0], w_out)
    xs3, conv_s = _conv_sample(xs.reshape(bs, ts, D_MODEL), state_conv[0], norm_mix[0], w_in, w_conv[0], w_out)
    xp, xs = ffn_pair(xp3.reshape(n_p, D_MODEL), xs3.reshape(n_s, D_MODEL), norm_ffb[0], w_ffb_gu, w_ffb_down, 0)

    xp, xs = ffn_pair(xp, xs, norm_ffa[1], w_ffa_gu, w_ffa_down, 1)
    proj_w = _nsa_weights(w_nsa_in[0])
    cmp_w = _compress_weights(pe_cmp[0], w_cmp_k1[0], w_cmp_k2[0], w_cmp_v1[0], w_cmp_v2[0])
    w_out = _to_bf16(w_nsa_out, 0)
    kv_shape = (2, N_KV_HEADS, HEAD_DIM)
    row_w = 2 * KV_W

    q_t, g_t, kvc_t, kvs_t, kvw_t, ks, kw, vs_t, vw_t = _nsa_project_prompt(xp, norm_mix[1], proj_w, seq_len=tp, tm=512)
    n_cmp = tp // L_CMP
    kc, vc_t = _compress_seq(kvc_t, cmp_w)
    kc = kc.reshape(bp, n_cmp, KV_W)
    vc_t = vc_t.reshape(KV_W, bp, n_cmp).transpose(1, 0, 2)
    seq = lambda a: a.reshape(bp, tp, a.shape[-1])
    xp3 = _attn_prompt(seq(xp), q_t, g_t, kc, vc_t, seq(ks), vs_t, seq(kw), vw_t, w_out)
    by_row = lambda a: a.reshape((bp,) + kv_shape + (a.shape[-1],)).transpose(0, 4, 1, 2, 3)[None]
    cmp_p = by_row(kvc_t)
    sel_p = by_row(kvs_t)
    win_p = by_row(kvw_t[:, :, tp - min(WINDOW, tp):])

    q_t, g_t, kvc, kvs, kvw = _nsa_project_sample(xs, norm_mix[1], proj_w)
    n_pages = page_table.shape[1]
    past = n_pages * PAGE_SIZE
    n_new = -(-ts // L_SEL) * L_SEL
    n_past_blk = past // L_SEL
    n_blk = n_past_blk + n_new // L_SEL
    n_top = min(TOP_N, n_blk)
    lanes = lambda a: jnp.pad(a.reshape(a.shape[0], bs, ts).transpose(1, 0, 2), ((0, 0), (0, 0), (0, TQ_SAMPLE - ts)))
    by_channel = lambda c: c.transpose(0, 2, 3, 4, 1)
    kc_past, vct_past = _compress_paged(by_channel(cache_cmp_kv[0]).reshape(-1, row_w, PAGE_SIZE), page_table, cmp_w)
    new_c = jnp.pad(kvc.reshape(bs, ts, row_w), ((0, 0), (0, n_new - ts), (0, 0)))
    kc_new, vct_new = _compress(new_c.reshape(bs * n_new // L_CMP, L_CMP * row_w), cmp_w, tm=bs * n_new // L_CMP)
    half = -(-n_blk // (LANES // 2)) * (LANES // 2)
    pad = 2 * half - 2 * n_blk
    kc_all = jnp.pad(jnp.concatenate([kc_past.reshape(bs, -1, KV_W), kc_new.reshape(bs, -1, KV_W)], axis=1),
                     ((0, 0), (0, pad), (0, 0)))
    vct_all = jnp.pad(jnp.concatenate([vct_past.reshape(KV_W, bs, -1), vct_new.reshape(KV_W, bs, -1)], axis=2),
                      ((0, 0), (0, 0), (0, pad)))
    wb = cache_win_kv.shape[2]
    new_w = jnp.pad(kvw.reshape(bs, ts, row_w), ((0, 0), (0, NEW_ROWS - ts), (0, 0)))
    part, idx = _attn_sample(lanes(q_t), lanes(g_t), kc_all, vct_all.transpose(1, 0, 2),
                             by_channel(cache_win_kv[0]).reshape(bs, row_w, wb), new_w, past=past, n_blk=n_blk)
    bid = idx[..., :ts].transpose(0, 3, 1, 2)
    q5 = q_t.reshape(N_KV_HEADS, GROUP, HEAD_DIM, bs, ts).transpose(3, 4, 0, 1, 2)
    q5 = jnp.pad(q5, ((0, 0),) * 3 + ((0, Q_ROWS - GROUP), (0, 0)))
    new_s = jnp.pad(kvs.reshape((bs, ts) + kv_shape).transpose(0, 2, 3, 4, 1), ((0, 0),) * 4 + ((0, PAGE_SIZE - ts),))
    o_s = _sel_sample(page_table, bid.reshape(-1), q5, by_channel(cache_sel_kv[0]), new_s,
                      past=past, n_past_blk=n_past_blk, n_top=n_top)
    o_s = o_s[:, :, :, :GROUP].reshape(n_s, N_Q)
    g_s = jnp.repeat(g_t[N_HEADS:2 * N_HEADS].T, HEAD_DIM, axis=1)
    part = part[:, :, :ts].transpose(0, 2, 1).reshape(n_s, N_Q)
    xs = _out_sample(xs, part, o_s, g_s, w_out)
    cmp_s = kvc.reshape((1, bs, ts) + kv_shape)
    sel_s = kvs.reshape((1, bs, ts) + kv_shape)
    win_s = jnp.concatenate([cache_win_kv[0], kvw.reshape((bs, ts) + kv_shape)], axis=1)[None, :, ts:]

    xp, xs = ffn_pair(xp3.reshape(n_p, D_MODEL), xs, norm_ffb[1], w_ffb_gu, w_ffb_down, 1, final=True)
    return (xp.reshape(bp, tp, D_MODEL), xs.reshape(bs, ts, D_MODEL),
            conv_p[None], conv_s[None], cmp_p, cmp_s, sel_p, sel_s, win_p, win_s)
```
